```python
import math
import jax, jax.numpy as jnp
from jax import lax
import numpy as np

D_MODEL = 1024
BATCH = 8
SEQ = 16384
DEPTH = 4

HEAD_DIM = 64
BLOCK = 128
DIL_GROUPS = ((128, 1), (512, 4), (2048, 16))
N_GROUPS = 3
A_HEADS_PER_GROUP = 4
A_HEADS = N_GROUPS * A_HEADS_PER_GROUP
A_WIDTH = A_HEADS * HEAD_DIM
A_OUT = A_HEADS_PER_GROUP * HEAD_DIM
SB_HEADS = 4
SB_HEAD_DIM = 128
SB_WIDTH = SB_HEADS * SB_HEAD_DIM
RET_HEADS = 4
RET_DK = 64
RET_DV = 128
RET_QK_WIDTH = RET_HEADS * RET_DK
RET_V_WIDTH = RET_HEADS * RET_DV
RET_CHUNK = 128
ROPE_BASE = 10000.0
REL_BUCKETS = 32
REL_MAX_DIST = 2048
N_BRANCH = 3
D_FF = 2816
CONV_WIDTH = 3
EPS = 1e-6

IN_SIZES = (A_WIDTH,) * 3 + (SB_WIDTH,) * 3 + (RET_QK_WIDTH,) * 2 + (RET_V_WIDTH,) * 2 + (N_BRANCH * D_MODEL,)
IN_WIDTH = 3 * A_WIDTH + 3 * SB_WIDTH + 2 * RET_QK_WIDTH + 2 * RET_V_WIDTH + N_BRANCH * D_MODEL

kernel_name = "hybrid_dilated_stickbreak_retention_convffn"


def rms_norm(x, w):
    xf = x.astype(jnp.float32)
    y = xf * lax.rsqrt(jnp.mean(xf * xf, axis=-1, keepdims=True) + EPS)
    return (y * w.astype(jnp.float32)).astype(x.dtype)


def t5_causal_bucket(dist):
    max_exact = REL_BUCKETS // 2
    n = np.maximum(dist, 0)
    large = max_exact + (np.log(np.maximum(n, 1) / max_exact) / np.log(REL_MAX_DIST / max_exact)
                         * (REL_BUCKETS - max_exact)).astype(np.int64)
    large = np.minimum(large, REL_BUCKETS - 1)
    return np.where(n < max_exact, n, large).astype(np.int32)


def dilated_window_attention(q, k, v, bias_hd, window, dilation):
    b, s, h, hd = q.shape
    n_steps = window // dilation
    span = dilation * BLOCK
    sp = -(-s // span) * span
    L = sp // dilation
    nb = L // BLOCK

    def to_blocks(t):
        t = jnp.pad(t, ((0, 0), (0, sp - s), (0, 0), (0, 0))).reshape(b, L, dilation, h, hd)
        return t.transpose(0, 2, 1, 3, 4).reshape(b, dilation, nb, BLOCK, h, hd)

    def with_prev(t):
        prev = jnp.pad(t, ((0, 0), (0, 0), (1, 0), (0, 0), (0, 0), (0, 0)))[:, :, :-1]
        return jnp.concatenate([prev, t], axis=3)

    qb = to_blocks(q)
    kw = with_prev(to_blocks(k))
    vw = with_prev(to_blocks(v))

    a_idx = np.arange(BLOCK)[:, None]
    c_idx = np.arange(2 * BLOCK)[None, :]
    steps = a_idx + BLOCK - c_idx
    in_band = (steps >= 0) & (steps <= n_steps)
    valid = in_band[None] & ((np.arange(nb)[:, None, None] > 0) | (c_idx[None] >= BLOCK))
    bucket = t5_causal_bucket(steps * dilation)
    bias = jnp.take(bias_hd.astype(jnp.float32), jnp.asarray(bucket), axis=0).transpose(2, 0, 1)

    logits = jnp.einsum('bdnqhc,bdnkhc->bdnhqk', qb, kw).astype(jnp.float32) * (hd ** -0.5) + bias
    logits = jnp.where(valid[None, None, :, None], logits, -jnp.inf)
    lse = jax.nn.logsumexp(logits, axis=-1)
    p = jnp.exp(logits - lse[..., None])
    out = jnp.einsum('bdnhqk,bdnkhc->bdnqhc', p.astype(v.dtype), vw)

    out = out.reshape(b, dilation, L, h, hd).transpose(0, 2, 1, 3, 4).reshape(b, sp, h, hd)[:, :s]
    lse = lse.transpose(0, 1, 2, 4, 3).reshape(b, dilation, L, h).transpose(0, 2, 1, 3).reshape(b, sp, h)[:, :s]
    return out, lse


def stick_breaking_attention(q, k, v):
    b, s, h, hd = q.shape
    nb = s // BLOCK
    scale = hd ** -0.5
    kb = k.reshape(b, nb, BLOCK, h, hd)
    vb = v.reshape(b, nb, BLOCK, h, hd)
    tri_in = jnp.asarray(np.arange(BLOCK)[:, None] > np.arange(BLOCK)[None, :], jnp.float32)
    diag_mask = np.arange(BLOCK)[None, :] < np.arange(BLOCK)[:, None]
    outs = []
    for i in range(nb):
        n = i + 1
        qi = q[:, i * BLOCK:(i + 1) * BLOCK]
        z = jnp.einsum('bqhc,bnkhc->bhqnk', qi, kb[:, :n]).astype(jnp.float32) * scale
        mask_np = np.ones((BLOCK, n, BLOCK), dtype=bool)
        mask_np[:, -1, :] = diag_mask
        mask = jnp.asarray(mask_np)
        log_keep = jnp.where(mask, jax.nn.log_sigmoid(-z), 0.0)
        later_in = jnp.einsum('bhqnk,kj->bhqnj', log_keep, tri_in)
        tri_blk = jnp.asarray(np.arange(n)[:, None] > np.arange(n)[None, :], jnp.float32)
        later_blk = jnp.einsum('bhqm,mn->bhqn', jnp.sum(log_keep, axis=-1), tri_blk)
        log_w = jnp.where(mask, log_keep + z + later_in + later_blk[..., None], -jnp.inf)
        w = jnp.exp(log_w)
        outs.append(jnp.einsum('bhqnk,bnkhc->bqhc', w.astype(v.dtype), vb[:, :n]))
    return jnp.concatenate(outs, axis=1)


def rotary(x, pos):
    half = x.shape[-1] // 2
    inv = ROPE_BASE ** (-jnp.arange(half, dtype=jnp.float32) / half)
    ang = pos.astype(jnp.float32)[:, None] * inv[None, :]
    cos = jnp.cos(ang)[None, :, None, :]
    sin = jnp.sin(ang)[None, :, None, :]
    x1 = x[..., :half].astype(jnp.float32)
    x2 = x[..., half:].astype(jnp.float32)
    return jnp.concatenate([x1 * cos - x2 * sin, x1 * sin + x2 * cos], axis=-1)


def retention(q, k, v):
    b, s, h, dk = q.shape
    dv = v.shape[-1]
    C = RET_CHUNK
    nc = s // C
    log_gamma = jnp.log1p(-jnp.exp2(-5.0 - jnp.arange(h, dtype=jnp.float32)))
    pos = jnp.arange(s, dtype=jnp.int32)
    qc = rotary(q, pos).reshape(b, nc, C, h, dk)
    kc = (rotary(k, pos) * (dk ** -0.5)).reshape(b, nc, C, h, dk)
    vc = v.astype(jnp.float32).reshape(b, nc, C, h, dv)

    n = jnp.arange(C, dtype=jnp.float32)
    diff = n[:, None] - n[None, :]
    decay_intra = jnp.where(diff >= 0, jnp.exp(diff[None] * log_gamma[:, None, None]), 0.0)
    intra = jnp.einsum('bnqhd,bnkhd->bnhqk', qc, kc) * decay_intra
    o_intra = jnp.einsum('bnhqk,bnkhe->bnqhe', intra, vc)

    zeta = jnp.exp((C - 1 - n)[:, None] * log_gamma[None, :])
    kv = jnp.einsum('bnkhd,bnkhe->bnhde', kc * zeta[None, None, :, :, None], vc)
    chunk_decay = jnp.exp(C * log_gamma)[:, None, None]

    def step(r, kv_i):
        return r * chunk_decay + kv_i, r

    _, r_prev = lax.scan(step, jnp.zeros((b, h, dk, dv), jnp.float32), kv.transpose(1, 0, 2, 3, 4))
    r_prev = r_prev.transpose(1, 0, 2, 3, 4)
    xi = jnp.exp((n + 1)[:, None] * log_gamma[None, :])
    o_cross = jnp.einsum('bnqhd,bnhde->bnqhe', qc * xi[None, None, :, :, None], r_prev)
    return (o_intra + o_cross).reshape(b, s, h, dv)


def conv_ffn(h, w_up, w_gate, conv_w, conv_b, w_down):
    u = h @ w_up
    u = lax.conv_general_dilated(u, conv_w[:, None, :].astype(u.dtype), window_strides=(1,),
                                 padding=((CONV_WIDTH - 1, 0),),
                                 dimension_numbers=('NWC', 'WIO', 'NWC'),
                                 feature_group_count=D_FF) + conv_b
    return (jax.nn.gelu(u) * (h @ w_gate)) @ w_down


def _fwd_setup_inputs(seed: int = 0) -> dict:
    key = jax.random.key(seed)
    ks = jax.random.split(key, 20)
    f32 = jnp.float32
    nrm = lambda k, shape, fan: jax.random.normal(k, shape, f32) * (fan ** -0.5)
    return {
        "x": jax.random.normal(ks[0], (BATCH, SEQ, D_MODEL), f32),
        "rel_bias": jax.random.normal(ks[1], (REL_BUCKETS, A_HEADS), f32) * 0.5,
        "norm_mix_w": 1.0 + 0.01 * jax.random.normal(ks[2], (DEPTH, D_MODEL), f32),
        "w_in": nrm(ks[3], (DEPTH, D_MODEL, IN_WIDTH), D_MODEL),
        "b_gate": 0.1 * jax.random.normal(ks[4], (DEPTH, N_BRANCH, D_MODEL), f32),
        "ret_norm_w": 1.0 + 0.01 * jax.random.normal(ks[5], (DEPTH, RET_V_WIDTH), f32),
        "w_proj_a": nrm(ks[6], (DEPTH, A_OUT, D_MODEL), A_OUT),
        "w_proj_b": nrm(ks[7], (DEPTH, SB_WIDTH, D_MODEL), SB_WIDTH),
        "w_proj_c": nrm(ks[8], (DEPTH, RET_V_WIDTH, D_MODEL), RET_V_WIDTH),
        "w_out": nrm(ks[9], (DEPTH, D_MODEL, D_MODEL), D_MODEL),
        "norm_ffn_w": 1.0 + 0.01 * jax.random.normal(ks[10], (DEPTH, D_MODEL), f32),
        "w_up": nrm(ks[11], (DEPTH, D_MODEL, D_FF), D_MODEL),
        "w_gate": nrm(ks[12], (DEPTH, D_MODEL, D_FF), D_MODEL),
        "conv_w": nrm(ks[13], (DEPTH, CONV_WIDTH, D_FF), CONV_WIDTH),
        "conv_b": 0.01 * jax.random.normal(ks[14], (DEPTH, D_FF), f32),
        "w_down": nrm(ks[15], (DEPTH, D_FF, D_MODEL), D_FF),
        "final_norm_w": 1.0 + 0.01 * jax.random.normal(ks[16], (D_MODEL,), f32),
    }


def _fwd_reference(x, rel_bias, norm_mix_w, w_in, b_gate, ret_norm_w, w_proj_a, w_proj_b, w_proj_c,
              w_out, norm_ffn_w, w_up, w_gate, conv_w, conv_b, w_down, final_norm_w):
    b, s, _ = x.shape
    offsets = np.cumsum(IN_SIZES)[:-1].tolist()
    for l in range(DEPTH):
        h = rms_norm(x, norm_mix_w[l])
        proj = h @ w_in[l]
        (a_q, a_k, a_v, sb_q, sb_k, sb_v, r_q, r_k, r_v, r_g, gate_in) = jnp.split(proj, offsets, axis=-1)

        a_q = a_q.reshape(b, s, N_GROUPS, A_HEADS_PER_GROUP, HEAD_DIM)
        a_k = a_k.reshape(b, s, N_GROUPS, A_HEADS_PER_GROUP, HEAD_DIM)
        a_v = a_v.reshape(b, s, N_GROUPS, A_HEADS_PER_GROUP, HEAD_DIM)
        outs, lses = [], []
        for g, (window, dilation) in enumerate(DIL_GROUPS):
            bias_g = rel_bias[:, g * A_HEADS_PER_GROUP:(g + 1) * A_HEADS_PER_GROUP]
            o_g, lse_g = dilated_window_attention(a_q[:, :, g], a_k[:, :, g], a_v[:, :, g], bias_g, window, dilation)
            outs.append(o_g)
            lses.append(lse_g)
        wts = jax.nn.softmax(jnp.stack(lses), axis=0)
        y_a = jnp.einsum('gbsh,gbshc->bshc', wts.astype(x.dtype), jnp.stack(outs)).reshape(b, s, A_OUT)

        y_b = stick_breaking_attention(sb_q.reshape(b, s, SB_HEADS, SB_HEAD_DIM),
                                       sb_k.reshape(b, s, SB_HEADS, SB_HEAD_DIM),
                                       sb_v.reshape(b, s, SB_HEADS, SB_HEAD_DIM)).reshape(b, s, SB_WIDTH)

        o_r = retention(r_q.reshape(b, s, RET_HEADS, RET_DK), r_k.reshape(b, s, RET_HEADS, RET_DK),
                        r_v.reshape(b, s, RET_HEADS, RET_DV))
        mu = jnp.mean(o_r, axis=-1, keepdims=True)
        var = jnp.mean(jnp.square(o_r - mu), axis=-1, keepdims=True)
        o_r = ((o_r - mu) * lax.rsqrt(var + EPS)).reshape(b, s, RET_V_WIDTH) * ret_norm_w[l].astype(jnp.float32)
        y_c = jax.nn.silu(r_g) * o_r.astype(x.dtype)

        gates = jax.nn.sigmoid(gate_in.reshape(b, s, N_BRANCH, D_MODEL) + b_gate[l])
        merged = (gates[:, :, 0] * (y_a @ w_proj_a[l]) + gates[:, :, 1] * (y_b @ w_proj_b[l])
                  + gates[:, :, 2] * (y_c @ w_proj_c[l]))
        x = x + merged @ w_out[l]

        h = rms_norm(x, norm_ffn_w[l])
        x = x + conv_ffn(h, w_up[l], w_gate[l], conv_w[l], conv_b[l], w_down[l])
    return rms_norm(x, final_norm_w)


import jax as _jax
import jax.numpy as _jnp

TWIN_FORMAT = 'train_step'
FWD_PARAMS = ['x', 'rel_bias', 'norm_mix_w', 'w_in', 'b_gate', 'ret_norm_w', 'w_proj_a', 'w_proj_b', 'w_proj_c', 'w_out', 'norm_ffn_w', 'w_up', 'w_gate', 'conv_w', 'conv_b', 'w_down', 'final_norm_w']
TWIN_WEIGHTS = ['rel_bias', 'norm_mix_w', 'w_in', 'b_gate', 'ret_norm_w', 'w_proj_a', 'w_proj_b', 'w_proj_c', 'w_out', 'norm_ffn_w', 'w_up', 'w_gate', 'conv_w', 'conv_b', 'w_down', 'final_norm_w']
TWIN_DIFF_INPUT = 'x'
TWIN_INPUTS = ['x', 'rel_bias', 'norm_mix_w', 'w_in', 'b_gate', 'ret_norm_w', 'w_proj_a', 'w_proj_b', 'w_proj_c', 'w_out', 'norm_ffn_w', 'w_up', 'w_gate', 'conv_w', 'conv_b', 'w_down', 'final_norm_w', 'loss_target', 'm_rel_bias', 'm_norm_mix_w', 'm_w_in', 'm_b_gate', 'm_ret_norm_w', 'm_w_proj_a', 'm_w_proj_b', 'm_w_proj_c', 'm_w_out', 'm_norm_ffn_w', 'm_w_up', 'm_w_gate', 'm_conv_w', 'm_conv_b', 'm_w_down', 'm_final_norm_w', 'v_rel_bias', 'v_norm_mix_w', 'v_w_in', 'v_b_gate', 'v_ret_norm_w', 'v_w_proj_a', 'v_w_proj_b', 'v_w_proj_c', 'v_w_out', 'v_norm_ffn_w', 'v_w_up', 'v_w_gate', 'v_conv_w', 'v_conv_b', 'v_w_down', 'v_final_norm_w']
TWIN_OUTPUTS = ['loss', 'grad_x', 'grad_rel_bias', 'grad_norm_mix_w', 'grad_w_in', 'grad_b_gate', 'grad_ret_norm_w', 'grad_w_proj_a', 'grad_w_proj_b', 'grad_w_proj_c', 'grad_w_out', 'grad_norm_ffn_w', 'grad_w_up', 'grad_w_gate', 'grad_conv_w', 'grad_conv_b', 'grad_w_down', 'grad_final_norm_w', 'delta_rel_bias', 'delta_norm_mix_w', 'delta_w_in', 'delta_b_gate', 'delta_ret_norm_w', 'delta_w_proj_a', 'delta_w_proj_b', 'delta_w_proj_c', 'delta_w_out', 'delta_norm_ffn_w', 'delta_w_up', 'delta_w_gate', 'delta_conv_w', 'delta_conv_b', 'delta_w_down', 'delta_final_norm_w', 'new_m_rel_bias', 'new_m_norm_mix_w', 'new_m_w_in', 'new_m_b_gate', 'new_m_ret_norm_w', 'new_m_w_proj_a', 'new_m_w_proj_b', 'new_m_w_proj_c', 'new_m_w_out', 'new_m_norm_ffn_w', 'new_m_w_up', 'new_m_w_gate', 'new_m_conv_w', 'new_m_conv_b', 'new_m_w_down', 'new_m_final_norm_w', 'new_v_rel_bias', 'new_v_norm_mix_w', 'new_v_w_in', 'new_v_b_gate', 'new_v_ret_norm_w', 'new_v_w_proj_a', 'new_v_w_proj_b', 'new_v_w_proj_c', 'new_v_w_out', 'new_v_norm_ffn_w', 'new_v_w_up', 'new_v_w_gate', 'new_v_conv_w', 'new_v_conv_b', 'new_v_w_down', 'new_v_final_norm_w']
TWIN_LEAF_KINDS = {'loss': 'loss', 'grad_x': 'grad_x', 'grad_rel_bias': 'grad_w', 'grad_norm_mix_w': 'grad_w', 'grad_w_in': 'grad_w', 'grad_b_gate': 'grad_w', 'grad_ret_norm_w': 'grad_w', 'grad_w_proj_a': 'grad_w', 'grad_w_proj_b': 'grad_w', 'grad_w_proj_c': 'grad_w', 'grad_w_out': 'grad_w', 'grad_norm_ffn_w': 'grad_w', 'grad_w_up': 'grad_w', 'grad_w_gate': 'grad_w', 'grad_conv_w': 'grad_w', 'grad_conv_b': 'grad_w', 'grad_w_down': 'grad_w', 'grad_final_norm_w': 'grad_w', 'delta_rel_bias': 'delta_w', 'delta_norm_mix_w': 'delta_w', 'delta_w_in': 'delta_w', 'delta_b_gate': 'delta_w', 'delta_ret_norm_w': 'delta_w', 'delta_w_proj_a': 'delta_w', 'delta_w_proj_b': 'delta_w', 'delta_w_proj_c': 'delta_w', 'delta_w_out': 'delta_w', 'delta_norm_ffn_w': 'delta_w', 'delta_w_up': 'delta_w', 'delta_w_gate': 'delta_w', 'delta_conv_w': 'delta_w', 'delta_conv_b': 'delta_w', 'delta_w_down': 'delta_w', 'delta_final_norm_w': 'delta_w', 'new_m_rel_bias': 'new_m', 'new_m_norm_mix_w': 'new_m', 'new_m_w_in': 'new_m', 'new_m_b_gate': 'new_m', 'new_m_ret_norm_w': 'new_m', 'new_m_w_proj_a': 'new_m', 'new_m_w_proj_b': 'new_m', 'new_m_w_proj_c': 'new_m', 'new_m_w_out': 'new_m', 'new_m_norm_ffn_w': 'new_m', 'new_m_w_up': 'new_m', 'new_m_w_gate': 'new_m', 'new_m_conv_w': 'new_m', 'new_m_conv_b': 'new_m', 'new_m_w_down': 'new_m', 'new_m_final_norm_w': 'new_m', 'new_v_rel_bias': 'new_v', 'new_v_norm_mix_w': 'new_v', 'new_v_w_in': 'new_v', 'new_v_b_gate': 'new_v', 'new_v_ret_norm_w': 'new_v', 'new_v_w_proj_a': 'new_v', 'new_v_w_proj_b': 'new_v', 'new_v_w_proj_c': 'new_v', 'new_v_w_out': 'new_v', 'new_v_norm_ffn_w': 'new_v', 'new_v_w_up': 'new_v', 'new_v_w_gate': 'new_v', 'new_v_conv_w': 'new_v', 'new_v_conv_b': 'new_v', 'new_v_w_down': 'new_v', 'new_v_final_norm_w': 'new_v'}


def _forward(args):
    return _fwd_reference(*[args[k] for k in FWD_PARAMS])


def _output_shape():
    def fwd():
        inp = _fwd_setup_inputs(0)
        return _fwd_reference(*[inp[k] for k in FWD_PARAMS])
    out = _jax.eval_shape(fwd)
    return out.shape, out.dtype

N_MICROBATCH = 1
ADAM_LR = 0.001
ADAM_B1 = 0.9
ADAM_B2 = 0.999
ADAM_EPS = 1e-08
ADAM_WD = 0.01
ADAM_STEP = 10
PER_EXAMPLE_BATCH_AXIS = {'x': 0, 'loss_target': 0}
SHARED_INPUTS = []
_WEIGHT_DTYPES = {'rel_bias': _jnp.float32, 'norm_mix_w': _jnp.float32, 'w_in': _jnp.float32, 'b_gate': _jnp.float32, 'ret_norm_w': _jnp.float32, 'w_proj_a': _jnp.float32, 'w_proj_b': _jnp.float32, 'w_proj_c': _jnp.float32, 'w_out': _jnp.float32, 'norm_ffn_w': _jnp.float32, 'w_up': _jnp.float32, 'w_gate': _jnp.float32, 'conv_w': _jnp.float32, 'conv_b': _jnp.float32, 'w_down': _jnp.float32, 'final_norm_w': _jnp.float32}
MOMENT_SCALE = {'rel_bias': 1.107242e-01, 'norm_mix_w': 3.124523e-01, 'w_in': 1.082423e-01, 'b_gate': 4.186760e-02, 'ret_norm_w': 1.781729e-01, 'w_proj_a': 4.565304e-02, 'w_proj_b': 1.347499e-01, 'w_proj_c': 1.216045e-01, 'w_out': 1.856418e-01, 'norm_ffn_w': 2.991589e-01, 'w_up': 1.247325e-01, 'w_gate': 1.205883e-01, 'conv_w': 1.234765e-01, 'conv_b': 1.192737e-01, 'w_down': 2.001677e-01, 'final_norm_w': 1.279710e+02}


def _to_microbatches(a, axis):
    t = _jnp.moveaxis(a, axis, 0)
    t = t.reshape((N_MICROBATCH, t.shape[0] // N_MICROBATCH) + t.shape[1:])
    return _jnp.moveaxis(t, 1, axis + 1)


def setup_inputs(seed: int = 0) -> dict:
    inp = _fwd_setup_inputs(seed)
    key = _jax.random.fold_in(_jax.random.key(seed), 7919)
    shape, _ = _output_shape()
    out = dict(inp)
    out["loss_target"] = _jax.random.normal(_jax.random.fold_in(key, 0), shape, _jnp.float32)
    for i, name in enumerate(TWIN_WEIGHTS):
        w = inp[name].astype(_jnp.float32)
        if MOMENT_SCALE is None:
            s = _jnp.sqrt(_jnp.mean(_jnp.square(w)) + 1e-30)
        else:
            s = MOMENT_SCALE[name]
        km, kv = _jax.random.split(_jax.random.fold_in(key, i + 1))
        out[name] = w
        out["m_" + name] = s * _jax.random.normal(km, w.shape, _jnp.float32)
        out["v_" + name] = (s * s) * _jax.random.uniform(kv, w.shape, _jnp.float32, 0.5, 1.5)
    if N_MICROBATCH > 1:
        for name, axis in PER_EXAMPLE_BATCH_AXIS.items():
            out[name] = _to_microbatches(out[name], axis)
    return {'x': out['x'], 'rel_bias': out['rel_bias'], 'norm_mix_w': out['norm_mix_w'], 'w_in': out['w_in'], 'b_gate': out['b_gate'], 'ret_norm_w': out['ret_norm_w'], 'w_proj_a': out['w_proj_a'], 'w_proj_b': out['w_proj_b'], 'w_proj_c': out['w_proj_c'], 'w_out': out['w_out'], 'norm_ffn_w': out['norm_ffn_w'], 'w_up': out['w_up'], 'w_gate': out['w_gate'], 'conv_w': out['conv_w'], 'conv_b': out['conv_b'], 'w_down': out['w_down'], 'final_norm_w': out['final_norm_w'], 'loss_target': out['loss_target'], 'm_rel_bias': out['m_rel_bias'], 'm_norm_mix_w': out['m_norm_mix_w'], 'm_w_in': out['m_w_in'], 'm_b_gate': out['m_b_gate'], 'm_ret_norm_w': out['m_ret_norm_w'], 'm_w_proj_a': out['m_w_proj_a'], 'm_w_proj_b': out['m_w_proj_b'], 'm_w_proj_c': out['m_w_proj_c'], 'm_w_out': out['m_w_out'], 'm_norm_ffn_w': out['m_norm_ffn_w'], 'm_w_up': out['m_w_up'], 'm_w_gate': out['m_w_gate'], 'm_conv_w': out['m_conv_w'], 'm_conv_b': out['m_conv_b'], 'm_w_down': out['m_w_down'], 'm_final_norm_w': out['m_final_norm_w'], 'v_rel_bias': out['v_rel_bias'], 'v_norm_mix_w': out['v_norm_mix_w'], 'v_w_in': out['v_w_in'], 'v_b_gate': out['v_b_gate'], 'v_ret_norm_w': out['v_ret_norm_w'], 'v_w_proj_a': out['v_w_proj_a'], 'v_w_proj_b': out['v_w_proj_b'], 'v_w_proj_c': out['v_w_proj_c'], 'v_w_out': out['v_w_out'], 'v_norm_ffn_w': out['v_norm_ffn_w'], 'v_w_up': out['v_w_up'], 'v_w_gate': out['v_w_gate'], 'v_conv_w': out['v_conv_w'], 'v_conv_b': out['v_conv_b'], 'v_w_down': out['v_w_down'], 'v_final_norm_w': out['v_final_norm_w']}


def _loss(weights, diff, rest, loss_target):
    with _jax.named_scope("forward"):
        args = {**rest, TWIN_DIFF_INPUT: diff, **{k: w.astype(_WEIGHT_DTYPES[k]) for k, w in weights.items()}}
        y = _forward(args)
    with _jax.named_scope("loss_head"):
        err = _jnp.square(y.astype(_jnp.float32) - loss_target)
        return 0.5 * _jnp.sum(_jnp.mean(err, axis=-1)) if err.ndim else 0.5 * err


def _adamw(w, g, m, v):
    m = ADAM_B1 * m + (1.0 - ADAM_B1) * g
    v = ADAM_B2 * v + (1.0 - ADAM_B2) * _jnp.square(g)
    m_hat = m / (1.0 - ADAM_B1 ** ADAM_STEP)
    v_hat = v / (1.0 - ADAM_B2 ** ADAM_STEP)
    delta = -ADAM_LR * (m_hat / (_jnp.sqrt(v_hat) + ADAM_EPS) + ADAM_WD * w)
    return delta, m, v


def reference(x, rel_bias, norm_mix_w, w_in, b_gate, ret_norm_w, w_proj_a, w_proj_b, w_proj_c, w_out, norm_ffn_w, w_up, w_gate, conv_w, conv_b, w_down, final_norm_w, loss_target, m_rel_bias, m_norm_mix_w, m_w_in, m_b_gate, m_ret_norm_w, m_w_proj_a, m_w_proj_b, m_w_proj_c, m_w_out, m_norm_ffn_w, m_w_up, m_w_gate, m_conv_w, m_conv_b, m_w_down, m_final_norm_w, v_rel_bias, v_norm_mix_w, v_w_in, v_b_gate, v_ret_norm_w, v_w_proj_a, v_w_proj_b, v_w_proj_c, v_w_out, v_norm_ffn_w, v_w_up, v_w_gate, v_conv_w, v_conv_b, v_w_down, v_final_norm_w):
    given = dict(x=x, rel_bias=rel_bias, norm_mix_w=norm_mix_w, w_in=w_in, b_gate=b_gate, ret_norm_w=ret_norm_w, w_proj_a=w_proj_a, w_proj_b=w_proj_b, w_proj_c=w_proj_c, w_out=w_out, norm_ffn_w=norm_ffn_w, w_up=w_up, w_gate=w_gate, conv_w=conv_w, conv_b=conv_b, w_down=w_down, final_norm_w=final_norm_w, loss_target=loss_target, m_rel_bias=m_rel_bias, m_norm_mix_w=m_norm_mix_w, m_w_in=m_w_in, m_b_gate=m_b_gate, m_ret_norm_w=m_ret_norm_w, m_w_proj_a=m_w_proj_a, m_w_proj_b=m_w_proj_b, m_w_proj_c=m_w_proj_c, m_w_out=m_w_out, m_norm_ffn_w=m_norm_ffn_w, m_w_up=m_w_up, m_w_gate=m_w_gate, m_conv_w=m_conv_w, m_conv_b=m_conv_b, m_w_down=m_w_down, m_final_norm_w=m_final_norm_w, v_rel_bias=v_rel_bias, v_norm_mix_w=v_norm_mix_w, v_w_in=v_w_in, v_b_gate=v_b_gate, v_ret_norm_w=v_ret_norm_w, v_w_proj_a=v_w_proj_a, v_w_proj_b=v_w_proj_b, v_w_proj_c=v_w_proj_c, v_w_out=v_w_out, v_norm_ffn_w=v_norm_ffn_w, v_w_up=v_w_up, v_w_gate=v_w_gate, v_conv_w=v_conv_w, v_conv_b=v_conv_b, v_w_down=v_w_down, v_final_norm_w=v_final_norm_w)
    weights = {n: given[n] for n in TWIN_WEIGHTS}
    shared = {n: given[n] for n in SHARED_INPUTS}
    per_example = {n: given[n] for n in ['x']}
    grad_fn = _jax.value_and_grad(_loss, argnums=(0, 1))

    def one_microbatch(ex, loss_target):
        ex = dict(ex)
        diff = ex.pop(TWIN_DIFF_INPUT)
        return grad_fn(weights, diff, {**shared, **ex}, loss_target)

    if N_MICROBATCH == 1:
        loss, (grad_w, grad_x) = one_microbatch(per_example, given["loss_target"])
    else:
        def body(carry, xs):
            loss_sum, grad_sum = carry
            l_k, (gw_k, gx_k) = one_microbatch(xs[0], xs[1])
            with _jax.named_scope("update"):
                return (loss_sum + l_k, _jax.tree.map(_jnp.add, grad_sum, gw_k)), gx_k

        init = (_jnp.zeros((), _jnp.float32), _jax.tree.map(_jnp.zeros_like, weights))
        (loss, grad_w), grad_x = _jax.lax.scan(body, init, (per_example, given["loss_target"]))
    with _jax.named_scope("update"):
        delta_w, new_m, new_v = {}, {}, {}
        for n in TWIN_WEIGHTS:
            delta_w[n], new_m[n], new_v[n] = _adamw(weights[n], grad_w[n], given["m_" + n], given["v_" + n])
    return (loss, grad_x, *[grad_w[n] for n in TWIN_WEIGHTS], *[delta_w[n] for n in TWIN_WEIGHTS],
            *[new_m[n] for n in TWIN_WEIGHTS], *[new_v[n] for n in TWIN_WEIGHTS])
```

```python
import functools

import numpy as np
import jax
import jax.numpy as jnp
from jax import lax
from jax.experimental import pallas as pl
from jax.experimental.pallas import tpu as pltpu

F32 = jnp.float32
BF16 = jnp.bfloat16

D_MODEL = 1024
DEPTH = 4
N_DEV = 8
BLOCK = 128
DIL_GROUPS = ((128, 1), (512, 4), (2048, 16))
A_HEADS_PER_GROUP = 4
HEAD_DIM = 64
A_GROUP_WIDTH = A_HEADS_PER_GROUP * HEAD_DIM
SB_HEADS = 4
SB_HEAD_DIM = 128
RET_HEADS = 4
RET_DK = 64
RET_DV = 128
ROPE_BASE = 10000.0
REL_BUCKETS = 32
REL_MAX_DIST = 2048
D_FF = 2816
EPS = 1e-6
LO_WIDTH = 3840
HI_WIDTH = 4608
A_SCALE = HEAD_DIM ** -0.5
SB_SCALE = SB_HEAD_DIM ** -0.5
RET_SCALE = RET_DK ** -0.5
NEG = -1e30

ADAM_LR = 0.001
ADAM_B1 = 0.9
ADAM_B2 = 0.999
ADAM_EPS = 1e-08
ADAM_WD = 0.01
ADAM_STEP = 10

VMEM_LIMIT_V7X = 56 * 1024 * 1024

_NT = (((1,), (1,)), ((), ()))
_NN = (((1,), (0,)), ((), ()))
_TN = (((0,), (0,)), ((), ()))


def _dot(a, b, dims):
    return lax.dot_general(a, b, dims, preferred_element_type=F32)


def _tile(n, cands):
    for c in cands:
        if n % c == 0:
            return c
    return n


def _params(*sem):
    return pltpu.CompilerParams(dimension_semantics=sem, vmem_limit_bytes=VMEM_LIMIT_V7X)


def _matmul(a, b, mode, name, out_dtype=F32, add=None):
    if mode == "tn":
        (K, M), N = a.shape, b.shape[1]
    elif mode == "nn":
        (M, K), N = a.shape, b.shape[1]
    else:
        (M, K), N = a.shape, b.shape[0]
    tm = _tile(M, (512, 384, 256, 128))
    tn = _tile(N, (1024, 768, 512, 384, 256, 128))
    tk = _tile(K, (1024, 768, 512, 384, 256, 128))
    nk = K // tk
    if mode == "tn":
        a_spec = pl.BlockSpec((tk, tm), lambda i, j, k: (k, i))
    else:
        a_spec = pl.BlockSpec((tm, tk), lambda i, j, k: (i, k))
    if mode == "nt":
        b_spec = pl.BlockSpec((tn, tk), lambda i, j, k: (j, k))
    else:
        b_spec = pl.BlockSpec((tk, tn), lambda i, j, k: (k, j))
    dims = {"nt": _NT, "nn": _NN, "tn": _TN}[mode]
    o_spec = pl.BlockSpec((tm, tn), lambda i, j, k: (i, j))
    has_add = add is not None

    def body(*refs):
        a_ref, b_ref = refs[0], refs[1]
        o_ref, acc = refs[-2], refs[-1]
        k = pl.program_id(2)

        @pl.when(k == 0)
        def _():
            acc[...] = jnp.zeros_like(acc)

        acc[...] += _dot(a_ref[...].astype(BF16), b_ref[...].astype(BF16), dims)

        @pl.when(k == nk - 1)
        def _():
            r = acc[...]
            if has_add:
                r = r + refs[2][...]
            o_ref[...] = r.astype(out_dtype)

    ins = [a, b] + ([add] if has_add else [])
    specs = [a_spec, b_spec] + ([o_spec] if has_add else [])
    return pl.pallas_call(
        body, name=name, grid=(M // tm, N // tn, nk),
        in_specs=specs, out_specs=o_spec,
        out_shape=jax.ShapeDtypeStruct((M, N), out_dtype),
        scratch_shapes=[pltpu.VMEM((tm, tn), F32)],
        compiler_params=_params("parallel", "parallel", "arbitrary"),
    )(*ins)


def _rms(x, w):
    return x * lax.rsqrt(jnp.mean(x * x, axis=-1, keepdims=True) + EPS) * w


def _rms_fwd(x, w, name):
    T = x.shape[0]
    tm = _tile(T, (512, 256, 128))

    def body(x_ref, w_ref, o_ref):
        o_ref[...] = _rms(x_ref[...], w_ref[...]).astype(BF16)

    return pl.pallas_call(
        body, name=name, grid=(T // tm,),
        in_specs=[pl.BlockSpec((tm, D_MODEL), lambda i: (i, 0)), pl.BlockSpec((1, D_MODEL), lambda i: (0, 0))],
        out_specs=pl.BlockSpec((tm, D_MODEL), lambda i: (i, 0)),
        out_shape=jax.ShapeDtypeStruct((T, D_MODEL), BF16),
        compiler_params=_params("parallel"),
    )(x, w.reshape(1, D_MODEL))


def _rms_bwd(x, w, dh, res, name):
    T = x.shape[0]
    tm = _tile(T, (512, 256, 128))

    def body(x_ref, w_ref, dh_ref, res_ref, dx_ref, dw_ref):
        _, vjp = jax.vjp(_rms, x_ref[...], w_ref[...])
        dx, dw = vjp(dh_ref[...])
        dx_ref[...] = dx + res_ref[...]

        @pl.when(pl.program_id(0) == 0)
        def _():
            dw_ref[...] = jnp.zeros_like(dw_ref)

        dw_ref[...] += dw

    row = pl.BlockSpec((tm, D_MODEL), lambda i: (i, 0))
    vec = pl.BlockSpec((1, D_MODEL), lambda i: (0, 0))
    return pl.pallas_call(
        body, name=name, grid=(T // tm,),
        in_specs=[row, vec, row, row], out_specs=[row, vec],
        out_shape=[jax.ShapeDtypeStruct((T, D_MODEL), F32), jax.ShapeDtypeStruct((1, D_MODEL), F32)],
        compiler_params=_params("arbitrary"),
    )(x, w.reshape(1, D_MODEL), dh, res)


def _loss_head(x, w, tgt, name):
    T = x.shape[0]
    tm = _tile(T, (512, 256, 128))

    def body(x_ref, w_ref, t_ref, l_ref, dx_ref, dw_ref):
        y, vjp = jax.vjp(_rms, x_ref[...], w_ref[...])
        e = y - t_ref[...]
        dx, dw = vjp(e * (1.0 / D_MODEL))
        dx_ref[...] = dx

        @pl.when(pl.program_id(0) == 0)
        def _():
            dw_ref[...] = jnp.zeros_like(dw_ref)
            l_ref[...] = jnp.zeros_like(l_ref)

        dw_ref[...] += dw
        l_ref[...] += 0.5 * jnp.sum(jnp.mean(e * e, axis=-1, keepdims=True))

    row = pl.BlockSpec((tm, D_MODEL), lambda i: (i, 0))
    vec = pl.BlockSpec((1, D_MODEL), lambda i: (0, 0))
    return pl.pallas_call(
        body, name=name, grid=(T // tm,),
        in_specs=[row, vec, row],
        out_specs=[pl.BlockSpec((8, 128), lambda i: (0, 0)), row, vec],
        out_shape=[jax.ShapeDtypeStruct((8, 128), F32), jax.ShapeDtypeStruct((T, D_MODEL), F32),
                   jax.ShapeDtypeStruct((1, D_MODEL), F32)],
        compiler_params=_params("arbitrary"),
    )(x, w.reshape(1, D_MODEL), tgt)


def _t5_bucket(dist):
    max_exact = REL_BUCKETS // 2
    n = np.maximum(dist, 0)
    large = max_exact + (np.log(np.maximum(n, 1) / max_exact) / np.log(REL_MAX_DIST / max_exact)
                         * (REL_BUCKETS - max_exact)).astype(np.int64)
    large = np.minimum(large, REL_BUCKETS - 1)
    return np.where(n < max_exact, n, large).astype(np.int32)


def _a_buckets():
    steps = np.arange(BLOCK)[:, None] + BLOCK - np.arange(2 * BLOCK)[None, :]
    out = []
    for window, dilation in DIL_GROUPS:
        in_band = (steps >= 0) & (steps <= window // dilation)
        out.append(np.where(in_band, _t5_bucket(steps * dilation), -1))
    return jnp.asarray(np.stack(out).astype(np.int32))


def _a_bias_tables(rel_bias, buckets):
    def body(rb_ref, bk_ref, o_ref):
        hh = pl.program_id(0)
        bk = bk_ref[0]
        acc = jnp.full((BLOCK, 2 * BLOCK), NEG, F32)
        for b in range(REL_BUCKETS):
            acc = jnp.where(bk == b, rb_ref[b, hh], acc)
        o_ref[0] = acc

    return pl.pallas_call(
        body, name="a_bias_tables", grid=(12,),
        in_specs=[pl.BlockSpec(memory_space=pltpu.SMEM),
                  pl.BlockSpec((1, BLOCK, 2 * BLOCK), lambda h: (h // 4, 0, 0))],
        out_specs=pl.BlockSpec((1, BLOCK, 2 * BLOCK), lambda h: (h, 0, 0)),
        out_shape=jax.ShapeDtypeStruct((12, BLOCK, 2 * BLOCK), F32),
        compiler_params=_params("parallel"),
    )(rel_bias, buckets)


def _a_bias_grad(dbias, buckets):
    def body(db_ref, bk_ref, o_ref):
        bk = bk_ref[0]
        db = db_ref[0]
        lane = lax.broadcasted_iota(jnp.int32, (8, 128), 1)
        acc = jnp.zeros((8, 128), F32)
        for b in range(REL_BUCKETS):
            acc = jnp.where(lane == b, jnp.sum(jnp.where(bk == b, db, 0.0)), acc)
        o_ref[0] = acc

    out = pl.pallas_call(
        body, name="a_bias_grad", grid=(12,),
        in_specs=[pl.BlockSpec((1, BLOCK, 2 * BLOCK), lambda h: (h, 0, 0)),
                  pl.BlockSpec((1, BLOCK, 2 * BLOCK), lambda h: (h // 4, 0, 0))],
        out_specs=pl.BlockSpec((1, 8, 128), lambda h: (h, 0, 0)),
        out_shape=jax.ShapeDtypeStruct((12, 8, 128), F32),
        compiler_params=_params("parallel"),
    )(dbias, buckets)
    return out[:, 0, :REL_BUCKETS].T


def _a_logits(q_ref, kp_ref, kc_ref, b_ref, h, ok):
    sl = slice(h * HEAD_DIM, (h + 1) * HEAD_DIM)
    qh = q_ref[:, sl]
    s = jnp.concatenate([_dot(qh, kp_ref[:, sl], _NT), _dot(qh, kc_ref[:, sl], _NT)], axis=1)
    s = s * A_SCALE + b_ref[h]
    return jnp.where(ok, s, NEG)


def _a_attn_fwd(proj_lo, bias_all, g, name):
    T = proj_lo.shape[0]
    d = DIL_GROUPS[g][1]
    L = T // d
    nb = L // BLOCK
    nlo = LO_WIDTH // A_GROUP_WIDTH
    pv = proj_lo.reshape(L, d * LO_WIDTH)

    def body(q_ref, kc_ref, kp_ref, vc_ref, vp_ref, b_ref, o_ref, l_ref):
        n = pl.program_id(1)
        col = lax.broadcasted_iota(jnp.int32, (BLOCK, 2 * BLOCK), 1)
        ok = col >= jnp.where(n > 0, 0, BLOCK)
        for h in range(A_HEADS_PER_GROUP):
            sl = slice(h * HEAD_DIM, (h + 1) * HEAD_DIM)
            s = _a_logits(q_ref, kp_ref, kc_ref, b_ref, h, ok)
            m = jnp.max(s, axis=-1, keepdims=True)
            p = jnp.exp(s - m)
            l = jnp.sum(p, axis=-1, keepdims=True)
            pb = p.astype(BF16)
            o = _dot(pb[:, :BLOCK], vp_ref[:, sl], _NN) + _dot(pb[:, BLOCK:], vc_ref[:, sl], _NN)
            o_ref[:, sl] = o / l
            l_ref[:, sl] = jnp.broadcast_to(m + jnp.log(l), (BLOCK, HEAD_DIM))

    blk = (BLOCK, A_GROUP_WIDTH)
    out_spec = pl.BlockSpec(blk, lambda r, n: (n, r))
    out, lse = pl.pallas_call(
        body, name=name, grid=(d, nb),
        in_specs=[pl.BlockSpec(blk, lambda r, n: (n, r * nlo + g)),
                  pl.BlockSpec(blk, lambda r, n: (n, r * nlo + 3 + g)),
                  pl.BlockSpec(blk, lambda r, n: (jnp.maximum(n - 1, 0), r * nlo + 3 + g)),
                  pl.BlockSpec(blk, lambda r, n: (n, r * nlo + 6 + g)),
                  pl.BlockSpec(blk, lambda r, n: (jnp.maximum(n - 1, 0), r * nlo + 6 + g)),
                  pl.BlockSpec((4, BLOCK, 2 * BLOCK), lambda r, n: (g, 0, 0))],
        out_specs=[out_spec, out_spec],
        out_shape=[jax.ShapeDtypeStruct((L, d * A_GROUP_WIDTH), F32)] * 2,
        compiler_params=_params("parallel", "arbitrary"),
    )(pv, pv, pv, pv, pv, bias_all)
    return out.reshape(T, A_GROUP_WIDTH), lse.reshape(T, A_GROUP_WIDTH)


def _a_attn_bwd(proj_lo, bias_all, out, lse, dout, dlse, g, name):
    T = proj_lo.shape[0]
    d = DIL_GROUPS[g][1]
    L = T // d
    nb = L // BLOCK
    nlo = LO_WIDTH // A_GROUP_WIDTH
    pv = proj_lo.reshape(L, d * LO_WIDTH)
    view = lambda t: t.reshape(L, d * A_GROUP_WIDTH)

    def body(q_ref, kc_ref, kp_ref, vc_ref, vp_ref, b_ref, o_ref, l_ref, do_ref, dl_ref,
             dq_ref, dk_ref, dv_ref, db_ref, ck, cv):
        r = pl.program_id(0)
        n = pl.program_id(1)

        @pl.when((r == 0) & (n == 0))
        def _():
            db_ref[...] = jnp.zeros_like(db_ref)

        @pl.when(n == 0)
        def _():
            ck[...] = jnp.zeros_like(ck)
            cv[...] = jnp.zeros_like(cv)

        @pl.when(n < nb)
        def _():
            col = lax.broadcasted_iota(jnp.int32, (BLOCK, 2 * BLOCK), 1)
            ok = col >= jnp.where(n > 0, 0, BLOCK)
            for h in range(A_HEADS_PER_GROUP):
                sl = slice(h * HEAD_DIM, (h + 1) * HEAD_DIM)
                s = _a_logits(q_ref, kp_ref, kc_ref, b_ref, h, ok)
                p = jnp.exp(s - l_ref[:, h * HEAD_DIM:h * HEAD_DIM + 1])
                do = do_ref[:, sl]
                dob = do.astype(BF16)
                delta = jnp.sum(do * o_ref[:, sl], axis=-1, keepdims=True)
                dl = jnp.sum(dl_ref[:, sl], axis=-1, keepdims=True)
                dp = jnp.concatenate([_dot(dob, vp_ref[:, sl], _NT), _dot(dob, vc_ref[:, sl], _NT)], axis=1)
                ds = p * (dp - delta + dl)
                db_ref[h] += ds
                dsb = (ds * A_SCALE).astype(BF16)
                pb = p.astype(BF16)
                qh = q_ref[:, sl]
                dq_ref[:, sl] = (_dot(dsb[:, :BLOCK], kp_ref[:, sl], _NN)
                                 + _dot(dsb[:, BLOCK:], kc_ref[:, sl], _NN)).astype(BF16)
                dk_ref[:, sl] = (ck[:, sl] + _dot(dsb[:, :BLOCK], qh, _TN)).astype(BF16)
                dv_ref[:, sl] = (cv[:, sl] + _dot(pb[:, :BLOCK], dob, _TN)).astype(BF16)
                ck[:, sl] = _dot(dsb[:, BLOCK:], qh, _TN)
                cv[:, sl] = _dot(pb[:, BLOCK:], dob, _TN)

        @pl.when(n == nb)
        def _():
            dk_ref[...] = ck[...].astype(BF16)
            dv_ref[...] = cv[...].astype(BF16)

    blk = (BLOCK, A_GROUP_WIDTH)
    nq = lambda n: jnp.minimum(n, nb - 1)
    prv = lambda n: jnp.maximum(jnp.minimum(n, nb - 1) - 1, 0)
    cur_o = pl.BlockSpec(blk, lambda r, n: (nq(n), r))
    lag_o = pl.BlockSpec(blk, lambda r, n: (jnp.maximum(n - 1, 0), r))
    dq, dk, dv, db = pl.pallas_call(
        body, name=name, grid=(d, nb + 1),
        in_specs=[pl.BlockSpec(blk, lambda r, n: (nq(n), r * nlo + g)),
                  pl.BlockSpec(blk, lambda r, n: (nq(n), r * nlo + 3 + g)),
                  pl.BlockSpec(blk, lambda r, n: (prv(n), r * nlo + 3 + g)),
                  pl.BlockSpec(blk, lambda r, n: (nq(n), r * nlo + 6 + g)),
                  pl.BlockSpec(blk, lambda r, n: (prv(n), r * nlo + 6 + g)),
                  pl.BlockSpec((4, BLOCK, 2 * BLOCK), lambda r, n: (g, 0, 0)),
                  cur_o, cur_o, cur_o, cur_o],
        out_specs=[cur_o, lag_o, lag_o, pl.BlockSpec((4, BLOCK, 2 * BLOCK), lambda r, n: (0, 0, 0))],
        out_shape=[jax.ShapeDtypeStruct((L, d * A_GROUP_WIDTH), BF16)] * 3
        + [jax.ShapeDtypeStruct((4, BLOCK, 2 * BLOCK), F32)],
        scratch_shapes=[pltpu.VMEM(blk, F32), pltpu.VMEM(blk, F32)],
        compiler_params=_params("arbitrary", "arbitrary"),
    )(pv, pv, pv, pv, pv, bias_all, view(out), view(lse), view(dout), view(dlse))
    return dq.reshape(T, -1), dk.reshape(T, -1), dv.reshape(T, -1), db


def _a_merge(outs, lses):
    m = jnp.maximum(jnp.maximum(lses[0], lses[1]), lses[2])
    e = [jnp.exp(l - m) for l in lses]
    inv = 1.0 / (e[0] + e[1] + e[2])
    return (e[0] * outs[0] + e[1] * outs[1] + e[2] * outs[2]) * inv


def _a_merge_fwd(outs, lses, name):
    T = outs[0].shape[0]
    tm = _tile(T, (512, 256, 128))

    def body(o0, o1, o2, l0, l1, l2, y_ref):
        y_ref[...] = _a_merge([o0[...], o1[...], o2[...]], [l0[...], l1[...], l2[...]]).astype(BF16)

    row = pl.BlockSpec((tm, A_GROUP_WIDTH), lambda i: (i, 0))
    return pl.pallas_call(
        body, name=name, grid=(T // tm,), in_specs=[row] * 6, out_specs=row,
        out_shape=jax.ShapeDtypeStruct((T, A_GROUP_WIDTH), BF16),
        compiler_params=_params("parallel"),
    )(*outs, *lses)


def _a_merge_bwd(outs, lses, dy, name):
    T = outs[0].shape[0]
    tm = _tile(T, (512, 256, 128))

    def body(o0, o1, o2, l0, l1, l2, dy_ref, *outs_ref):
        _, vjp = jax.vjp(_a_merge, [o0[...], o1[...], o2[...]], [l0[...], l1[...], l2[...]])
        do, dl = vjp(dy_ref[...])
        for ref, val in zip(outs_ref, list(do) + list(dl)):
            ref[...] = val

    row = pl.BlockSpec((tm, A_GROUP_WIDTH), lambda i: (i, 0))
    res = pl.pallas_call(
        body, name=name, grid=(T // tm,), in_specs=[row] * 7, out_specs=[row] * 6,
        out_shape=[jax.ShapeDtypeStruct((T, A_GROUP_WIDTH), F32)] * 6,
        compiler_params=_params("parallel"),
    )(*outs, *lses, dy)
    return res[:3], res[3:]


SB_TQ = 256
SB_Q_COL, SB_K_COL, SB_V_COL = 18, 22, 26


def _sb_tri():
    return jnp.asarray(np.arange(BLOCK)[:, None] > np.arange(BLOCK)[None, :], BF16)


def _log_keep(z):
    return -(jnp.maximum(z, 0.0) + jnp.log1p(jnp.exp(-jnp.abs(z))))


def _sb_fwd(proj_lo, name):
    T = proj_lo.shape[0]
    TQ = min(SB_TQ, T)
    nq = T // TQ
    nbq = TQ // BLOCK

    def body(q_ref, k_ref, v_ref, tri_ref, o_ref, c_ref, acc, cs):
        i = pl.program_id(1)
        q = q_ref[...]
        tri = tri_ref[...]
        acc[...] = jnp.zeros_like(acc)
        cs[...] = jnp.zeros_like(cs)
        row = i * TQ + lax.broadcasted_iota(jnp.int32, (TQ, BLOCK), 0)
        col = lax.broadcasted_iota(jnp.int32, (TQ, BLOCK), 1)

        def blk(n, masked):
            off = pl.multiple_of(n * BLOCK, BLOCK)
            kb = k_ref[pl.ds(off, BLOCK), :]
            vb = v_ref[pl.ds(off, BLOCK), :]
            z = _dot(q, kb, _NT) * SB_SCALE
            lk = _log_keep(z)
            if masked:
                valid = row > n * BLOCK + col
                lk = jnp.where(valid, lk, 0.0)
            w = jnp.exp(z + lk + _dot(lk.astype(BF16), tri, _NN) + cs[...])
            if masked:
                w = jnp.where(valid, w, 0.0)
            acc[...] += _dot(w.astype(BF16), vb, _NN)
            cs[...] += jnp.sum(lk, axis=1, keepdims=True)

        for j in reversed(range(nbq)):
            blk(i * nbq + j, True)

        def step(m, carry):
            blk(i * nbq - 1 - m, False)
            return carry

        lax.fori_loop(0, i * nbq, step, 0)
        o_ref[...] = acc[...]
        c_ref[...] = cs[...]

    qo = pl.BlockSpec((TQ, BLOCK), lambda h, i: (i, h))
    return pl.pallas_call(
        body, name=name, grid=(SB_HEADS, nq),
        in_specs=[pl.BlockSpec((TQ, BLOCK), lambda h, i: (i, SB_Q_COL + h)),
                  pl.BlockSpec((T, BLOCK), lambda h, i: (0, SB_K_COL + h)),
                  pl.BlockSpec((T, BLOCK), lambda h, i: (0, SB_V_COL + h)),
                  pl.BlockSpec((BLOCK, BLOCK), lambda h, i: (0, 0))],
        out_specs=[qo, qo],
        out_shape=[jax.ShapeDtypeStruct((T, SB_HEADS * BLOCK), F32)] * 2,
        scratch_shapes=[pltpu.VMEM((TQ, BLOCK), F32), pltpu.VMEM((TQ, BLOCK), F32)],
        compiler_params=_params("parallel", "arbitrary"),
    )(proj_lo, proj_lo, proj_lo, _sb_tri())


def _sb_bwd(proj_lo, ctot, dy, name):
    T = proj_lo.shape[0]
    TQ = min(SB_TQ, T)
    nq = T // TQ
    nbq = TQ // BLOCK

    def body(q_ref, k_ref, v_ref, c_ref, do_ref, tri_ref, dq_ref, dk_hbm, dv_hbm,
             dka, dva, dqa, cps, pgs, sem):
        h = pl.program_id(0)
        i = pl.program_id(1)

        @pl.when(i == 0)
        def _():
            dka[...] = jnp.zeros_like(dka)
            dva[...] = jnp.zeros_like(dva)

        q = q_ref[...]
        tri = tri_ref[...]
        dob = do_ref[...].astype(BF16)
        dqa[...] = jnp.zeros_like(dqa)
        cps[...] = jnp.zeros_like(cps)
        pgs[...] = jnp.zeros_like(pgs)
        row = i * TQ + lax.broadcasted_iota(jnp.int32, (TQ, BLOCK), 0)
        col = lax.broadcasted_iota(jnp.int32, (TQ, BLOCK), 1)

        def blk(n, masked):
            off = pl.multiple_of(n * BLOCK, BLOCK)
            kb = k_ref[pl.ds(off, BLOCK), :]
            vb = v_ref[pl.ds(off, BLOCK), :]
            z = _dot(q, kb, _NT) * SB_SCALE
            lk = _log_keep(z)
            if masked:
                valid = row > n * BLOCK + col
                lk = jnp.where(valid, lk, 0.0)
            cps[...] += jnp.sum(lk, axis=1, keepdims=True)
            lb = z + lk
            w = jnp.exp(lb + _dot(lk.astype(BF16), tri, _NN) + (c_ref[...] - cps[...]))
            if masked:
                w = jnp.where(valid, w, 0.0)
            g = w * _dot(dob, vb, _NT)
            dz = g - jnp.exp(lb) * (g + _dot(g.astype(BF16), tri, _NT) + pgs[...])
            if masked:
                dz = jnp.where(valid, dz, 0.0)
            pgs[...] += jnp.sum(g, axis=1, keepdims=True)
            dzb = (dz * SB_SCALE).astype(BF16)
            dqa[...] += _dot(dzb, kb, _NN)
            dka[pl.ds(off, BLOCK), :] += _dot(dzb, q, _TN)
            dva[pl.ds(off, BLOCK), :] += _dot(w.astype(BF16), dob, _TN)

        def step(n, carry):
            blk(n, False)
            return carry

        lax.fori_loop(0, i * nbq, step, 0)
        for j in range(nbq):
            blk(i * nbq + j, True)
        dq_ref[...] = dqa[...].astype(BF16)

        @pl.when(i == nq - 1)
        def _():
            ck = pltpu.make_async_copy(dka, dk_hbm.at[h], sem.at[0])
            cv = pltpu.make_async_copy(dva, dv_hbm.at[h], sem.at[1])
            ck.start()
            cv.start()
            ck.wait()
            cv.wait()

    qo = pl.BlockSpec((TQ, BLOCK), lambda h, i: (i, h))
    return pl.pallas_call(
        body, name=name, grid=(SB_HEADS, nq),
        in_specs=[pl.BlockSpec((TQ, BLOCK), lambda h, i: (i, SB_Q_COL + h)),
                  pl.BlockSpec((T, BLOCK), lambda h, i: (0, SB_K_COL + h)),
                  pl.BlockSpec((T, BLOCK), lambda h, i: (0, SB_V_COL + h)),
                  qo, qo,
                  pl.BlockSpec((BLOCK, BLOCK), lambda h, i: (0, 0))],
        out_specs=[qo, pl.BlockSpec(memory_space=pl.ANY), pl.BlockSpec(memory_space=pl.ANY)],
        out_shape=[jax.ShapeDtypeStruct((T, SB_HEADS * BLOCK), BF16),
                   jax.ShapeDtypeStruct((SB_HEADS, T, BLOCK), F32),
                   jax.ShapeDtypeStruct((SB_HEADS, T, BLOCK), F32)],
        scratch_shapes=[pltpu.VMEM((T, BLOCK), F32), pltpu.VMEM((T, BLOCK), F32),
                        pltpu.VMEM((TQ, BLOCK), F32), pltpu.VMEM((TQ, BLOCK), F32),
                        pltpu.VMEM((TQ, BLOCK), F32), pltpu.SemaphoreType.DMA((2,))],
        compiler_params=_params("arbitrary", "arbitrary"),
    )(proj_lo, proj_lo, proj_lo, ctot, dy, _sb_tri())


def _ret_tables(T):
    half = RET_DK // 2
    inv = ROPE_BASE ** (-jnp.arange(half, dtype=F32) / half)
    ang = jnp.arange(T, dtype=jnp.int32).astype(F32)[:, None] * inv[None, :]
    cos, sin = jnp.cos(ang), jnp.sin(ang)
    cos_t = jnp.tile(cos, (1, 2 * RET_HEADS))
    sin_t = jnp.tile(jnp.concatenate([-sin, sin], axis=1), (1, RET_HEADS))
    lg = np.log1p(-np.exp2(-5.0 - np.arange(RET_HEADS)))
    n = np.arange(BLOCK)
    diff = n[:, None] - n[None, :]
    dmat = np.where(diff >= 0, np.exp(np.minimum(diff, BLOCK)[None] * lg[:, None, None]), 0.0)
    zeta = np.repeat(np.exp((BLOCK - 1 - n)[:, None] * lg[None, :]), RET_DK, axis=1)
    xi = np.repeat(np.exp((n + 1)[:, None] * lg[None, :]), RET_DK, axis=1)
    chunk_decay = [float(v) for v in np.exp(BLOCK * lg)]
    return (cos_t, sin_t, jnp.asarray(dmat, F32), jnp.asarray(xi, F32), jnp.asarray(zeta, F32)), chunk_decay


def _swap_halves(x):
    lane = lax.broadcasted_iota(jnp.int32, x.shape, 1)
    lower = (lane % RET_DK) < (RET_DK // 2)
    w = x.shape[1]
    return jnp.where(lower, pltpu.roll(x, w - RET_DK // 2, 1), pltpu.roll(x, RET_DK // 2, 1))


def _gn_gate(o, rg, w):
    mu = jnp.mean(o, axis=-1, keepdims=True)
    xc = o - mu
    var = jnp.mean(xc * xc, axis=-1, keepdims=True)
    return (rg * jax.nn.sigmoid(rg)) * (xc * lax.rsqrt(var + EPS) * w)


def _ret_fwd(proj_hi, ret_norm_w, tables, chunk_decay, name):
    T = proj_hi.shape[0]
    nc = T // BLOCK
    cos_t, sin_t, dmat, xi, zeta = tables

    def body(rq_ref, rk_ref, rv_ref, rg_ref, w_ref, cos_ref, sin_ref, d_ref, xi_ref, ze_ref,
             y_ref, o_ref, st_ref, rs):
        @pl.when(pl.program_id(0) == 0)
        def _():
            rs[...] = jnp.zeros_like(rs)

        cos, sin = cos_ref[...], sin_ref[...]
        rq, rk = rq_ref[...], rk_ref[...]
        q = rq * cos + _swap_halves(rq) * sin
        k = (rk * cos + _swap_halves(rk) * sin) * RET_SCALE
        qb, kb = q.astype(BF16), k.astype(BF16)
        qx, kz = (q * xi_ref[...]).astype(BF16), (k * ze_ref[...]).astype(BF16)
        for h in range(RET_HEADS):
            sl = slice(h * RET_DK, (h + 1) * RET_DK)
            sv = slice(h * RET_DV, (h + 1) * RET_DV)
            vb = rv_ref[:, sv].astype(BF16)
            r = rs[h]
            st_ref[0, h] = r
            intra = _dot(qb[:, sl], kb[:, sl], _NT) * d_ref[h]
            o = _dot(intra.astype(BF16), vb, _NN) + _dot(qx[:, sl], r.astype(BF16), _NN)
            rs[h] = r * chunk_decay[h] + _dot(kz[:, sl], vb, _TN)
            o_ref[:, sv] = o
            y_ref[:, sv] = _gn_gate(o, rg_ref[:, sv], w_ref[:, sv]).astype(BF16)

    qk = (BLOCK, RET_HEADS * RET_DK)
    vv = (BLOCK, RET_HEADS * RET_DV)
    const = lambda shape: pl.BlockSpec(shape, lambda n: (0,) * len(shape))
    return pl.pallas_call(
        body, name=name, grid=(nc,),
        in_specs=[pl.BlockSpec(qk, lambda n: (n, 0)), pl.BlockSpec(qk, lambda n: (n, 1)),
                  pl.BlockSpec(vv, lambda n: (n, 1)), pl.BlockSpec(vv, lambda n: (n, 2)),
                  const((1, RET_HEADS * RET_DV)),
                  pl.BlockSpec(qk, lambda n: (n, 0)), pl.BlockSpec(qk, lambda n: (n, 0)),
                  const((RET_HEADS, BLOCK, BLOCK)), const(qk), const(qk)],
        out_specs=[pl.BlockSpec(vv, lambda n: (n, 0)), pl.BlockSpec(vv, lambda n: (n, 0)),
                   pl.BlockSpec((1, RET_HEADS, RET_DK, RET_DV), lambda n: (n, 0, 0, 0))],
        out_shape=[jax.ShapeDtypeStruct((T, RET_HEADS * RET_DV), BF16),
                   jax.ShapeDtypeStruct((T, RET_HEADS * RET_DV), F32),
                   jax.ShapeDtypeStruct((nc, RET_HEADS, RET_DK, RET_DV), F32)],
        scratch_shapes=[pltpu.VMEM((RET_HEADS, RET_DK, RET_DV), F32)],
        compiler_params=_params("arbitrary"),
    )(proj_hi, proj_hi, proj_hi, proj_hi, ret_norm_w.reshape(1, -1), cos_t, sin_t, dmat, xi, zeta)


def _ret_bwd(proj_hi, ret_norm_w, o_r, states, dy, tables, chunk_decay, name):
    T = proj_hi.shape[0]
    nc = T // BLOCK
    cos_t, sin_t, dmat, xi, zeta = tables

    def body(rq_ref, rk_ref, rv_ref, rg_ref, w_ref, cos_ref, sin_ref, d_ref, xi_ref, ze_ref,
             o_ref, st_ref, dy_ref, dq_ref, dk_ref, dv_ref, dg_ref, dw_ref, drs, dqs, dks):
        @pl.when(pl.program_id(0) == 0)
        def _():
            drs[...] = jnp.zeros_like(drs)
            dw_ref[...] = jnp.zeros_like(dw_ref)

        cos, sin = cos_ref[...], sin_ref[...]
        rq, rk = rq_ref[...], rk_ref[...]
        q = rq * cos + _swap_halves(rq) * sin
        k = (rk * cos + _swap_halves(rk) * sin) * RET_SCALE
        qb, kb = q.astype(BF16), k.astype(BF16)
        qx, kz = (q * xi_ref[...]).astype(BF16), (k * ze_ref[...]).astype(BF16)
        for h in range(RET_HEADS):
            sl = slice(h * RET_DK, (h + 1) * RET_DK)
            sv = slice(h * RET_DV, (h + 1) * RET_DV)
            _, vjp = jax.vjp(_gn_gate, o_ref[:, sv], rg_ref[:, sv], w_ref[:, sv])
            do, dg, dw = vjp(dy_ref[:, sv])
            dg_ref[:, sv] = dg.astype(BF16)
            dw_ref[:, sv] += dw
            dob = do.astype(BF16)
            vb = rv_ref[:, sv].astype(BF16)
            rb = st_ref[0, h].astype(BF16)
            dr = drs[h]
            drb = dr.astype(BF16)
            dmat_h = d_ref[h]
            a = (_dot(dob, vb, _NT) * dmat_h).astype(BF16)
            p = (_dot(qb[:, sl], kb[:, sl], _NT) * dmat_h).astype(BF16)
            dqs[:, sl] = _dot(a, kb[:, sl], _NN) + _dot(dob, rb, _NT) * xi_ref[:, sl]
            dks[:, sl] = _dot(a, qb[:, sl], _TN) + _dot(vb, drb, _NT) * ze_ref[:, sl]
            dv_ref[:, sv] = (_dot(p, dob, _TN) + _dot(kz[:, sl], drb, _NN)).astype(BF16)
            drs[h] = dr * chunk_decay[h] + _dot(qx[:, sl], dob, _TN)
        dq = dqs[...]
        dk = dks[...] * RET_SCALE
        dq_ref[...] = (dq * cos + _swap_halves(dq * sin)).astype(BF16)
        dk_ref[...] = (dk * cos + _swap_halves(dk * sin)).astype(BF16)

    qk = (BLOCK, RET_HEADS * RET_DK)
    vv = (BLOCK, RET_HEADS * RET_DV)
    rev = lambda n: nc - 1 - n
    const = lambda shape: pl.BlockSpec(shape, lambda n: (0,) * len(shape))
    return pl.pallas_call(
        body, name=name, grid=(nc,),
        in_specs=[pl.BlockSpec(qk, lambda n: (rev(n), 0)), pl.BlockSpec(qk, lambda n: (rev(n), 1)),
                  pl.BlockSpec(vv, lambda n: (rev(n), 1)), pl.BlockSpec(vv, lambda n: (rev(n), 2)),
                  const((1, RET_HEADS * RET_DV)),
                  pl.BlockSpec(qk, lambda n: (rev(n), 0)), pl.BlockSpec(qk, lambda n: (rev(n), 0)),
                  const((RET_HEADS, BLOCK, BLOCK)), const(qk), const(qk),
                  pl.BlockSpec(vv, lambda n: (rev(n), 0)),
                  pl.BlockSpec((1, RET_HEADS, RET_DK, RET_DV), lambda n: (rev(n), 0, 0, 0)),
                  pl.BlockSpec(vv, lambda n: (rev(n), 0))],
        out_specs=[pl.BlockSpec(qk, lambda n: (rev(n), 0)), pl.BlockSpec(qk, lambda n: (rev(n), 0)),
                   pl.BlockSpec(vv, lambda n: (rev(n), 0)), pl.BlockSpec(vv, lambda n: (rev(n), 0)),
                   const((1, RET_HEADS * RET_DV))],
        out_shape=[jax.ShapeDtypeStruct((T, RET_HEADS * RET_DK), BF16)] * 2
        + [jax.ShapeDtypeStruct((T, RET_HEADS * RET_DV), BF16)] * 2
        + [jax.ShapeDtypeStruct((1, RET_HEADS * RET_DV), F32)],
        scratch_shapes=[pltpu.VMEM((RET_HEADS, RET_DK, RET_DV), F32), pltpu.VMEM(qk, F32), pltpu.VMEM(qk, F32)],
        compiler_params=_params("arbitrary"),
    )(proj_hi, proj_hi, proj_hi, proj_hi, ret_norm_w.reshape(1, -1), cos_t, sin_t, dmat, xi, zeta,
      o_r, states, dy)


GATE_BLK = 512
GATE_FIRST_BLK = 3


def _gated(g0, g1, g2, b0, b1, b2, pa, pb, pc):
    return jax.nn.sigmoid(g0 + b0) * pa + jax.nn.sigmoid(g1 + b1) * pb + jax.nn.sigmoid(g2 + b2) * pc


def _gate_specs(tm):
    return [pl.BlockSpec((tm, GATE_BLK), functools.partial(lambda i, c: (i, c), c=GATE_FIRST_BLK + j))
            for j in range(6)]


def _gate_args(g, bg_ref):
    gi = [jnp.concatenate([g[2 * j][...], g[2 * j + 1][...]], axis=1) for j in range(3)]
    return gi + [bg_ref[j:j + 1, :] for j in range(3)]


def _merge_fwd(proj_hi, b_gate, pa, pb, pc, name):
    T = proj_hi.shape[0]
    tm = _tile(T, (256, 128))

    def body(g0, g1, g2, g3, g4, g5, bg_ref, pa_ref, pb_ref, pc_ref, o_ref):
        args = _gate_args((g0, g1, g2, g3, g4, g5), bg_ref)
        o_ref[...] = _gated(*args, pa_ref[...], pb_ref[...], pc_ref[...]).astype(BF16)

    row = pl.BlockSpec((tm, D_MODEL), lambda i: (i, 0))
    return pl.pallas_call(
        body, name=name, grid=(T // tm,),
        in_specs=_gate_specs(tm) + [pl.BlockSpec((3, D_MODEL), lambda i: (0, 0)), row, row, row],
        out_specs=row, out_shape=jax.ShapeDtypeStruct((T, D_MODEL), BF16),
        compiler_params=_params("parallel"),
    )(*([proj_hi] * 6), b_gate, pa, pb, pc)


def _merge_bwd(proj_hi, b_gate, pa, pb, pc, dm, name):
    T = proj_hi.shape[0]
    tm = _tile(T, (256, 128))

    def body(g0, g1, g2, g3, g4, g5, bg_ref, pa_ref, pb_ref, pc_ref, dm_ref,
             dgi_ref, dbg_ref, dpa_ref, dpb_ref, dpc_ref):
        args = _gate_args((g0, g1, g2, g3, g4, g5), bg_ref)
        _, vjp = jax.vjp(_gated, *args, pa_ref[...], pb_ref[...], pc_ref[...])
        d = vjp(dm_ref[...])
        for j in range(3):
            dgi_ref[:, j * D_MODEL:(j + 1) * D_MODEL] = d[j].astype(BF16)
        dpa_ref[...] = d[6].astype(BF16)
        dpb_ref[...] = d[7].astype(BF16)
        dpc_ref[...] = d[8].astype(BF16)

        @pl.when(pl.program_id(0) == 0)
        def _():
            dbg_ref[...] = jnp.zeros_like(dbg_ref)

        for j in range(3):
            dbg_ref[j:j + 1, :] += d[3 + j]

    row = pl.BlockSpec((tm, D_MODEL), lambda i: (i, 0))
    vec = pl.BlockSpec((3, D_MODEL), lambda i: (0, 0))
    return pl.pallas_call(
        body, name=name, grid=(T // tm,),
        in_specs=_gate_specs(tm) + [vec, row, row, row, row],
        out_specs=[pl.BlockSpec((tm, 3 * D_MODEL), lambda i: (i, 0)), vec, row, row, row],
        out_shape=[jax.ShapeDtypeStruct((T, 3 * D_MODEL), BF16), jax.ShapeDtypeStruct((3, D_MODEL), F32)]
        + [jax.ShapeDtypeStruct((T, D_MODEL), BF16)] * 3,
        compiler_params=_params("arbitrary"),
    )(*([proj_hi] * 6), b_gate, pa, pb, pc, dm)


FFN_TM = 256


def _gelu(u):
    return 0.5 * u * (1.0 + jnp.tanh(0.7978845608028654 * (u + 0.044715 * (u * u * u))))


def _conv_taps(u0, prev8, first):
    prev8 = jnp.where(first, 0.0, prev8)
    row = lax.broadcasted_iota(jnp.int32, u0.shape, 0)
    s1 = jnp.where(row == 0, prev8[7:8], pltpu.roll(u0, 1, 0))
    s2 = jnp.where(row == 0, prev8[6:7], jnp.where(row == 1, prev8[7:8], pltpu.roll(u0, 2, 0)))
    return s1, s2


def _ffn_specs(T, tm):
    row = pl.BlockSpec((tm, D_FF), lambda i: (i, 0))
    prev = pl.BlockSpec((8, D_FF), lambda i: (jnp.maximum(i * (tm // 8) - 1, 0), 0))
    nxt = pl.BlockSpec((8, D_FF), lambda i: (jnp.minimum((i + 1) * (tm // 8), T // 8 - 1), 0))
    return row, prev, nxt


def _ffn_mid_fwd(u0, gt, cw, cb, name):
    T = u0.shape[0]
    tm = _tile(T, (FFN_TM, 128))
    row, prev, _ = _ffn_specs(T, tm)

    def body(u_ref, p_ref, g_ref, cw_ref, cb_ref, f_ref):
        u0 = u_ref[...]
        s1, s2 = _conv_taps(u0, p_ref[...], pl.program_id(0) == 0)
        cw = cw_ref[...]
        u = cw[0:1] * s2 + cw[1:2] * s1 + cw[2:3] * u0 + cb_ref[...]
        f_ref[...] = (_gelu(u) * g_ref[...]).astype(BF16)

    return pl.pallas_call(
        body, name=name, grid=(T // tm,),
        in_specs=[row, prev, row, pl.BlockSpec((3, D_FF), lambda i: (0, 0)), pl.BlockSpec((1, D_FF), lambda i: (0, 0))],
        out_specs=row, out_shape=jax.ShapeDtypeStruct((T, D_FF), BF16),
        compiler_params=_params("parallel"),
    )(u0, u0, gt, cw, cb.reshape(1, D_FF))


def _ffn_mid_bwd_a(u0, gt, cw, cb, df, name):
    T = u0.shape[0]
    tm = _tile(T, (FFN_TM, 128))
    row, prev, _ = _ffn_specs(T, tm)

    def body(u_ref, p_ref, g_ref, cw_ref, cb_ref, df_ref, du_ref, dg_ref, dcw_ref, dcb_ref):
        u0 = u_ref[...]
        s1, s2 = _conv_taps(u0, p_ref[...], pl.program_id(0) == 0)
        cw = cw_ref[...]
        u = cw[0:1] * s2 + cw[1:2] * s1 + cw[2:3] * u0 + cb_ref[...]
        a, vjp = jax.vjp(_gelu, u)
        df = df_ref[...]
        dg_ref[...] = (df * a).astype(BF16)
        du = vjp(df * g_ref[...])[0]
        du_ref[...] = du

        @pl.when(pl.program_id(0) == 0)
        def _():
            dcw_ref[...] = jnp.zeros_like(dcw_ref)
            dcb_ref[...] = jnp.zeros_like(dcb_ref)

        dcw_ref[0:1, :] += jnp.sum(du * s2, axis=0, keepdims=True)
        dcw_ref[1:2, :] += jnp.sum(du * s1, axis=0, keepdims=True)
        dcw_ref[2:3, :] += jnp.sum(du * u0, axis=0, keepdims=True)
        dcb_ref[...] += jnp.sum(du, axis=0, keepdims=True)

    c3 = pl.BlockSpec((3, D_FF), lambda i: (0, 0))
    c1 = pl.BlockSpec((1, D_FF), lambda i: (0, 0))
    return pl.pallas_call(
        body, name=name, grid=(T // tm,),
        in_specs=[row, prev, row, c3, c1, row], out_specs=[row, row, c3, c1],
        out_shape=[jax.ShapeDtypeStruct((T, D_FF), F32), jax.ShapeDtypeStruct((T, D_FF), BF16),
                   jax.ShapeDtypeStruct((3, D_FF), F32), jax.ShapeDtypeStruct((1, D_FF), F32)],
        compiler_params=_params("arbitrary"),
    )(u0, u0, gt, cw, cb.reshape(1, D_FF), df)


def _ffn_mid_bwd_b(du, cw, name):
    T = du.shape[0]
    tm = _tile(T, (FFN_TM, 128))
    row, _, nxt = _ffn_specs(T, tm)

    def body(du_ref, n_ref, cw_ref, o_ref):
        du = du_ref[...]
        nx = jnp.where(pl.program_id(0) == T // tm - 1, 0.0, n_ref[...])
        r = lax.broadcasted_iota(jnp.int32, du.shape, 0)
        u1 = jnp.where(r == tm - 1, nx[0:1], pltpu.roll(du, tm - 1, 0))
        u2 = jnp.where(r == tm - 1, nx[1:2], jnp.where(r == tm - 2, nx[0:1], pltpu.roll(du, tm - 2, 0)))
        cw = cw_ref[...]
        o_ref[...] = (cw[2:3] * du + cw[1:2] * u1 + cw[0:1] * u2).astype(BF16)

    return pl.pallas_call(
        body, name=name, grid=(T // tm,),
        in_specs=[row, nxt, pl.BlockSpec((3, D_FF), lambda i: (0, 0))],
        out_specs=row, out_shape=jax.ShapeDtypeStruct((T, D_FF), BF16),
        compiler_params=_params("parallel"),
    )(du, du, cw)


def _mesh_peers():
    x, y, c = lax.axis_index("x"), lax.axis_index("y"), lax.axis_index("c")
    peers = []
    for k in range(1, N_DEV):
        px = 1 - x if k & 4 else x
        py = 1 - y if k & 2 else y
        pc = 1 - c if k & 1 else c
        peers.append(((px, py, pc), 4 * px + 2 * py + pc))
    return 4 * x + 2 * y + c, peers


def _exchange(src, name, scatter):
    shape = src.shape[-2:]

    def body(s_ref, o_ref, send_sems, recv_sems, local_sem):
        me, peers = _mesh_peers()
        mine = pltpu.make_async_copy(s_ref.at[me] if scatter else s_ref, o_ref.at[me], local_sem)
        mine.start()
        sends, recvs = [], []
        for k, (dev, idx) in enumerate(peers):
            sends.append(pltpu.make_async_remote_copy(
                src_ref=s_ref.at[idx] if scatter else s_ref, dst_ref=o_ref.at[me],
                send_sem=send_sems.at[k], recv_sem=recv_sems.at[k],
                device_id=dev, device_id_type=pl.DeviceIdType.MESH))
            recvs.append(pltpu.make_async_remote_copy(
                src_ref=s_ref.at[idx] if scatter else s_ref, dst_ref=o_ref.at[idx],
                send_sem=send_sems.at[k], recv_sem=recv_sems.at[k],
                device_id=dev, device_id_type=pl.DeviceIdType.MESH))
        for cp in sends:
            cp.start()
        for cp in recvs:
            cp.wait_recv()
        for cp in sends:
            cp.wait_send()
        mine.wait()

    return pl.pallas_call(
        body, name=name,
        in_specs=[pl.BlockSpec(memory_space=pl.ANY)], out_specs=pl.BlockSpec(memory_space=pl.ANY),
        out_shape=jax.ShapeDtypeStruct((N_DEV,) + shape, src.dtype),
        scratch_shapes=[pltpu.SemaphoreType.DMA((N_DEV - 1,)), pltpu.SemaphoreType.DMA((N_DEV - 1,)),
                        pltpu.SemaphoreType.DMA],
    )(src)


def _sum_devices(parts, name):
    _, R, C = parts.shape
    tr = _tile(R, (256, 128, 64, 32, 16, 8))

    def body(p_ref, o_ref):
        acc = p_ref[0]
        for j in range(1, N_DEV):
            acc = acc + p_ref[j]
        o_ref[...] = acc

    return pl.pallas_call(
        body, name=name, grid=(R // tr,),
        in_specs=[pl.BlockSpec((N_DEV, tr, C), lambda i: (0, i, 0))],
        out_specs=pl.BlockSpec((tr, C), lambda i: (i, 0)),
        out_shape=jax.ShapeDtypeStruct((R, C), F32),
        compiler_params=_params("parallel"),
    )(parts)


def _adamw(w, g, m, v, name):
    shape = w.shape
    C = shape[-1]
    R = int(np.prod(shape[:-1])) if len(shape) > 1 else 1
    tr = _tile(R, (256, 128, 64, 32, 16, 8))

    def body(w_ref, g_ref, m_ref, v_ref, d_ref, nm_ref, nv_ref):
        g = g_ref[...]
        m = ADAM_B1 * m_ref[...] + (1.0 - ADAM_B1) * g
        v = ADAM_B2 * v_ref[...] + (1.0 - ADAM_B2) * (g * g)
        m_hat = m / (1.0 - ADAM_B1 ** ADAM_STEP)
        v_hat = v / (1.0 - ADAM_B2 ** ADAM_STEP)
        d_ref[...] = -ADAM_LR * (m_hat / (jnp.sqrt(v_hat) + ADAM_EPS) + ADAM_WD * w_ref[...])
        nm_ref[...] = m
        nv_ref[...] = v

    blk = pl.BlockSpec((tr, C), lambda i: (i, 0))
    outs = pl.pallas_call(
        body, name=name, grid=(R // tr,), in_specs=[blk] * 4, out_specs=[blk] * 3,
        out_shape=[jax.ShapeDtypeStruct((R, C), F32)] * 3,
        compiler_params=_params("parallel"),
    )(*[t.reshape(R, C) for t in (w, g, m, v)])
    return [o.reshape(shape) for o in outs]


PACK_ROWS = (1056, 352, 352, 128, 352, 32, 64, 64)
PACK_LAYER = sum(PACK_ROWS)
SMALL_SIZES = (("rel_bias", 384), ("norm_mix_w", 4096), ("ret_norm_w", 2048), ("norm_ffn_w", 4096),
               ("conv_b", 11264), ("final_norm_w", 1024), ("b_gate", 12288), ("conv_w", 33792))
SMALL_ROWS = 72


def _layer_forward(x, wl, tabs, l):
    sv = {"x_in": x}
    h = _rms_fwd(x, wl["norm_mix_w"], f"rms_mix_fwd_{l}")
    lo = _matmul(h, wl["win_lo"], "nt", f"in_proj_lo_{l}", out_dtype=BF16)
    hi = _matmul(h, wl["win_hi"], "nt", f"in_proj_hi_{l}")
    outs, lses = [], []
    for g in range(3):
        o, s = _a_attn_fwd(lo, tabs["bias"], g, f"a_fwd_{g}_{l}")
        outs.append(o)
        lses.append(s)
    y_a = _a_merge_fwd(outs, lses, f"a_merge_fwd_{l}")
    y_b, ctot = _sb_fwd(lo, f"sb_fwd_{l}")
    y_c, o_r, states = _ret_fwd(hi, wl["ret_norm_w"], tabs["ret"], tabs["decay"], f"ret_fwd_{l}")
    pa = _matmul(y_a, wl["wpa"], "nt", f"proj_a_{l}")
    pb = _matmul(y_b, wl["wpb"], "nt", f"proj_b_{l}")
    pc = _matmul(y_c, wl["wpc"], "nt", f"proj_c_{l}")
    merged = _merge_fwd(hi, wl["b_gate"], pa, pb, pc, f"merge_fwd_{l}")
    x_mid = _matmul(merged, wl["wout"], "nn", f"out_proj_{l}", add=x)
    h2 = _rms_fwd(x_mid, wl["norm_ffn_w"], f"rms_ffn_fwd_{l}")
    u0 = _matmul(h2, wl["wup"], "nt", f"ffn_up_{l}")
    gt = _matmul(h2, wl["wgate"], "nt", f"ffn_gate_{l}")
    f = _ffn_mid_fwd(u0, gt, wl["conv_w"], wl["conv_b"], f"ffn_mid_fwd_{l}")
    x_out = _matmul(f, wl["wdown"], "nn", f"ffn_down_{l}", add=x_mid)
    sv.update(h=h, lo=lo, hi=hi, outs=outs, lses=lses, y_a=y_a, y_b=y_b, ctot=ctot, y_c=y_c, o_r=o_r,
              states=states, pa=pa, pb=pb, pc=pc, merged=merged, x_mid=x_mid, h2=h2, u0=u0, gt=gt, f=f)
    return x_out, sv


def _layer_backward(dx, sv, wl, tabs, l):
    df = _matmul(dx, wl["wdown"], "nt", f"ffn_down_dx_{l}")
    d_wdown = _matmul(sv["f"], dx, "tn", f"ffn_down_dw_{l}")
    du, dgt, d_cw, d_cb = _ffn_mid_bwd_a(sv["u0"], sv["gt"], wl["conv_w"], wl["conv_b"], df, f"ffn_mid_bwd_a_{l}")
    du0 = _ffn_mid_bwd_b(du, wl["conv_w"], f"ffn_mid_bwd_b_{l}")
    dh2 = _matmul(du0, wl["wup"], "nn", f"ffn_up_dx_{l}")
    dh2 = _matmul(dgt, wl["wgate"], "nn", f"ffn_gate_dx_{l}", add=dh2)
    d_wup = _matmul(du0, sv["h2"], "tn", f"ffn_up_dw_{l}")
    d_wgate = _matmul(dgt, sv["h2"], "tn", f"ffn_gate_dw_{l}")
    dx_mid, d_nffn = _rms_bwd(sv["x_mid"], wl["norm_ffn_w"], dh2, dx, f"rms_ffn_bwd_{l}")
    dm = _matmul(dx_mid, wl["wout"], "nt", f"out_proj_dx_{l}")
    d_wout = _matmul(sv["merged"], dx_mid, "tn", f"out_proj_dw_{l}")
    dgi, d_bg, dpa, dpb, dpc = _merge_bwd(sv["hi"], wl["b_gate"], sv["pa"], sv["pb"], sv["pc"], dm, f"merge_bwd_{l}")
    dy_a = _matmul(dpa, wl["wpa"], "nn", f"proj_a_dx_{l}")
    dy_b = _matmul(dpb, wl["wpb"], "nn", f"proj_b_dx_{l}")
    dy_c = _matmul(dpc, wl["wpc"], "nn", f"proj_c_dx_{l}")
    d_wpa = _matmul(dpa, sv["y_a"], "tn", f"proj_a_dw_{l}")
    d_wpb = _matmul(dpb, sv["y_b"], "tn", f"proj_b_dw_{l}")
    d_wpc = _matmul(dpc, sv["y_c"], "tn", f"proj_c_dw_{l}")
    d_rq, d_rk, d_rv, d_rg, d_rnw = _ret_bwd(sv["hi"], wl["ret_norm_w"], sv["o_r"], sv["states"], dy_c,
                                             tabs["ret"], tabs["decay"], f"ret_bwd_{l}")
    d_sq, d_sk, d_sv = _sb_bwd(sv["lo"], sv["ctot"], dy_b, f"sb_bwd_{l}")
    douts, dlses = _a_merge_bwd(sv["outs"], sv["lses"], dy_a, f"a_merge_bwd_{l}")
    dqs, dks, dvs, dbs = [], [], [], []
    for g in range(3):
        dq, dk, dv, db = _a_attn_bwd(sv["lo"], tabs["bias"], sv["outs"][g], sv["lses"][g], douts[g], dlses[g],
                                     g, f"a_bwd_{g}_{l}")
        dqs.append(dq)
        dks.append(dk)
        dvs.append(dv)
        dbs.append(db)
    heads = lambda t: [t[i].astype(BF16) for i in range(SB_HEADS)]
    dlo = jnp.concatenate(dqs + dks + dvs + [d_sq] + heads(d_sk) + heads(d_sv), axis=1)
    dhi = jnp.concatenate([d_rq, d_rk, d_rv, d_rg, dgi], axis=1)
    dh = _matmul(dlo, wl["win_lo"], "nn", f"in_proj_lo_dx_{l}")
    dh = _matmul(dhi, wl["win_hi"], "nn", f"in_proj_hi_dx_{l}", add=dh)
    d_win = jnp.concatenate([_matmul(dlo, sv["h"], "tn", f"in_proj_lo_dw_{l}"),
                             _matmul(dhi, sv["h"], "tn", f"in_proj_hi_dw_{l}")], axis=0)
    dx_in, d_nmix = _rms_bwd(sv["x_in"], wl["norm_mix_w"], dh, dx_mid, f"rms_mix_bwd_{l}")
    chunks = lambda t: t.reshape(N_DEV, -1, D_MODEL)
    big = jnp.concatenate([chunks(d_win), chunks(d_wup), chunks(d_wgate), chunks(d_wout), chunks(d_wdown),
                           chunks(d_wpa), chunks(d_wpb), chunks(d_wpc)], axis=1)
    small = dict(norm_mix_w=d_nmix[0], ret_norm_w=d_rnw[0], norm_ffn_w=d_nffn[0], conv_b=d_cb[0],
                 b_gate=d_bg, conv_w=d_cw, dbias=jnp.concatenate(dbs, axis=0))
    return dx_in, big, small


def kernel(x, rel_bias, norm_mix_w, w_in, b_gate, ret_norm_w, w_proj_a, w_proj_b, w_proj_c, w_out, norm_ffn_w, w_up, w_gate, conv_w, conv_b, w_down, final_norm_w, loss_target, m_rel_bias, m_norm_mix_w, m_w_in, m_b_gate, m_ret_norm_w, m_w_proj_a, m_w_proj_b, m_w_proj_c, m_w_out, m_norm_ffn_w, m_w_up, m_w_gate, m_conv_w, m_conv_b, m_w_down, m_final_norm_w, v_rel_bias, v_norm_mix_w, v_w_in, v_b_gate, v_ret_norm_w, v_w_proj_a, v_w_proj_b, v_w_proj_c, v_w_out, v_norm_ffn_w, v_w_up, v_w_gate, v_conv_w, v_conv_b, v_w_down, v_final_norm_w):
    T = x.shape[1]
    me = 4 * lax.axis_index("x") + 2 * lax.axis_index("y") + lax.axis_index("c")

    rows = []
    for l in range(DEPTH):
        rows += [w_in[l].T, w_up[l].T, w_gate[l].T, w_out[l], w_down[l],
                 w_proj_a[l].T.reshape(-1, D_MODEL), w_proj_b[l].T.reshape(-1, D_MODEL),
                 w_proj_c[l].T.reshape(-1, D_MODEL)]
    wall = _exchange(jnp.concatenate(rows, axis=0).astype(BF16), "gather_weights", scatter=False)
    n_bg, n_cw = b_gate.size, conv_w.size
    tiny = jnp.concatenate([b_gate.reshape(-1), conv_w.reshape(-1), jnp.zeros((8 * D_MODEL - n_bg - n_cw,), F32)])
    tall = _exchange(tiny.reshape(8, D_MODEL), "gather_small_weights", scatter=False).reshape(N_DEV, -1)
    spread = lambda t, w: t.reshape(N_DEV, DEPTH, 3, w).transpose(1, 2, 0, 3).reshape(DEPTH, 3, N_DEV * w)
    b_gate_full = spread(tall[:, :n_bg], b_gate.shape[-1])
    conv_w_full = spread(tall[:, n_bg:n_bg + n_cw], conv_w.shape[-1])

    def layer_weights(l):
        base = l * PACK_LAYER
        offs = np.cumsum((0,) + PACK_ROWS)
        seg = lambda j: wall[:, base + offs[j]:base + offs[j + 1], :]
        win = seg(0).reshape(-1, D_MODEL)
        unpack = lambda j, k: seg(j).reshape(N_DEV, D_MODEL // N_DEV, k).reshape(D_MODEL, k)
        return dict(win_lo=win[:LO_WIDTH], win_hi=win[LO_WIDTH:], wup=seg(1).reshape(-1, D_MODEL),
                    wgate=seg(2).reshape(-1, D_MODEL), wout=seg(3).reshape(-1, D_MODEL),
                    wdown=seg(4).reshape(-1, D_MODEL), wpa=unpack(5, 256), wpb=unpack(6, 512), wpc=unpack(7, 512),
                    norm_mix_w=norm_mix_w[l], norm_ffn_w=norm_ffn_w[l], ret_norm_w=ret_norm_w[l],
                    b_gate=b_gate_full[l], conv_w=conv_w_full[l], conv_b=conv_b[l])

    buckets = _a_buckets()
    ret_tabs, decay = _ret_tables(T)
    tabs = dict(bias=_a_bias_tables(rel_bias, buckets), ret=ret_tabs, decay=decay)

    xs = x[0]
    saved, wls = [], []
    for l in range(DEPTH):
        wls.append(layer_weights(l))
        xs, sv = _layer_forward(xs, wls[l], tabs, l)
        saved.append(sv)
    loss_tile, dx, d_final = _loss_head(xs, final_norm_w, loss_target[0], "loss_head")
    loss = lax.psum(loss_tile[0, 0], ("x", "y", "c"))

    bigs, smalls = [None] * DEPTH, [None] * DEPTH
    for l in reversed(range(DEPTH)):
        dx, bigs[l], smalls[l] = _layer_backward(dx, saved[l], wls[l], tabs, l)
    grad_x = dx[None]

    parts = _exchange(jnp.concatenate(bigs, axis=1), "scatter_grads", scatter=True)
    mine = _sum_devices(parts, "sum_grads")
    dbias = smalls[0]["dbias"] + smalls[1]["dbias"] + smalls[2]["dbias"] + smalls[3]["dbias"]
    small_vals = dict(rel_bias=_a_bias_grad(dbias, buckets), final_norm_w=d_final[0])
    for name in ("norm_mix_w", "ret_norm_w", "norm_ffn_w", "conv_b", "b_gate", "conv_w"):
        small_vals[name] = jnp.stack([smalls[l][name] for l in range(DEPTH)])
    flat = jnp.concatenate([small_vals[n].reshape(-1) for n, _ in SMALL_SIZES])
    flat = jnp.concatenate([flat, jnp.zeros((SMALL_ROWS * D_MODEL - flat.shape[0],), F32)])
    sparts = _exchange(flat.reshape(SMALL_ROWS, D_MODEL), "gather_small_grads", scatter=False)
    ssum = _sum_devices(sparts, "sum_small_grads").reshape(-1)

    grads = {}
    off = 0
    for name, size in SMALL_SIZES:
        grads[name] = ssum[off:off + size]
        off += size
    grads["rel_bias"] = grads["rel_bias"].reshape(REL_BUCKETS, 12)
    for name in ("norm_mix_w", "norm_ffn_w"):
        grads[name] = grads[name].reshape(DEPTH, D_MODEL)
    grads["ret_norm_w"] = grads["ret_norm_w"].reshape(DEPTH, -1)
    grads["conv_b"] = grads["conv_b"].reshape(DEPTH, D_FF)
    bw, cwid = b_gate.shape[-1], conv_w.shape[-1]
    grads["b_gate"] = lax.dynamic_slice_in_dim(grads["b_gate"].reshape(DEPTH, 3, -1), me * bw, bw, axis=2)
    grads["conv_w"] = lax.dynamic_slice_in_dim(grads["conv_w"].reshape(DEPTH, 3, -1), me * cwid, cwid, axis=2)

    offs = np.cumsum((0,) + PACK_ROWS)
    per_layer = mine.reshape(DEPTH, PACK_LAYER, D_MODEL)
    seg = lambda j: per_layer[:, offs[j]:offs[j + 1], :]
    back = lambda j, k: seg(j).reshape(DEPTH, D_MODEL // N_DEV, k).transpose(0, 2, 1)
    grads["w_in"] = seg(0).transpose(0, 2, 1)
    grads["w_up"] = seg(1).transpose(0, 2, 1)
    grads["w_gate"] = seg(2).transpose(0, 2, 1)
    grads["w_out"] = seg(3)
    grads["w_down"] = seg(4)
    grads["w_proj_a"] = back(5, 256)
    grads["w_proj_b"] = back(6, 512)
    grads["w_proj_c"] = back(7, 512)

    order = ["rel_bias", "norm_mix_w", "w_in", "b_gate", "ret_norm_w", "w_proj_a", "w_proj_b", "w_proj_c",
             "w_out", "norm_ffn_w", "w_up", "w_gate", "conv_w", "conv_b", "w_down", "final_norm_w"]
    ws = dict(rel_bias=rel_bias, norm_mix_w=norm_mix_w, w_in=w_in, b_gate=b_gate, ret_norm_w=ret_norm_w,
              w_proj_a=w_proj_a, w_proj_b=w_proj_b, w_proj_c=w_proj_c, w_out=w_out, norm_ffn_w=norm_ffn_w,
              w_up=w_up, w_gate=w_gate, conv_w=conv_w, conv_b=conv_b, w_down=w_down, final_norm_w=final_norm_w)
    ms = dict(rel_bias=m_rel_bias, norm_mix_w=m_norm_mix_w, w_in=m_w_in, b_gate=m_b_gate, ret_norm_w=m_ret_norm_w,
              w_proj_a=m_w_proj_a, w_proj_b=m_w_proj_b, w_proj_c=m_w_proj_c, w_out=m_w_out,
              norm_ffn_w=m_norm_ffn_w, w_up=m_w_up, w_gate=m_w_gate, conv_w=m_conv_w, conv_b=m_conv_b,
              w_down=m_w_down, final_norm_w=m_final_norm_w)
    vs = dict(rel_bias=v_rel_bias, norm_mix_w=v_norm_mix_w, w_in=v_w_in, b_gate=v_b_gate, ret_norm_w=v_ret_norm_w,
              w_proj_a=v_w_proj_a, w_proj_b=v_w_proj_b, w_proj_c=v_w_proj_c, w_out=v_w_out,
              norm_ffn_w=v_norm_ffn_w, w_up=v_w_up, w_gate=v_w_gate, conv_w=v_conv_w, conv_b=v_conv_b,
              w_down=v_w_down, final_norm_w=v_final_norm_w)
    deltas, new_m, new_v = [], [], []
    for name in order:
        g = grads[name].reshape(ws[name].shape)
        grads[name] = g
        d, nm, nv = _adamw(ws[name], g, ms[name], vs[name], f"adamw_{name}")
        deltas.append(d)
        new_m.append(nm)
        new_v.append(nv)
    return (loss, grad_x, *[grads[n] for n in order], *deltas, *new_m, *new_v)
```

```python
import functools

import numpy as np
import jax
import jax.numpy as jnp
from jax import lax
from jax.experimental import pallas as pl
from jax.experimental.pallas import tpu as pltpu

F32 = jnp.float32
BF16 = jnp.bfloat16

D_MODEL = 1024
DEPTH = 4
N_DEV = 8
BLOCK = 128
DIL_GROUPS = ((128, 1), (512, 4), (2048, 16))
A_HEADS_PER_GROUP = 4
HEAD_DIM = 64
A_GROUP_WIDTH = A_HEADS_PER_GROUP * HEAD_DIM
SB_HEADS = 4
SB_HEAD_DIM = 128
RET_HEADS = 4
RET_DK = 64
RET_DV = 128
ROPE_BASE = 10000.0
REL_BUCKETS = 32
REL_MAX_DIST = 2048
D_FF = 2816
EPS = 1e-6
LO_WIDTH = 3840
HI_WIDTH = 4608
A_SCALE = HEAD_DIM ** -0.5
SB_SCALE = SB_HEAD_DIM ** -0.5
RET_SCALE = RET_DK ** -0.5
NEG = -1e30

ADAM_LR = 0.001
ADAM_B1 = 0.9
ADAM_B2 = 0.999
ADAM_EPS = 1e-08
ADAM_WD = 0.01
ADAM_STEP = 10

VMEM_LIMIT_V7X = 56 * 1024 * 1024

_NT = (((1,), (1,)), ((), ()))
_NN = (((1,), (0,)), ((), ()))
_TN = (((0,), (0,)), ((), ()))


def _dot(a, b, dims):
    return lax.dot_general(a, b, dims, preferred_element_type=F32)


def _tile(n, cands):
    for c in cands:
        if n % c == 0:
            return c
    return n


def _params(*sem):
    return pltpu.CompilerParams(dimension_semantics=sem, vmem_limit_bytes=VMEM_LIMIT_V7X)


def _matmul(a, b, mode, name, out_dtype=F32, add=None):
    if mode == "tn":
        (K, M), N = a.shape, b.shape[1]
    elif mode == "nn":
        (M, K), N = a.shape, b.shape[1]
    else:
        (M, K), N = a.shape, b.shape[0]
    tm = _tile(M, (512, 384, 256, 128))
    tn = _tile(N, (1024, 768, 512, 384, 256, 128))
    tk = _tile(K, (1024, 768, 512, 384, 256, 128))
    nk = K // tk
    if mode == "tn":
        a_spec = pl.BlockSpec((tk, tm), lambda i, j, k: (k, i))
    else:
        a_spec = pl.BlockSpec((tm, tk), lambda i, j, k: (i, k))
    if mode == "nt":
        b_spec = pl.BlockSpec((tn, tk), lambda i, j, k: (j, k))
    else:
        b_spec = pl.BlockSpec((tk, tn), lambda i, j, k: (k, j))
    dims = {"nt": _NT, "nn": _NN, "tn": _TN}[mode]
    o_spec = pl.BlockSpec((tm, tn), lambda i, j, k: (i, j))
    has_add = add is not None

    def body(*refs):
        a_ref, b_ref = refs[0], refs[1]
        o_ref, acc = refs[-2], refs[-1]
        k = pl.program_id(2)

        @pl.when(k == 0)
        def _():
            acc[...] = jnp.zeros_like(acc)

        acc[...] += _dot(a_ref[...].astype(BF16), b_ref[...].astype(BF16), dims)

        @pl.when(k == nk - 1)
        def _():
            r = acc[...]
            if has_add:
                r = r + refs[2][...]
            o_ref[...] = r.astype(out_dtype)

    ins = [a, b] + ([add] if has_add else [])
    specs = [a_spec, b_spec] + ([o_spec] if has_add else [])
    return pl.pallas_call(
        body, name=name, grid=(M // tm, N // tn, nk),
        in_specs=specs, out_specs=o_spec,
        out_shape=jax.ShapeDtypeStruct((M, N), out_dtype),
        scratch_shapes=[pltpu.VMEM((tm, tn), F32)],
        compiler_params=_params("parallel", "parallel", "arbitrary"),
    )(*ins)


def _rms(x, w):
    return x * lax.rsqrt(jnp.mean(x * x, axis=-1, keepdims=True) + EPS) * w


def _rms_fwd(x, w, name):
    T = x.shape[0]
    tm = _tile(T, (512, 256, 128))

    def body(x_ref, w_ref, o_ref):
        o_ref[...] = _rms(x_ref[...], w_ref[...]).astype(BF16)

    return pl.pallas_call(
        body, name=name, grid=(T // tm,),
        in_specs=[pl.BlockSpec((tm, D_MODEL), lambda i: (i, 0)), pl.BlockSpec((1, D_MODEL), lambda i: (0, 0))],
        out_specs=pl.BlockSpec((tm, D_MODEL), lambda i: (i, 0)),
        out_shape=jax.ShapeDtypeStruct((T, D_MODEL), BF16),
        compiler_params=_params("parallel"),
    )(x, w.reshape(1, D_MODEL))


def _rms_bwd(x, w, dh, res, name):
    T = x.shape[0]
    tm = _tile(T, (512, 256, 128))

    def body(x_ref, w_ref, dh_ref, res_ref, dx_ref, dw_ref):
        _, vjp = jax.vjp(_rms, x_ref[...], w_ref[...])
        dx, dw = vjp(dh_ref[...])
        dx_ref[...] = dx + res_ref[...]

        @pl.when(pl.program_id(0) == 0)
        def _():
            dw_ref[...] = jnp.zeros_like(dw_ref)

        dw_ref[...] += dw

    row = pl.BlockSpec((tm, D_MODEL), lambda i: (i, 0))
    vec = pl.BlockSpec((1, D_MODEL), lambda i: (0, 0))
    return pl.pallas_call(
        body, name=name, grid=(T // tm,),
        in_specs=[row, vec, row, row], out_specs=[row, vec],
        out_shape=[jax.ShapeDtypeStruct((T, D_MODEL), F32), jax.ShapeDtypeStruct((1, D_MODEL), F32)],
        compiler_params=_params("arbitrary"),
    )(x, w.reshape(1, D_MODEL), dh, res)


def _loss_head(x, w, tgt, name):
    T = x.shape[0]
    tm = _tile(T, (512, 256, 128))

    def body(x_ref, w_ref, t_ref, l_ref, dx_ref, dw_ref):
        y, vjp = jax.vjp(_rms, x_ref[...], w_ref[...])
        e = y - t_ref[...]
        dx, dw = vjp(e * (1.0 / D_MODEL))
        dx_ref[...] = dx

        @pl.when(pl.program_id(0) == 0)
        def _():
            dw_ref[...] = jnp.zeros_like(dw_ref)
            l_ref[...] = jnp.zeros_like(l_ref)

        dw_ref[...] += dw
        l_ref[...] += 0.5 * jnp.sum(jnp.mean(e * e, axis=-1, keepdims=True))

    row = pl.BlockSpec((tm, D_MODEL), lambda i: (i, 0))
    vec = pl.BlockSpec((1, D_MODEL), lambda i: (0, 0))
    return pl.pallas_call(
        body, name=name, grid=(T // tm,),
        in_specs=[row, vec, row],
        out_specs=[pl.BlockSpec((8, 128), lambda i: (0, 0)), row, vec],
        out_shape=[jax.ShapeDtypeStruct((8, 128), F32), jax.ShapeDtypeStruct((T, D_MODEL), F32),
                   jax.ShapeDtypeStruct((1, D_MODEL), F32)],
        compiler_params=_params("arbitrary"),
    )(x, w.reshape(1, D_MODEL), tgt)


def _t5_bucket(dist):
    max_exact = REL_BUCKETS // 2
    n = np.maximum(dist, 0)
    large = max_exact + (np.log(np.maximum(n, 1) / max_exact) / np.log(REL_MAX_DIST / max_exact)
                         * (REL_BUCKETS - max_exact)).astype(np.int64)
    large = np.minimum(large, REL_BUCKETS - 1)
    return np.where(n < max_exact, n, large).astype(np.int32)


def _a_buckets():
    steps = np.arange(BLOCK)[:, None] + BLOCK - np.arange(2 * BLOCK)[None, :]
    out = []
    for window, dilation in DIL_GROUPS:
        in_band = (steps >= 0) & (steps <= window // dilation)
        out.append(np.where(in_band, _t5_bucket(steps * dilation), -1))
    return jnp.asarray(np.stack(out).astype(np.int32))


def _a_bias_tables(rel_bias, buckets):
    def body(rb_ref, bk_ref, o_ref):
        hh = pl.program_id(0)
        bk = bk_ref[0]
        acc = jnp.full((BLOCK, 2 * BLOCK), NEG, F32)
        for b in range(REL_BUCKETS):
            acc = jnp.where(bk == b, rb_ref[b, hh], acc)
        o_ref[0] = acc

    return pl.pallas_call(
        body, name="a_bias_tables", grid=(12,),
        in_specs=[pl.BlockSpec(memory_space=pltpu.SMEM),
                  pl.BlockSpec((1, BLOCK, 2 * BLOCK), lambda h: (h // 4, 0, 0))],
        out_specs=pl.BlockSpec((1, BLOCK, 2 * BLOCK), lambda h: (h, 0, 0)),
        out_shape=jax.ShapeDtypeStruct((12, BLOCK, 2 * BLOCK), F32),
        compiler_params=_params("parallel"),
    )(rel_bias, buckets)


def _a_bias_grad(dbias, buckets):
    def body(db_ref, bk_ref, o_ref):
        bk = bk_ref[0]
        db = db_ref[0]
        lane = lax.broadcasted_iota(jnp.int32, (8, 128), 1)
        acc = jnp.zeros((8, 128), F32)
        for b in range(REL_BUCKETS):
            acc = jnp.where(lane == b, jnp.sum(jnp.where(bk == b, db, 0.0)), acc)
        o_ref[0] = acc

    out = pl.pallas_call(
        body, name="a_bias_grad", grid=(12,),
        in_specs=[pl.BlockSpec((1, BLOCK, 2 * BLOCK), lambda h: (h, 0, 0)),
                  pl.BlockSpec((1, BLOCK, 2 * BLOCK), lambda h: (h // 4, 0, 0))],
        out_specs=pl.BlockSpec((1, 8, 128), lambda h: (h, 0, 0)),
        out_shape=jax.ShapeDtypeStruct((12, 8, 128), F32),
        compiler_params=_params("parallel"),
    )(dbias, buckets)
    return out[:, 0, :REL_BUCKETS].T


def _a_logits(q_ref, kp_ref, kc_ref, b_ref, h, ok):
    sl = slice(h * HEAD_DIM, (h + 1) * HEAD_DIM)
    qh = q_ref[:, sl]
    s = jnp.concatenate([_dot(qh, kp_ref[:, sl], _NT), _dot(qh, kc_ref[:, sl], _NT)], axis=1)
    s = s * A_SCALE + b_ref[h]
    return jnp.where(ok, s, NEG)


def _a_attn_fwd(proj_lo, bias_all, g, name):
    T = proj_lo.shape[0]
    d = DIL_GROUPS[g][1]
    L = T // d
    nb = L // BLOCK
    nlo = LO_WIDTH // A_GROUP_WIDTH
    pv = proj_lo.reshape(L, d * LO_WIDTH)

    def body(q_ref, kc_ref, kp_ref, vc_ref, vp_ref, b_ref, o_ref, l_ref):
        n = pl.program_id(1)
        col = lax.broadcasted_iota(jnp.int32, (BLOCK, 2 * BLOCK), 1)
        ok = col >= jnp.where(n > 0, 0, BLOCK)
        for h in range(A_HEADS_PER_GROUP):
            sl = slice(h * HEAD_DIM, (h + 1) * HEAD_DIM)
            s = _a_logits(q_ref, kp_ref, kc_ref, b_ref, h, ok)
            m = jnp.max(s, axis=-1, keepdims=True)
            p = jnp.exp(s - m)
            l = jnp.sum(p, axis=-1, keepdims=True)
            pb = p.astype(BF16)
            o = _dot(pb[:, :BLOCK], vp_ref[:, sl], _NN) + _dot(pb[:, BLOCK:], vc_ref[:, sl], _NN)
            o_ref[:, sl] = o / l
            l_ref[:, sl] = jnp.broadcast_to(m + jnp.log(l), (BLOCK, HEAD_DIM))

    blk = (BLOCK, A_GROUP_WIDTH)
    out_spec = pl.BlockSpec(blk, lambda r, n: (n, r))
    out, lse = pl.pallas_call(
        body, name=name, grid=(d, nb),
        in_specs=[pl.BlockSpec(blk, lambda r, n: (n, r * nlo + g)),
                  pl.BlockSpec(blk, lambda r, n: (n, r * nlo + 3 + g)),
                  pl.BlockSpec(blk, lambda r, n: (jnp.maximum(n - 1, 0), r * nlo + 3 + g)),
                  pl.BlockSpec(blk, lambda r, n: (n, r * nlo + 6 + g)),
                  pl.BlockSpec(blk, lambda r, n: (jnp.maximum(n - 1, 0), r * nlo + 6 + g)),
                  pl.BlockSpec((4, BLOCK, 2 * BLOCK), lambda r, n: (g, 0, 0))],
        out_specs=[out_spec, out_spec],
        out_shape=[jax.ShapeDtypeStruct((L, d * A_GROUP_WIDTH), F32)] * 2,
        compiler_params=_params("parallel", "arbitrary"),
    )(pv, pv, pv, pv, pv, bias_all)
    return out.reshape(T, A_GROUP_WIDTH), lse.reshape(T, A_GROUP_WIDTH)


def _a_attn_bwd(proj_lo, bias_all, out, lse, dout, dlse, g, name):
    T = proj_lo.shape[0]
    d = DIL_GROUPS[g][1]
    L = T // d
    nb = L // BLOCK
    nlo = LO_WIDTH // A_GROUP_WIDTH
    pv = proj_lo.reshape(L, d * LO_WIDTH)
    view = lambda t: t.reshape(L, d * A_GROUP_WIDTH)

    def body(q_ref, kc_ref, kp_ref, vc_ref, vp_ref, b_ref, o_ref, l_ref, do_ref, dl_ref,
             dq_ref, dk_ref, dv_ref, db_ref, ck, cv):
        r = pl.program_id(0)
        n = pl.program_id(1)

        @pl.when((r == 0) & (n == 0))
        def _():
            db_ref[...] = jnp.zeros_like(db_ref)

        @pl.when(n == 0)
        def _():
            ck[...] = jnp.zeros_like(ck)
            cv[...] = jnp.zeros_like(cv)

        @pl.when(n < nb)
        def _():
            col = lax.broadcasted_iota(jnp.int32, (BLOCK, 2 * BLOCK), 1)
            ok = col >= jnp.where(n > 0, 0, BLOCK)
            for h in range(A_HEADS_PER_GROUP):
                sl = slice(h * HEAD_DIM, (h + 1) * HEAD_DIM)
                s = _a_logits(q_ref, kp_ref, kc_ref, b_ref, h, ok)
                p = jnp.exp(s - l_ref[:, h * HEAD_DIM:h * HEAD_DIM + 1])
                do = do_ref[:, sl]
                dob = do.astype(BF16)
                delta = jnp.sum(do * o_ref[:, sl], axis=-1, keepdims=True)
                dl = jnp.sum(dl_ref[:, sl], axis=-1, keepdims=True)
                dp = jnp.concatenate([_dot(dob, vp_ref[:, sl], _NT), _dot(dob, vc_ref[:, sl], _NT)], axis=1)
                ds = p * (dp - delta + dl)
                db_ref[h] += ds
                dsb = (ds * A_SCALE).astype(BF16)
                pb = p.astype(BF16)
                qh = q_ref[:, sl]
                dq_ref[:, sl] = (_dot(dsb[:, :BLOCK], kp_ref[:, sl], _NN)
                                 + _dot(dsb[:, BLOCK:], kc_ref[:, sl], _NN)).astype(BF16)
                dk_ref[:, sl] = (ck[:, sl] + _dot(dsb[:, :BLOCK], qh, _TN)).astype(BF16)
                dv_ref[:, sl] = (cv[:, sl] + _dot(pb[:, :BLOCK], dob, _TN)).astype(BF16)
                ck[:, sl] = _dot(dsb[:, BLOCK:], qh, _TN)
                cv[:, sl] = _dot(pb[:, BLOCK:], dob, _TN)

        @pl.when(n == nb)
        def _():
            dk_ref[...] = ck[...].astype(BF16)
            dv_ref[...] = cv[...].astype(BF16)

    blk = (BLOCK, A_GROUP_WIDTH)
    nq = lambda n: jnp.minimum(n, nb - 1)
    prv = lambda n: jnp.maximum(jnp.minimum(n, nb - 1) - 1, 0)
    cur_o = pl.BlockSpec(blk, lambda r, n: (nq(n), r))
    lag_o = pl.BlockSpec(blk, lambda r, n: (jnp.maximum(n - 1, 0), r))
    dq, dk, dv, db = pl.pallas_call(
        body, name=name, grid=(d, nb + 1),
        in_specs=[pl.BlockSpec(blk, lambda r, n: (nq(n), r * nlo + g)),
                  pl.BlockSpec(blk, lambda r, n: (nq(n), r * nlo + 3 + g)),
                  pl.BlockSpec(blk, lambda r, n: (prv(n), r * nlo + 3 + g)),
                  pl.BlockSpec(blk, lambda r, n: (nq(n), r * nlo + 6 + g)),
                  pl.BlockSpec(blk, lambda r, n: (prv(n), r * nlo + 6 + g)),
                  pl.BlockSpec((4, BLOCK, 2 * BLOCK), lambda r, n: (g, 0, 0)),
                  cur_o, cur_o, cur_o, cur_o],
        out_specs=[cur_o, lag_o, lag_o, pl.BlockSpec((4, BLOCK, 2 * BLOCK), lambda r, n: (0, 0, 0))],
        out_shape=[jax.ShapeDtypeStruct((L, d * A_GROUP_WIDTH), BF16)] * 3
        + [jax.ShapeDtypeStruct((4, BLOCK, 2 * BLOCK), F32)],
        scratch_shapes=[pltpu.VMEM(blk, F32), pltpu.VMEM(blk, F32)],
        compiler_params=_params("arbitrary", "arbitrary"),
    )(pv, pv, pv, pv, pv, bias_all, view(out), view(lse), view(dout), view(dlse))
    return dq.reshape(T, -1), dk.reshape(T, -1), dv.reshape(T, -1), db


def _a_merge(outs, lses):
    m = jnp.maximum(jnp.maximum(lses[0], lses[1]), lses[2])
    e = [jnp.exp(l - m) for l in lses]
    inv = 1.0 / (e[0] + e[1] + e[2])
    return (e[0] * outs[0] + e[1] * outs[1] + e[2] * outs[2]) * inv


def _a_merge_fwd(outs, lses, name):
    T = outs[0].shape[0]
    tm = _tile(T, (512, 256, 128))

    def body(o0, o1, o2, l0, l1, l2, y_ref):
        y_ref[...] = _a_merge([o0[...], o1[...], o2[...]], [l0[...], l1[...], l2[...]]).astype(BF16)

    row = pl.BlockSpec((tm, A_GROUP_WIDTH), lambda i: (i, 0))
    return pl.pallas_call(
        body, name=name, grid=(T // tm,), in_specs=[row] * 6, out_specs=row,
        out_shape=jax.ShapeDtypeStruct((T, A_GROUP_WIDTH), BF16),
        compiler_params=_params("parallel"),
    )(*outs, *lses)


def _a_merge_bwd(outs, lses, dy, name):
    T = outs[0].shape[0]
    tm = _tile(T, (512, 256, 128))

    def body(o0, o1, o2, l0, l1, l2, dy_ref, *outs_ref):
        _, vjp = jax.vjp(_a_merge, [o0[...], o1[...], o2[...]], [l0[...], l1[...], l2[...]])
        do, dl = vjp(dy_ref[...])
        for ref, val in zip(outs_ref, list(do) + list(dl)):
            ref[...] = val

    row = pl.BlockSpec((tm, A_GROUP_WIDTH), lambda i: (i, 0))
    res = pl.pallas_call(
        body, name=name, grid=(T // tm,), in_specs=[row] * 7, out_specs=[row] * 6,
        out_shape=[jax.ShapeDtypeStruct((T, A_GROUP_WIDTH), F32)] * 6,
        compiler_params=_params("parallel"),
    )(*outs, *lses, dy)
    return res[:3], res[3:]


SB_TQ = 512
SB_KS = 256
SB_UNROLL = 2
SB_Q_COL, SB_K_COL, SB_V_COL = 18, 22, 26


def _sb_tri():
    return jnp.asarray(np.arange(SB_KS)[:, None] > np.arange(SB_KS)[None, :], BF16)


def _neg_log_keep(z):
    return jnp.maximum(z, 0.0) + jnp.log(1.0 + jnp.exp(-jnp.abs(z)))


def _sb_sizes(T):
    assert T % SB_TQ == 0 and SB_TQ % (SB_KS * SB_UNROLL) == 0
    return SB_TQ, SB_KS, SB_TQ // SB_KS, SB_UNROLL


def _wide(c, width):
    return jnp.concatenate([c] * (width // BLOCK), axis=1)


def _sb_fwd(proj_lo, name):
    T = proj_lo.shape[0]
    TQ, KS, nsub, unroll = _sb_sizes(T)
    nq = T // TQ

    def body(q_ref, k_ref, v_ref, tri_ref, o_ref, c_ref, acc, cs):
        i = pl.program_id(1)
        q = q_ref[...]
        tri = tri_ref[...]
        row = i * TQ + lax.broadcasted_iota(jnp.int32, (TQ, KS), 0)
        col = lax.broadcasted_iota(jnp.int32, (TQ, KS), 1)

        def sub(n, masked, c, o):
            off = pl.multiple_of(n * KS, KS)
            z = _dot(q, k_ref[pl.ds(off, KS), :], _NT) * SB_SCALE
            nl = _neg_log_keep(z)
            if masked:
                valid = row > n * KS + col
                nl = jnp.where(valid, nl, 0.0)
            w = jnp.exp(z - nl - _dot(nl.astype(BF16), tri, _NN) - _wide(c, KS))
            if masked:
                w = jnp.where(valid, w, 0.0)
            o = o + _dot(w.astype(BF16), v_ref[pl.ds(off, KS), :], _NN)
            return c + jnp.sum(nl, axis=1, keepdims=True), o

        c = jnp.zeros((TQ, BLOCK), F32)
        o = jnp.zeros((TQ, BLOCK), F32)
        for j in reversed(range(nsub)):
            c, o = sub(i * nsub + j, True, c, o)
        cs[...] = c
        acc[...] = o

        def step(m, carry):
            c = cs[...]
            o = jnp.zeros((TQ, BLOCK), F32)
            for u in range(unroll):
                c, o = sub(i * nsub - 1 - (m * unroll + u), False, c, o)
            cs[...] = c
            acc[...] += o
            return carry

        lax.fori_loop(0, i * nsub // unroll, step, 0)
        o_ref[...] = acc[...]
        c_ref[...] = cs[...]

    qo = pl.BlockSpec((TQ, BLOCK), lambda h, i: (i, h))
    return pl.pallas_call(
        body, name=name, grid=(SB_HEADS, nq),
        in_specs=[pl.BlockSpec((TQ, BLOCK), lambda h, i: (i, SB_Q_COL + h)),
                  pl.BlockSpec((T, BLOCK), lambda h, i: (0, SB_K_COL + h)),
                  pl.BlockSpec((T, BLOCK), lambda h, i: (0, SB_V_COL + h)),
                  pl.BlockSpec((SB_KS, SB_KS), lambda h, i: (0, 0))],
        out_specs=[qo, qo],
        out_shape=[jax.ShapeDtypeStruct((T, SB_HEADS * BLOCK), F32)] * 2,
        scratch_shapes=[pltpu.VMEM((TQ, BLOCK), F32), pltpu.VMEM((TQ, BLOCK), F32)],
        compiler_params=_params("parallel", "arbitrary"),
    )(proj_lo, proj_lo, proj_lo, _sb_tri())


def _sb_bwd(proj_lo, ctot, dy, name):
    T = proj_lo.shape[0]
    TQ, KS, nsub, unroll = _sb_sizes(T)
    nq = T // TQ

    def body(q_ref, k_ref, v_ref, c_ref, do_ref, tri_ref, dq_ref, dk_hbm, dv_hbm,
             dka, dva, dqa, cps, pgs, sem):
        h = pl.program_id(0)
        i = pl.program_id(1)

        @pl.when(i == 0)
        def _():
            dka[...] = jnp.zeros_like(dka)
            dva[...] = jnp.zeros_like(dva)

        q = q_ref[...]
        tri = tri_ref[...]
        dob = do_ref[...].astype(BF16)
        ctot = c_ref[...]
        row = i * TQ + lax.broadcasted_iota(jnp.int32, (TQ, KS), 0)
        col = lax.broadcasted_iota(jnp.int32, (TQ, KS), 1)

        def sub(n, masked, cp, pg, dq):
            off = pl.multiple_of(n * KS, KS)
            kb = k_ref[pl.ds(off, KS), :]
            z = _dot(q, kb, _NT) * SB_SCALE
            nl = _neg_log_keep(z)
            if masked:
                valid = row > n * KS + col
                nl = jnp.where(valid, nl, 0.0)
            cp = cp + jnp.sum(nl, axis=1, keepdims=True)
            lb = z - nl
            w = jnp.exp(lb - _dot(nl.astype(BF16), tri, _NN) - _wide(ctot - cp, KS))
            if masked:
                w = jnp.where(valid, w, 0.0)
            g = w * _dot(dob, v_ref[pl.ds(off, KS), :], _NT)
            dz = g - jnp.exp(lb) * (g + _dot(g.astype(BF16), tri, _NT) + _wide(pg, KS))
            if masked:
                dz = jnp.where(valid, dz, 0.0)
            pg = pg + jnp.sum(g, axis=1, keepdims=True)
            dzb = (dz * SB_SCALE).astype(BF16)
            dka[pl.ds(off, KS), :] += _dot(dzb, q, _TN)
            dva[pl.ds(off, KS), :] += _dot(w.astype(BF16), dob, _TN)
            return cp, pg, dq + _dot(dzb, kb, _NN)

        zero = jnp.zeros((TQ, BLOCK), F32)
        cps[...] = zero
        pgs[...] = zero
        dqa[...] = zero

        def step(m, carry):
            cp, pg, dq = cps[...], pgs[...], zero
            for u in range(unroll):
                cp, pg, dq = sub(m * unroll + u, False, cp, pg, dq)
            cps[...] = cp
            pgs[...] = pg
            dqa[...] += dq
            return carry

        lax.fori_loop(0, i * nsub // unroll, step, 0)
        cp, pg, dq = cps[...], pgs[...], dqa[...]
        for j in range(nsub):
            cp, pg, dq = sub(i * nsub + j, True, cp, pg, dq)
        dq_ref[...] = dq.astype(BF16)

        @pl.when(i == nq - 1)
        def _():
            ck = pltpu.make_async_copy(dka, dk_hbm.at[h], sem.at[0])
            cv = pltpu.make_async_copy(dva, dv_hbm.at[h], sem.at[1])
            ck.start()
            cv.start()
            ck.wait()
            cv.wait()

    qo = pl.BlockSpec((TQ, BLOCK), lambda h, i: (i, h))
    return pl.pallas_call(
        body, name=name, grid=(SB_HEADS, nq),
        in_specs=[pl.BlockSpec((TQ, BLOCK), lambda h, i: (i, SB_Q_COL + h)),
                  pl.BlockSpec((T, BLOCK), lambda h, i: (0, SB_K_COL + h)),
                  pl.BlockSpec((T, BLOCK), lambda h, i: (0, SB_V_COL + h)),
                  qo, qo,
                  pl.BlockSpec((SB_KS, SB_KS), lambda h, i: (0, 0))],
        out_specs=[qo, pl.BlockSpec(memory_space=pl.ANY), pl.BlockSpec(memory_space=pl.ANY)],
        out_shape=[jax.ShapeDtypeStruct((T, SB_HEADS * BLOCK), BF16),
                   jax.ShapeDtypeStruct((SB_HEADS, T, BLOCK), F32),
                   jax.ShapeDtypeStruct((SB_HEADS, T, BLOCK), F32)],
        scratch_shapes=[pltpu.VMEM((T, BLOCK), F32), pltpu.VMEM((T, BLOCK), F32),
                        pltpu.VMEM((TQ, BLOCK), F32), pltpu.VMEM((TQ, BLOCK), F32),
                        pltpu.VMEM((TQ, BLOCK), F32), pltpu.SemaphoreType.DMA((2,))],
        compiler_params=_params("arbitrary", "arbitrary"),
    )(proj_lo, proj_lo, proj_lo, ctot, dy, _sb_tri())


def _ret_tables(T):
    half = RET_DK // 2
    inv = ROPE_BASE ** (-jnp.arange(half, dtype=F32) / half)
    ang = jnp.arange(T, dtype=jnp.int32).astype(F32)[:, None] * inv[None, :]
    cos, sin = jnp.cos(ang), jnp.sin(ang)
    cos_t = jnp.tile(cos, (1, 2 * RET_HEADS))
    sin_t = jnp.tile(jnp.concatenate([-sin, sin], axis=1), (1, RET_HEADS))
    lg = np.log1p(-np.exp2(-5.0 - np.arange(RET_HEADS)))
    n = np.arange(BLOCK)
    diff = n[:, None] - n[None, :]
    dmat = np.where(diff >= 0, np.exp(np.minimum(diff, BLOCK)[None] * lg[:, None, None]), 0.0)
    zeta = np.repeat(np.exp((BLOCK - 1 - n)[:, None] * lg[None, :]), RET_DK, axis=1)
    xi = np.repeat(np.exp((n + 1)[:, None] * lg[None, :]), RET_DK, axis=1)
    chunk_decay = [float(v) for v in np.exp(BLOCK * lg)]
    return (cos_t, sin_t, jnp.asarray(dmat, F32), jnp.asarray(xi, F32), jnp.asarray(zeta, F32)), chunk_decay


def _swap_halves(x):
    lane = lax.broadcasted_iota(jnp.int32, x.shape, 1)
    lower = (lane % RET_DK) < (RET_DK // 2)
    w = x.shape[1]
    return jnp.where(lower, pltpu.roll(x, w - RET_DK // 2, 1), pltpu.roll(x, RET_DK // 2, 1))


def _gn_gate(o, rg, w):
    mu = jnp.mean(o, axis=-1, keepdims=True)
    xc = o - mu
    var = jnp.mean(xc * xc, axis=-1, keepdims=True)
    return (rg * jax.nn.sigmoid(rg)) * (xc * lax.rsqrt(var + EPS) * w)


def _ret_fwd(proj_hi, ret_norm_w, tables, chunk_decay, name):
    T = proj_hi.shape[0]
    nc = T // BLOCK
    cos_t, sin_t, dmat, xi, zeta = tables

    def body(rq_ref, rk_ref, rv_ref, rg_ref, w_ref, cos_ref, sin_ref, d_ref, xi_ref, ze_ref,
             y_ref, o_ref, st_ref, rs):
        @pl.when(pl.program_id(0) == 0)
        def _():
            rs[...] = jnp.zeros_like(rs)

        cos, sin = cos_ref[...], sin_ref[...]
        rq, rk = rq_ref[...], rk_ref[...]
        q = rq * cos + _swap_halves(rq) * sin
        k = (rk * cos + _swap_halves(rk) * sin) * RET_SCALE
        qb, kb = q.astype(BF16), k.astype(BF16)
        qx, kz = (q * xi_ref[...]).astype(BF16), (k * ze_ref[...]).astype(BF16)
        for h in range(RET_HEADS):
            sl = slice(h * RET_DK, (h + 1) * RET_DK)
            sv = slice(h * RET_DV, (h + 1) * RET_DV)
            vb = rv_ref[:, sv].astype(BF16)
            r = rs[h]
            st_ref[0, h] = r
            intra = _dot(qb[:, sl], kb[:, sl], _NT) * d_ref[h]
            o = _dot(intra.astype(BF16), vb, _NN) + _dot(qx[:, sl], r.astype(BF16), _NN)
            rs[h] = r * chunk_decay[h] + _dot(kz[:, sl], vb, _TN)
            o_ref[:, sv] = o
            y_ref[:, sv] = _gn_gate(o, rg_ref[:, sv], w_ref[:, sv]).astype(BF16)

    qk = (BLOCK, RET_HEADS * RET_DK)
    vv = (BLOCK, RET_HEADS * RET_DV)
    const = lambda shape: pl.BlockSpec(shape, lambda n: (0,) * len(shape))
    return pl.pallas_call(
        body, name=name, grid=(nc,),
        in_specs=[pl.BlockSpec(qk, lambda n: (n, 0)), pl.BlockSpec(qk, lambda n: (n, 1)),
                  pl.BlockSpec(vv, lambda n: (n, 1)), pl.BlockSpec(vv, lambda n: (n, 2)),
                  const((1, RET_HEADS * RET_DV)),
                  pl.BlockSpec(qk, lambda n: (n, 0)), pl.BlockSpec(qk, lambda n: (n, 0)),
                  const((RET_HEADS, BLOCK, BLOCK)), const(qk), const(qk)],
        out_specs=[pl.BlockSpec(vv, lambda n: (n, 0)), pl.BlockSpec(vv, lambda n: (n, 0)),
                   pl.BlockSpec((1, RET_HEADS, RET_DK, RET_DV), lambda n: (n, 0, 0, 0))],
        out_shape=[jax.ShapeDtypeStruct((T, RET_HEADS * RET_DV), BF16),
                   jax.ShapeDtypeStruct((T, RET_HEADS * RET_DV), F32),
                   jax.ShapeDtypeStruct((nc, RET_HEADS, RET_DK, RET_DV), F32)],
        scratch_shapes=[pltpu.VMEM((RET_HEADS, RET_DK, RET_DV), F32)],
        compiler_params=_params("arbitrary"),
    )(proj_hi, proj_hi, proj_hi, proj_hi, ret_norm_w.reshape(1, -1), cos_t, sin_t, dmat, xi, zeta)


def _ret_bwd(proj_hi, ret_norm_w, o_r, states, dy, tables, chunk_decay, name):
    T = proj_hi.shape[0]
    nc = T // BLOCK
    cos_t, sin_t, dmat, xi, zeta = tables

    def body(rq_ref, rk_ref, rv_ref, rg_ref, w_ref, cos_ref, sin_ref, d_ref, xi_ref, ze_ref,
             o_ref, st_ref, dy_ref, dq_ref, dk_ref, dv_ref, dg_ref, dw_ref, drs, dqs, dks):
        @pl.when(pl.program_id(0) == 0)
        def _():
            drs[...] = jnp.zeros_like(drs)
            dw_ref[...] = jnp.zeros_like(dw_ref)

        cos, sin = cos_ref[...], sin_ref[...]
        rq, rk = rq_ref[...], rk_ref[...]
        q = rq * cos + _swap_halves(rq) * sin
        k = (rk * cos + _swap_halves(rk) * sin) * RET_SCALE
        qb, kb = q.astype(BF16), k.astype(BF16)
        qx, kz = (q * xi_ref[...]).astype(BF16), (k * ze_ref[...]).astype(BF16)
        for h in range(RET_HEADS):
            sl = slice(h * RET_DK, (h + 1) * RET_DK)
            sv = slice(h * RET_DV, (h + 1) * RET_DV)
            _, vjp = jax.vjp(_gn_gate, o_ref[:, sv], rg_ref[:, sv], w_ref[:, sv])
            do, dg, dw = vjp(dy_ref[:, sv])
            dg_ref[:, sv] = dg.astype(BF16)
            dw_ref[:, sv] += dw
            dob = do.astype(BF16)
            vb = rv_ref[:, sv].astype(BF16)
            rb = st_ref[0, h].astype(BF16)
            dr = drs[h]
            drb = dr.astype(BF16)
            dmat_h = d_ref[h]
            a = (_dot(dob, vb, _NT) * dmat_h).astype(BF16)
            p = (_dot(qb[:, sl], kb[:, sl], _NT) * dmat_h).astype(BF16)
            dqs[:, sl] = _dot(a, kb[:, sl], _NN) + _dot(dob, rb, _NT) * xi_ref[:, sl]
            dks[:, sl] = _dot(a, qb[:, sl], _TN) + _dot(vb, drb, _NT) * ze_ref[:, sl]
            dv_ref[:, sv] = (_dot(p, dob, _TN) + _dot(kz[:, sl], drb, _NN)).astype(BF16)
            drs[h] = dr * chunk_decay[h] + _dot(qx[:, sl], dob, _TN)
        dq = dqs[...]
        dk = dks[...] * RET_SCALE
        dq_ref[...] = (dq * cos + _swap_halves(dq * sin)).astype(BF16)
        dk_ref[...] = (dk * cos + _swap_halves(dk * sin)).astype(BF16)

    qk = (BLOCK, RET_HEADS * RET_DK)
    vv = (BLOCK, RET_HEADS * RET_DV)
    rev = lambda n: nc - 1 - n
    const = lambda shape: pl.BlockSpec(shape, lambda n: (0,) * len(shape))
    return pl.pallas_call(
        body, name=name, grid=(nc,),
        in_specs=[pl.BlockSpec(qk, lambda n: (rev(n), 0)), pl.BlockSpec(qk, lambda n: (rev(n), 1)),
                  pl.BlockSpec(vv, lambda n: (rev(n), 1)), pl.BlockSpec(vv, lambda n: (rev(n), 2)),
                  const((1, RET_HEADS * RET_DV)),
                  pl.BlockSpec(qk, lambda n: (rev(n), 0)), pl.BlockSpec(qk, lambda n: (rev(n), 0)),
                  const((RET_HEADS, BLOCK, BLOCK)), const(qk), const(qk),
                  pl.BlockSpec(vv, lambda n: (rev(n), 0)),
                  pl.BlockSpec((1, RET_HEADS, RET_DK, RET_DV), lambda n: (rev(n), 0, 0, 0)),
                  pl.BlockSpec(vv, lambda n: (rev(n), 0))],
        out_specs=[pl.BlockSpec(qk, lambda n: (rev(n), 0)), pl.BlockSpec(qk, lambda n: (rev(n), 0)),
                   pl.BlockSpec(vv, lambda n: (rev(n), 0)), pl.BlockSpec(vv, lambda n: (rev(n), 0)),
                   const((1, RET_HEADS * RET_DV))],
        out_shape=[jax.ShapeDtypeStruct((T, RET_HEADS * RET_DK), BF16)] * 2
        + [jax.ShapeDtypeStruct((T, RET_HEADS * RET_DV), BF16)] * 2
        + [jax.ShapeDtypeStruct((1, RET_HEADS * RET_DV), F32)],
        scratch_shapes=[pltpu.VMEM((RET_HEADS, RET_DK, RET_DV), F32), pltpu.VMEM(qk, F32), pltpu.VMEM(qk, F32)],
        compiler_params=_params("arbitrary"),
    )(proj_hi, proj_hi, proj_hi, proj_hi, ret_norm_w.reshape(1, -1), cos_t, sin_t, dmat, xi, zeta,
      o_r, states, dy)


GATE_BLK = 512
GATE_FIRST_BLK = 3


def _gated(g0, g1, g2, b0, b1, b2, pa, pb, pc):
    return jax.nn.sigmoid(g0 + b0) * pa + jax.nn.sigmoid(g1 + b1) * pb + jax.nn.sigmoid(g2 + b2) * pc


def _gate_specs(tm):
    return [pl.BlockSpec((tm, GATE_BLK), functools.partial(lambda i, c: (i, c), c=GATE_FIRST_BLK + j))
            for j in range(6)]


def _gate_args(g, bg_ref):
    gi = [jnp.concatenate([g[2 * j][...], g[2 * j + 1][...]], axis=1) for j in range(3)]
    return gi + [bg_ref[j:j + 1, :] for j in range(3)]


def _merge_fwd(proj_hi, b_gate, pa, pb, pc, name):
    T = proj_hi.shape[0]
    tm = _tile(T, (256, 128))

    def body(g0, g1, g2, g3, g4, g5, bg_ref, pa_ref, pb_ref, pc_ref, o_ref):
        args = _gate_args((g0, g1, g2, g3, g4, g5), bg_ref)
        o_ref[...] = _gated(*args, pa_ref[...], pb_ref[...], pc_ref[...]).astype(BF16)

    row = pl.BlockSpec((tm, D_MODEL), lambda i: (i, 0))
    return pl.pallas_call(
        body, name=name, grid=(T // tm,),
        in_specs=_gate_specs(tm) + [pl.BlockSpec((3, D_MODEL), lambda i: (0, 0)), row, row, row],
        out_specs=row, out_shape=jax.ShapeDtypeStruct((T, D_MODEL), BF16),
        compiler_params=_params("parallel"),
    )(*([proj_hi] * 6), b_gate, pa, pb, pc)


def _merge_bwd(proj_hi, b_gate, pa, pb, pc, dm, name):
    T = proj_hi.shape[0]
    tm = _tile(T, (256, 128))

    def body(g0, g1, g2, g3, g4, g5, bg_ref, pa_ref, pb_ref, pc_ref, dm_ref,
             dgi_ref, dbg_ref, dpa_ref, dpb_ref, dpc_ref):
        args = _gate_args((g0, g1, g2, g3, g4, g5), bg_ref)
        _, vjp = jax.vjp(_gated, *args, pa_ref[...], pb_ref[...], pc_ref[...])
        d = vjp(dm_ref[...])
        for j in range(3):
            dgi_ref[:, j * D_MODEL:(j + 1) * D_MODEL] = d[j].astype(BF16)
        dpa_ref[...] = d[6].astype(BF16)
        dpb_ref[...] = d[7].astype(BF16)
        dpc_ref[...] = d[8].astype(BF16)

        @pl.when(pl.program_id(0) == 0)
        def _():
            dbg_ref[...] = jnp.zeros_like(dbg_ref)

        for j in range(3):
            dbg_ref[j:j + 1, :] += d[3 + j]

    row = pl.BlockSpec((tm, D_MODEL), lambda i: (i, 0))
    vec = pl.BlockSpec((3, D_MODEL), lambda i: (0, 0))
    return pl.pallas_call(
        body, name=name, grid=(T // tm,),
        in_specs=_gate_specs(tm) + [vec, row, row, row, row],
        out_specs=[pl.BlockSpec((tm, 3 * D_MODEL), lambda i: (i, 0)), vec, row, row, row],
        out_shape=[jax.ShapeDtypeStruct((T, 3 * D_MODEL), BF16), jax.ShapeDtypeStruct((3, D_MODEL), F32)]
        + [jax.ShapeDtypeStruct((T, D_MODEL), BF16)] * 3,
        compiler_params=_params("arbitrary"),
    )(*([proj_hi] * 6), b_gate, pa, pb, pc, dm)


FFN_TM = 256


def _gelu(u):
    return 0.5 * u * (1.0 + jnp.tanh(0.7978845608028654 * (u + 0.044715 * (u * u * u))))


def _conv_taps(u0, prev8, first):
    prev8 = jnp.where(first, 0.0, prev8)
    row = lax.broadcasted_iota(jnp.int32, u0.shape, 0)
    s1 = jnp.where(row == 0, prev8[7:8], pltpu.roll(u0, 1, 0))
    s2 = jnp.where(row == 0, prev8[6:7], jnp.where(row == 1, prev8[7:8], pltpu.roll(u0, 2, 0)))
    return s1, s2


def _ffn_specs(T, tm):
    row = pl.BlockSpec((tm, D_FF), lambda i: (i, 0))
    prev = pl.BlockSpec((8, D_FF), lambda i: (jnp.maximum(i * (tm // 8) - 1, 0), 0))
    nxt = pl.BlockSpec((8, D_FF), lambda i: (jnp.minimum((i + 1) * (tm // 8), T // 8 - 1), 0))
    return row, prev, nxt


def _ffn_mid_fwd(u0, gt, cw, cb, name):
    T = u0.shape[0]
    tm = _tile(T, (FFN_TM, 128))
    row, prev, _ = _ffn_specs(T, tm)

    def body(u_ref, p_ref, g_ref, cw_ref, cb_ref, f_ref):
        u0 = u_ref[...]
        s1, s2 = _conv_taps(u0, p_ref[...], pl.program_id(0) == 0)
        cw = cw_ref[...]
        u = cw[0:1] * s2 + cw[1:2] * s1 + cw[2:3] * u0 + cb_ref[...]
        f_ref[...] = (_gelu(u) * g_ref[...]).astype(BF16)

    return pl.pallas_call(
        body, name=name, grid=(T // tm,),
        in_specs=[row, prev, row, pl.BlockSpec((3, D_FF), lambda i: (0, 0)), pl.BlockSpec((1, D_FF), lambda i: (0, 0))],
        out_specs=row, out_shape=jax.ShapeDtypeStruct((T, D_FF), BF16),
        compiler_params=_params("parallel"),
    )(u0, u0, gt, cw, cb.reshape(1, D_FF))


def _ffn_mid_bwd_a(u0, gt, cw, cb, df, name):
    T = u0.shape[0]
    tm = _tile(T, (FFN_TM, 128))
    row, prev, _ = _ffn_specs(T, tm)

    def body(u_ref, p_ref, g_ref, cw_ref, cb_ref, df_ref, du_ref, dg_ref, dcw_ref, dcb_ref):
        u0 = u_ref[...]
        s1, s2 = _conv_taps(u0, p_ref[...], pl.program_id(0) == 0)
        cw = cw_ref[...]
        u = cw[0:1] * s2 + cw[1:2] * s1 + cw[2:3] * u0 + cb_ref[...]
        a, vjp = jax.vjp(_gelu, u)
        df = df_ref[...]
        dg_ref[...] = (df * a).astype(BF16)
        du = vjp(df * g_ref[...])[0]
        du_ref[...] = du

        @pl.when(pl.program_id(0) == 0)
        def _():
            dcw_ref[...] = jnp.zeros_like(dcw_ref)
            dcb_ref[...] = jnp.zeros_like(dcb_ref)

        dcw_ref[0:1, :] += jnp.sum(du * s2, axis=0, keepdims=True)
        dcw_ref[1:2, :] += jnp.sum(du * s1, axis=0, keepdims=True)
        dcw_ref[2:3, :] += jnp.sum(du * u0, axis=0, keepdims=True)
        dcb_ref[...] += jnp.sum(du, axis=0, keepdims=True)

    c3 = pl.BlockSpec((3, D_FF), lambda i: (0, 0))
    c1 = pl.BlockSpec((1, D_FF), lambda i: (0, 0))
    return pl.pallas_call(
        body, name=name, grid=(T // tm,),
        in_specs=[row, prev, row, c3, c1, row], out_specs=[row, row, c3, c1],
        out_shape=[jax.ShapeDtypeStruct((T, D_FF), F32), jax.ShapeDtypeStruct((T, D_FF), BF16),
                   jax.ShapeDtypeStruct((3, D_FF), F32), jax.ShapeDtypeStruct((1, D_FF), F32)],
        compiler_params=_params("arbitrary"),
    )(u0, u0, gt, cw, cb.reshape(1, D_FF), df)


def _ffn_mid_bwd_b(du, cw, name):
    T = du.shape[0]
    tm = _tile(T, (FFN_TM, 128))
    row, _, nxt = _ffn_specs(T, tm)

    def body(du_ref, n_ref, cw_ref, o_ref):
        du = du_ref[...]
        nx = jnp.where(pl.program_id(0) == T // tm - 1, 0.0, n_ref[...])
        r = lax.broadcasted_iota(jnp.int32, du.shape, 0)
        u1 = jnp.where(r == tm - 1, nx[0:1], pltpu.roll(du, tm - 1, 0))
        u2 = jnp.where(r == tm - 1, nx[1:2], jnp.where(r == tm - 2, nx[0:1], pltpu.roll(du, tm - 2, 0)))
        cw = cw_ref[...]
        o_ref[...] = (cw[2:3] * du + cw[1:2] * u1 + cw[0:1] * u2).astype(BF16)

    return pl.pallas_call(
        body, name=name, grid=(T // tm,),
        in_specs=[row, nxt, pl.BlockSpec((3, D_FF), lambda i: (0, 0))],
        out_specs=row, out_shape=jax.ShapeDtypeStruct((T, D_FF), BF16),
        compiler_params=_params("parallel"),
    )(du, du, cw)


def _mesh_peers():
    x, y, c = lax.axis_index("x"), lax.axis_index("y"), lax.axis_index("c")
    peers = []
    for k in range(1, N_DEV):
        px = 1 - x if k & 4 else x
        py = 1 - y if k & 2 else y
        pc = 1 - c if k & 1 else c
        peers.append(((px, py, pc), 4 * px + 2 * py + pc))
    return 4 * x + 2 * y + c, peers


def _exchange(src, name, scatter):
    shape = src.shape[-2:]

    def body(s_ref, o_ref, send_sems, recv_sems, local_sem):
        me, peers = _mesh_peers()
        mine = pltpu.make_async_copy(s_ref.at[me] if scatter else s_ref, o_ref.at[me], local_sem)
        mine.start()
        sends, recvs = [], []
        for k, (dev, idx) in enumerate(peers):
            sends.append(pltpu.make_async_remote_copy(
                src_ref=s_ref.at[idx] if scatter else s_ref, dst_ref=o_ref.at[me],
                send_sem=send_sems.at[k], recv_sem=recv_sems.at[k],
                device_id=dev, device_id_type=pl.DeviceIdType.MESH))
            recvs.append(pltpu.make_async_remote_copy(
                src_ref=s_ref.at[idx] if scatter else s_ref, dst_ref=o_ref.at[idx],
                send_sem=send_sems.at[k], recv_sem=recv_sems.at[k],
                device_id=dev, device_id_type=pl.DeviceIdType.MESH))
        for cp in sends:
            cp.start()
        for cp in recvs:
            cp.wait_recv()
        for cp in sends:
            cp.wait_send()
        mine.wait()

    return pl.pallas_call(
        body, name=name,
        in_specs=[pl.BlockSpec(memory_space=pl.ANY)], out_specs=pl.BlockSpec(memory_space=pl.ANY),
        out_shape=jax.ShapeDtypeStruct((N_DEV,) + shape, src.dtype),
        scratch_shapes=[pltpu.SemaphoreType.DMA((N_DEV - 1,)), pltpu.SemaphoreType.DMA((N_DEV - 1,)),
                        pltpu.SemaphoreType.DMA],
    )(src)


def _sum_devices(parts, name):
    _, R, C = parts.shape
    tr = _tile(R, (256, 128, 64, 32, 16, 8))

    def body(p_ref, o_ref):
        acc = p_ref[0]
        for j in range(1, N_DEV):
            acc = acc + p_ref[j]
        o_ref[...] = acc

    return pl.pallas_call(
        body, name=name, grid=(R // tr,),
        in_specs=[pl.BlockSpec((N_DEV, tr, C), lambda i: (0, i, 0))],
        out_specs=pl.BlockSpec((tr, C), lambda i: (i, 0)),
        out_shape=jax.ShapeDtypeStruct((R, C), F32),
        compiler_params=_params("parallel"),
    )(parts)


def _adamw(w, g, m, v, name):
    shape = w.shape
    C = shape[-1]
    R = int(np.prod(shape[:-1])) if len(shape) > 1 else 1
    tr = _tile(R, (256, 128, 64, 32, 16, 8))

    def body(w_ref, g_ref, m_ref, v_ref, d_ref, nm_ref, nv_ref):
        g = g_ref[...]
        m = ADAM_B1 * m_ref[...] + (1.0 - ADAM_B1) * g
        v = ADAM_B2 * v_ref[...] + (1.0 - ADAM_B2) * (g * g)
        m_hat = m / (1.0 - ADAM_B1 ** ADAM_STEP)
        v_hat = v / (1.0 - ADAM_B2 ** ADAM_STEP)
        d_ref[...] = -ADAM_LR * (m_hat / (jnp.sqrt(v_hat) + ADAM_EPS) + ADAM_WD * w_ref[...])
        nm_ref[...] = m
        nv_ref[...] = v

    blk = pl.BlockSpec((tr, C), lambda i: (i, 0))
    outs = pl.pallas_call(
        body, name=name, grid=(R // tr,), in_specs=[blk] * 4, out_specs=[blk] * 3,
        out_shape=[jax.ShapeDtypeStruct((R, C), F32)] * 3,
        compiler_params=_params("parallel"),
    )(*[t.reshape(R, C) for t in (w, g, m, v)])
    return [o.reshape(shape) for o in outs]


PACK_ROWS = (1056, 352, 352, 128, 352, 32, 64, 64)
PACK_LAYER = sum(PACK_ROWS)
SMALL_SIZES = (("rel_bias", 384), ("norm_mix_w", 4096), ("ret_norm_w", 2048), ("norm_ffn_w", 4096),
               ("conv_b", 11264), ("final_norm_w", 1024), ("b_gate", 12288), ("conv_w", 33792))
SMALL_ROWS = 72


def _layer_forward(x, wl, tabs, l):
    sv = {"x_in": x}
    h = _rms_fwd(x, wl["norm_mix_w"], f"rms_mix_fwd_{l}")
    lo = _matmul(h, wl["win_lo"], "nt", f"in_proj_lo_{l}", out_dtype=BF16)
    hi = _matmul(h, wl["win_hi"], "nt", f"in_proj_hi_{l}")
    outs, lses = [], []
    for g in range(3):
        o, s = _a_attn_fwd(lo, tabs["bias"], g, f"a_fwd_{g}_{l}")
        outs.append(o)
        lses.append(s)
    y_a = _a_merge_fwd(outs, lses, f"a_merge_fwd_{l}")
    y_b, ctot = _sb_fwd(lo, f"sb_fwd_{l}")
    y_c, o_r, states = _ret_fwd(hi, wl["ret_norm_w"], tabs["ret"], tabs["decay"], f"ret_fwd_{l}")
    pa = _matmul(y_a, wl["wpa"], "nt", f"proj_a_{l}")
    pb = _matmul(y_b, wl["wpb"], "nt", f"proj_b_{l}")
    pc = _matmul(y_c, wl["wpc"], "nt", f"proj_c_{l}")
    merged = _merge_fwd(hi, wl["b_gate"], pa, pb, pc, f"merge_fwd_{l}")
    x_mid = _matmul(merged, wl["wout"], "nn", f"out_proj_{l}", add=x)
    h2 = _rms_fwd(x_mid, wl["norm_ffn_w"], f"rms_ffn_fwd_{l}")
    u0 = _matmul(h2, wl["wup"], "nt", f"ffn_up_{l}")
    gt = _matmul(h2, wl["wgate"], "nt", f"ffn_gate_{l}")
    f = _ffn_mid_fwd(u0, gt, wl["conv_w"], wl["conv_b"], f"ffn_mid_fwd_{l}")
    x_out = _matmul(f, wl["wdown"], "nn", f"ffn_down_{l}", add=x_mid)
    sv.update(h=h, lo=lo, hi=hi, outs=outs, lses=lses, y_a=y_a, y_b=y_b, ctot=ctot, y_c=y_c, o_r=o_r,
              states=states, pa=pa, pb=pb, pc=pc, merged=merged, x_mid=x_mid, h2=h2, u0=u0, gt=gt, f=f)
    return x_out, sv


def _layer_backward(dx, sv, wl, tabs, l):
    df = _matmul(dx, wl["wdown"], "nt", f"ffn_down_dx_{l}")
    d_wdown = _matmul(sv["f"], dx, "tn", f"ffn_down_dw_{l}")
    du, dgt, d_cw, d_cb = _ffn_mid_bwd_a(sv["u0"], sv["gt"], wl["conv_w"], wl["conv_b"], df, f"ffn_mid_bwd_a_{l}")
    du0 = _ffn_mid_bwd_b(du, wl["conv_w"], f"ffn_mid_bwd_b_{l}")
    dh2 = _matmul(du0, wl["wup"], "nn", f"ffn_up_dx_{l}")
    dh2 = _matmul(dgt, wl["wgate"], "nn", f"ffn_gate_dx_{l}", add=dh2)
    d_wup = _matmul(du0, sv["h2"], "tn", f"ffn_up_dw_{l}")
    d_wgate = _matmul(dgt, sv["h2"], "tn", f"ffn_gate_dw_{l}")
    dx_mid, d_nffn = _rms_bwd(sv["x_mid"], wl["norm_ffn_w"], dh2, dx, f"rms_ffn_bwd_{l}")
    dm = _matmul(dx_mid, wl["wout"], "nt", f"out_proj_dx_{l}")
    d_wout = _matmul(sv["merged"], dx_mid, "tn", f"out_proj_dw_{l}")
    dgi, d_bg, dpa, dpb, dpc = _merge_bwd(sv["hi"], wl["b_gate"], sv["pa"], sv["pb"], sv["pc"], dm, f"merge_bwd_{l}")
    dy_a = _matmul(dpa, wl["wpa"], "nn", f"proj_a_dx_{l}")
    dy_b = _matmul(dpb, wl["wpb"], "nn", f"proj_b_dx_{l}")
    dy_c = _matmul(dpc, wl["wpc"], "nn", f"proj_c_dx_{l}")
    d_wpa = _matmul(dpa, sv["y_a"], "tn", f"proj_a_dw_{l}")
    d_wpb = _matmul(dpb, sv["y_b"], "tn", f"proj_b_dw_{l}")
    d_wpc = _matmul(dpc, sv["y_c"], "tn", f"proj_c_dw_{l}")
    d_rq, d_rk, d_rv, d_rg, d_rnw = _ret_bwd(sv["hi"], wl["ret_norm_w"], sv["o_r"], sv["states"], dy_c,
                                             tabs["ret"], tabs["decay"], f"ret_bwd_{l}")
    d_sq, d_sk, d_sv = _sb_bwd(sv["lo"], sv["ctot"], dy_b, f"sb_bwd_{l}")
    douts, dlses = _a_merge_bwd(sv["outs"], sv["lses"], dy_a, f"a_merge_bwd_{l}")
    dqs, dks, dvs, dbs = [], [], [], []
    for g in range(3):
        dq, dk, dv, db = _a_attn_bwd(sv["lo"], tabs["bias"], sv["outs"][g], sv["lses"][g], douts[g], dlses[g],
                                     g, f"a_bwd_{g}_{l}")
        dqs.append(dq)
        dks.append(dk)
        dvs.append(dv)
        dbs.append(db)
    heads = lambda t: [t[i].astype(BF16) for i in range(SB_HEADS)]
    dlo = jnp.concatenate(dqs + dks + dvs + [d_sq] + heads(d_sk) + heads(d_sv), axis=1)
    dhi = jnp.concatenate([d_rq, d_rk, d_rv, d_rg, dgi], axis=1)
    dh = _matmul(dlo, wl["win_lo"], "nn", f"in_proj_lo_dx_{l}")
    dh = _matmul(dhi, wl["win_hi"], "nn", f"in_proj_hi_dx_{l}", add=dh)
    d_win = jnp.concatenate([_matmul(dlo, sv["h"], "tn", f"in_proj_lo_dw_{l}"),
                             _matmul(dhi, sv["h"], "tn", f"in_proj_hi_dw_{l}")], axis=0)
    dx_in, d_nmix = _rms_bwd(sv["x_in"], wl["norm_mix_w"], dh, dx_mid, f"rms_mix_bwd_{l}")
    chunks = lambda t: t.reshape(N_DEV, -1, D_MODEL)
    big = jnp.concatenate([chunks(d_win), chunks(d_wup), chunks(d_wgate), chunks(d_wout), chunks(d_wdown),
                           chunks(d_wpa), chunks(d_wpb), chunks(d_wpc)], axis=1)
    small = dict(norm_mix_w=d_nmix[0], ret_norm_w=d_rnw[0], norm_ffn_w=d_nffn[0], conv_b=d_cb[0],
                 b_gate=d_bg, conv_w=d_cw, dbias=jnp.concatenate(dbs, axis=0))
    return dx_in, big, small


def kernel(x, rel_bias, norm_mix_w, w_in, b_gate, ret_norm_w, w_proj_a, w_proj_b, w_proj_c, w_out, norm_ffn_w, w_up, w_gate, conv_w, conv_b, w_down, final_norm_w, loss_target, m_rel_bias, m_norm_mix_w, m_w_in, m_b_gate, m_ret_norm_w, m_w_proj_a, m_w_proj_b, m_w_proj_c, m_w_out, m_norm_ffn_w, m_w_up, m_w_gate, m_conv_w, m_conv_b, m_w_down, m_final_norm_w, v_rel_bias, v_norm_mix_w, v_w_in, v_b_gate, v_ret_norm_w, v_w_proj_a, v_w_proj_b, v_w_proj_c, v_w_out, v_norm_ffn_w, v_w_up, v_w_gate, v_conv_w, v_conv_b, v_w_down, v_final_norm_w):
    T = x.shape[1]
    me = 4 * lax.axis_index("x") + 2 * lax.axis_index("y") + lax.axis_index("c")

    rows = []
    for l in range(DEPTH):
        rows += [w_in[l].T, w_up[l].T, w_gate[l].T, w_out[l], w_down[l],
                 w_proj_a[l].T.reshape(-1, D_MODEL), w_proj_b[l].T.reshape(-1, D_MODEL),
                 w_proj_c[l].T.reshape(-1, D_MODEL)]
    wall = _exchange(jnp.concatenate(rows, axis=0).astype(BF16), "gather_weights", scatter=False)
    n_bg, n_cw = b_gate.size, conv_w.size
    tiny = jnp.concatenate([b_gate.reshape(-1), conv_w.reshape(-1), jnp.zeros((8 * D_MODEL - n_bg - n_cw,), F32)])
    tall = _exchange(tiny.reshape(8, D_MODEL), "gather_small_weights", scatter=False).reshape(N_DEV, -1)
    spread = lambda t, w: t.reshape(N_DEV, DEPTH, 3, w).transpose(1, 2, 0, 3).reshape(DEPTH, 3, N_DEV * w)
    b_gate_full = spread(tall[:, :n_bg], b_gate.shape[-1])
    conv_w_full = spread(tall[:, n_bg:n_bg + n_cw], conv_w.shape[-1])

    def layer_weights(l):
        base = l * PACK_LAYER
        offs = np.cumsum((0,) + PACK_ROWS)
        seg = lambda j: wall[:, base + offs[j]:base + offs[j + 1], :]
        win = seg(0).reshape(-1, D_MODEL)
        unpack = lambda j, k: seg(j).reshape(N_DEV, D_MODEL // N_DEV, k).reshape(D_MODEL, k)
        return dict(win_lo=win[:LO_WIDTH], win_hi=win[LO_WIDTH:], wup=seg(1).reshape(-1, D_MODEL),
                    wgate=seg(2).reshape(-1, D_MODEL), wout=seg(3).reshape(-1, D_MODEL),
                    wdown=seg(4).reshape(-1, D_MODEL), wpa=unpack(5, 256), wpb=unpack(6, 512), wpc=unpack(7, 512),
                    norm_mix_w=norm_mix_w[l], norm_ffn_w=norm_ffn_w[l], ret_norm_w=ret_norm_w[l],
                    b_gate=b_gate_full[l], conv_w=conv_w_full[l], conv_b=conv_b[l])

    buckets = _a_buckets()
    ret_tabs, decay = _ret_tables(T)
    tabs = dict(bias=_a_bias_tables(rel_bias, buckets), ret=ret_tabs, decay=decay)

    xs = x[0]
    saved, wls = [], []
    for l in range(DEPTH):
        wls.append(layer_weights(l))
        xs, sv = _layer_forward(xs, wls[l], tabs, l)
        saved.append(sv)
    loss_tile, dx, d_final = _loss_head(xs, final_norm_w, loss_target[0], "loss_head")
    loss = lax.psum(loss_tile[0, 0], ("x", "y", "c"))

    bigs, smalls = [None] * DEPTH, [None] * DEPTH
    for l in reversed(range(DEPTH)):
        dx, bigs[l], smalls[l] = _layer_backward(dx, saved[l], wls[l], tabs, l)
    grad_x = dx[None]

    parts = _exchange(jnp.concatenate(bigs, axis=1), "scatter_grads", scatter=True)
    mine = _sum_devices(parts, "sum_grads")
    dbias = smalls[0]["dbias"] + smalls[1]["dbias"] + smalls[2]["dbias"] + smalls[3]["dbias"]
    small_vals = dict(rel_bias=_a_bias_grad(dbias, buckets), final_norm_w=d_final[0])
    for name in ("norm_mix_w", "ret_norm_w", "norm_ffn_w", "conv_b", "b_gate", "conv_w"):
        small_vals[name] = jnp.stack([smalls[l][name] for l in range(DEPTH)])
    flat = jnp.concatenate([small_vals[n].reshape(-1) for n, _ in SMALL_SIZES])
    flat = jnp.concatenate([flat, jnp.zeros((SMALL_ROWS * D_MODEL - flat.shape[0],), F32)])
    sparts = _exchange(flat.reshape(SMALL_ROWS, D_MODEL), "gather_small_grads", scatter=False)
    ssum = _sum_devices(sparts, "sum_small_grads").reshape(-1)

    grads = {}
    off = 0
    for name, size in SMALL_SIZES:
        grads[name] = ssum[off:off + size]
        off += size
    grads["rel_bias"] = grads["rel_bias"].reshape(REL_BUCKETS, 12)
    for name in ("norm_mix_w", "norm_ffn_w"):
        grads[name] = grads[name].reshape(DEPTH, D_MODEL)
    grads["ret_norm_w"] = grads["ret_norm_w"].reshape(DEPTH, -1)
    grads["conv_b"] = grads["conv_b"].reshape(DEPTH, D_FF)
    bw, cwid = b_gate.shape[-1], conv_w.shape[-1]
    grads["b_gate"] = lax.dynamic_slice_in_dim(grads["b_gate"].reshape(DEPTH, 3, -1), me * bw, bw, axis=2)
    grads["conv_w"] = lax.dynamic_slice_in_dim(grads["conv_w"].reshape(DEPTH, 3, -1), me * cwid, cwid, axis=2)

    offs = np.cumsum((0,) + PACK_ROWS)
    per_layer = mine.reshape(DEPTH, PACK_LAYER, D_MODEL)
    seg = lambda j: per_layer[:, offs[j]:offs[j + 1], :]
    back = lambda j, k: seg(j).reshape(DEPTH, D_MODEL // N_DEV, k).transpose(0, 2, 1)
    grads["w_in"] = seg(0).transpose(0, 2, 1)
    grads["w_up"] = seg(1).transpose(0, 2, 1)
    grads["w_gate"] = seg(2).transpose(0, 2, 1)
    grads["w_out"] = seg(3)
    grads["w_down"] = seg(4)
    grads["w_proj_a"] = back(5, 256)
    grads["w_proj_b"] = back(6, 512)
    grads["w_proj_c"] = back(7, 512)

    order = ["rel_bias", "norm_mix_w", "w_in", "b_gate", "ret_norm_w", "w_proj_a", "w_proj_b", "w_proj_c",
             "w_out", "norm_ffn_w", "w_up", "w_gate", "conv_w", "conv_b", "w_down", "final_norm_w"]
    ws = dict(rel_bias=rel_bias, norm_mix_w=norm_mix_w, w_in=w_in, b_gate=b_gate, ret_norm_w=ret_norm_w,
              w_proj_a=w_proj_a, w_proj_b=w_proj_b, w_proj_c=w_proj_c, w_out=w_out, norm_ffn_w=norm_ffn_w,
              w_up=w_up, w_gate=w_gate, conv_w=conv_w, conv_b=conv_b, w_down=w_down, final_norm_w=final_norm_w)
    ms = dict(rel_bias=m_rel_bias, norm_mix_w=m_norm_mix_w, w_in=m_w_in, b_gate=m_b_gate, ret_norm_w=m_ret_norm_w,
              w_proj_a=m_w_proj_a, w_proj_b=m_w_proj_b, w_proj_c=m_w_proj_c, w_out=m_w_out,
              norm_ffn_w=m_norm_ffn_w, w_up=m_w_up, w_gate=m_w_gate, conv_w=m_conv_w, conv_b=m_conv_b,
              w_down=m_w_down, final_norm_w=m_final_norm_w)
    vs = dict(rel_bias=v_rel_bias, norm_mix_w=v_norm_mix_w, w_in=v_w_in, b_gate=v_b_gate, ret_norm_w=v_ret_norm_w,
              w_proj_a=v_w_proj_a, w_proj_b=v_w_proj_b, w_proj_c=v_w_proj_c, w_out=v_w_out,
              norm_ffn_w=v_norm_ffn_w, w_up=v_w_up, w_gate=v_w_gate, conv_w=v_conv_w, conv_b=v_conv_b,
              w_down=v_w_down, final_norm_w=v_final_norm_w)
    deltas, new_m, new_v = [], [], []
    for name in order:
        g = grads[name].reshape(ws[name].shape)
        grads[name] = g
        d, nm, nv = _adamw(ws[name], g, ms[name], vs[name], f"adamw_{name}")
        deltas.append(d)
        new_m.append(nm)
        new_v.append(nv)
    return (loss, grad_x, *[grads[n] for n in order], *deltas, *new_m, *new_v)
```

```python
import functools

import numpy as np
import jax
import jax.numpy as jnp
from jax import lax
from jax.experimental import pallas as pl
from jax.experimental.pallas import tpu as pltpu

F32 = jnp.float32
BF16 = jnp.bfloat16

D_MODEL = 1024
DEPTH = 4
N_DEV = 8
BLOCK = 128
DIL_GROUPS = ((128, 1), (512, 4), (2048, 16))
A_HEADS_PER_GROUP = 4
HEAD_DIM = 64
A_GROUP_WIDTH = A_HEADS_PER_GROUP * HEAD_DIM
SB_HEADS = 4
SB_HEAD_DIM = 128
RET_HEADS = 4
RET_DK = 64
RET_DV = 128
ROPE_BASE = 10000.0
REL_BUCKETS = 32
REL_MAX_DIST = 2048
D_FF = 2816
EPS = 1e-6
LO_WIDTH = 3840
HI_WIDTH = 4608
A_SCALE = HEAD_DIM ** -0.5
SB_SCALE = SB_HEAD_DIM ** -0.5
RET_SCALE = RET_DK ** -0.5
NEG = -1e30

ADAM_LR = 0.001
ADAM_B1 = 0.9
ADAM_B2 = 0.999
ADAM_EPS = 1e-08
ADAM_WD = 0.01
ADAM_STEP = 10

VMEM_LIMIT_V7X = 56 * 1024 * 1024

_NT = (((1,), (1,)), ((), ()))
_NN = (((1,), (0,)), ((), ()))
_TN = (((0,), (0,)), ((), ()))


def _dot(a, b, dims):
    return lax.dot_general(a, b, dims, preferred_element_type=F32)


def _tile(n, cands):
    for c in cands:
        if n % c == 0:
            return c
    return n


def _params(*sem):
    return pltpu.CompilerParams(dimension_semantics=sem, vmem_limit_bytes=VMEM_LIMIT_V7X)


def _matmul(a, b, mode, name, out_dtype=None, add=None):
    if out_dtype is None:
        out_dtype = BF16 if mode == "tn" else F32
    if mode == "tn":
        (K, M), N = a.shape, b.shape[1]
    elif mode == "nn":
        (M, K), N = a.shape, b.shape[1]
    else:
        (M, K), N = a.shape, b.shape[0]
    wide = (1536, 1408, 1280, 1152, 1024, 768, 512, 384, 256, 128)
    if mode == "tn":
        tm, tn, tk = _tile(M, wide), _tile(N, wide[4:]), _tile(K, (1024, 512, 256, 128))
    else:
        tm, tn, tk = _tile(M, (512, 256, 128)), _tile(N, wide), K
    nk = K // tk
    if mode == "tn":
        a_spec = pl.BlockSpec((tk, tm), lambda i, j, k: (k, i))
    else:
        a_spec = pl.BlockSpec((tm, tk), lambda i, j, k: (i, k))
    if mode == "nt":
        b_spec = pl.BlockSpec((tn, tk), lambda i, j, k: (j, k))
    else:
        b_spec = pl.BlockSpec((tk, tn), lambda i, j, k: (k, j))
    dims = {"nt": _NT, "nn": _NN, "tn": _TN}[mode]
    o_spec = pl.BlockSpec((tm, tn), lambda i, j, k: (i, j))
    has_add = add is not None

    def finish(refs, r):
        if has_add:
            r = r + refs[2][...]
        refs[-1 if nk == 1 else -2][...] = r.astype(out_dtype)

    def body(*refs):
        part = _dot(refs[0][...].astype(BF16), refs[1][...].astype(BF16), dims)
        if nk == 1:
            finish(refs, part)
            return
        acc = refs[-1]
        k = pl.program_id(2)

        @pl.when(k == 0)
        def _():
            acc[...] = part

        @pl.when(k > 0)
        def _():
            acc[...] += part

        @pl.when(k == nk - 1)
        def _():
            finish(refs, acc[...])

    ins = [a, b] + ([add] if has_add else [])
    specs = [a_spec, b_spec] + ([o_spec] if has_add else [])
    return pl.pallas_call(
        body, name=name, grid=(M // tm, N // tn, nk),
        in_specs=specs, out_specs=o_spec,
        out_shape=jax.ShapeDtypeStruct((M, N), out_dtype),
        scratch_shapes=[] if nk == 1 else [pltpu.VMEM((tm, tn), F32)],
        compiler_params=_params("parallel", "parallel", "arbitrary"),
    )(*ins)


def _rms(x, w):
    return x * lax.rsqrt(jnp.mean(x * x, axis=-1, keepdims=True) + EPS) * w


def _rms_fwd(x, w, name):
    T = x.shape[0]
    tm = _tile(T, (512, 256, 128))

    def body(x_ref, w_ref, o_ref):
        o_ref[...] = _rms(x_ref[...], w_ref[...]).astype(BF16)

    return pl.pallas_call(
        body, name=name, grid=(T // tm,),
        in_specs=[pl.BlockSpec((tm, D_MODEL), lambda i: (i, 0)), pl.BlockSpec((1, D_MODEL), lambda i: (0, 0))],
        out_specs=pl.BlockSpec((tm, D_MODEL), lambda i: (i, 0)),
        out_shape=jax.ShapeDtypeStruct((T, D_MODEL), BF16),
        compiler_params=_params("parallel"),
    )(x, w.reshape(1, D_MODEL))


def _rms_bwd(x, w, dh, res, name):
    T = x.shape[0]
    tm = _tile(T, (512, 256, 128))

    def body(x_ref, w_ref, dh_ref, res_ref, dx_ref, dw_ref):
        _, vjp = jax.vjp(_rms, x_ref[...], w_ref[...])
        dx, dw = vjp(dh_ref[...])
        dx_ref[...] = dx + res_ref[...]

        @pl.when(pl.program_id(0) == 0)
        def _():
            dw_ref[...] = jnp.zeros_like(dw_ref)

        dw_ref[...] += dw

    row = pl.BlockSpec((tm, D_MODEL), lambda i: (i, 0))
    vec = pl.BlockSpec((1, D_MODEL), lambda i: (0, 0))
    return pl.pallas_call(
        body, name=name, grid=(T // tm,),
        in_specs=[row, vec, row, row], out_specs=[row, vec],
        out_shape=[jax.ShapeDtypeStruct((T, D_MODEL), F32), jax.ShapeDtypeStruct((1, D_MODEL), F32)],
        compiler_params=_params("arbitrary"),
    )(x, w.reshape(1, D_MODEL), dh, res)


def _loss_head(x, w, tgt, name):
    T = x.shape[0]
    tm = _tile(T, (512, 256, 128))

    def body(x_ref, w_ref, t_ref, l_ref, dx_ref, dw_ref):
        y, vjp = jax.vjp(_rms, x_ref[...], w_ref[...])
        e = y - t_ref[...]
        dx, dw = vjp(e * (1.0 / D_MODEL))
        dx_ref[...] = dx

        @pl.when(pl.program_id(0) == 0)
        def _():
            dw_ref[...] = jnp.zeros_like(dw_ref)
            l_ref[...] = jnp.zeros_like(l_ref)

        dw_ref[...] += dw
        l_ref[...] += 0.5 * jnp.sum(jnp.mean(e * e, axis=-1, keepdims=True))

    row = pl.BlockSpec((tm, D_MODEL), lambda i: (i, 0))
    vec = pl.BlockSpec((1, D_MODEL), lambda i: (0, 0))
    return pl.pallas_call(
        body, name=name, grid=(T // tm,),
        in_specs=[row, vec, row],
        out_specs=[pl.BlockSpec((8, 128), lambda i: (0, 0)), row, vec],
        out_shape=[jax.ShapeDtypeStruct((8, 128), F32), jax.ShapeDtypeStruct((T, D_MODEL), F32),
                   jax.ShapeDtypeStruct((1, D_MODEL), F32)],
        compiler_params=_params("arbitrary"),
    )(x, w.reshape(1, D_MODEL), tgt)


def _t5_bucket(dist):
    max_exact = REL_BUCKETS // 2
    n = np.maximum(dist, 0)
    large = max_exact + (np.log(np.maximum(n, 1) / max_exact) / np.log(REL_MAX_DIST / max_exact)
                         * (REL_BUCKETS - max_exact)).astype(np.int64)
    large = np.minimum(large, REL_BUCKETS - 1)
    return np.where(n < max_exact, n, large).astype(np.int32)


def _a_buckets():
    steps = np.arange(BLOCK)[:, None] + BLOCK - np.arange(2 * BLOCK)[None, :]
    out = []
    for window, dilation in DIL_GROUPS:
        in_band = (steps >= 0) & (steps <= window // dilation)
        out.append(np.where(in_band, _t5_bucket(steps * dilation), -1))
    return jnp.asarray(np.stack(out).astype(np.int32))


def _a_bias_tables(rel_bias, buckets):
    def body(rb_ref, bk_ref, o_ref):
        hh = pl.program_id(0)
        bk = bk_ref[0]
        acc = jnp.full((BLOCK, 2 * BLOCK), NEG, F32)
        for b in range(REL_BUCKETS):
            acc = jnp.where(bk == b, rb_ref[b, hh], acc)
        o_ref[0] = acc

    return pl.pallas_call(
        body, name="a_bias_tables", grid=(12,),
        in_specs=[pl.BlockSpec(memory_space=pltpu.SMEM),
                  pl.BlockSpec((1, BLOCK, 2 * BLOCK), lambda h: (h // 4, 0, 0))],
        out_specs=pl.BlockSpec((1, BLOCK, 2 * BLOCK), lambda h: (h, 0, 0)),
        out_shape=jax.ShapeDtypeStruct((12, BLOCK, 2 * BLOCK), F32),
        compiler_params=_params("parallel"),
    )(rel_bias, buckets)


def _a_bias_grad(dbias, buckets):
    def body(db_ref, bk_ref, o_ref):
        bk = bk_ref[0]
        db = db_ref[0]
        lane = lax.broadcasted_iota(jnp.int32, (8, 128), 1)
        acc = jnp.zeros((8, 128), F32)
        for b in range(REL_BUCKETS):
            acc = jnp.where(lane == b, jnp.sum(jnp.where(bk == b, db, 0.0)), acc)
        o_ref[0] = acc

    out = pl.pallas_call(
        body, name="a_bias_grad", grid=(12,),
        in_specs=[pl.BlockSpec((1, BLOCK, 2 * BLOCK), lambda h: (h, 0, 0)),
                  pl.BlockSpec((1, BLOCK, 2 * BLOCK), lambda h: (h // 4, 0, 0))],
        out_specs=pl.BlockSpec((1, 8, 128), lambda h: (h, 0, 0)),
        out_shape=jax.ShapeDtypeStruct((12, 8, 128), F32),
        compiler_params=_params("parallel"),
    )(dbias, buckets)
    return out[:, 0, :REL_BUCKETS].T


def _a_logits(q_ref, kp_ref, kc_ref, b_ref, h, ok):
    sl = slice(h * HEAD_DIM, (h + 1) * HEAD_DIM)
    qh = q_ref[:, sl]
    s = jnp.concatenate([_dot(qh, kp_ref[:, sl], _NT), _dot(qh, kc_ref[:, sl], _NT)], axis=1)
    s = s * A_SCALE + b_ref[h]
    return jnp.where(ok, s, NEG)


def _a_attn_fwd(proj_lo, bias_all, g, name):
    T = proj_lo.shape[0]
    d = DIL_GROUPS[g][1]
    L = T // d
    nb = L // BLOCK
    nlo = LO_WIDTH // A_GROUP_WIDTH
    pv = proj_lo.reshape(L, d * LO_WIDTH)

    def body(q_ref, kc_ref, kp_ref, vc_ref, vp_ref, b_ref, o_ref, l_ref):
        n = pl.program_id(1)
        col = lax.broadcasted_iota(jnp.int32, (BLOCK, 2 * BLOCK), 1)
        ok = col >= jnp.where(n > 0, 0, BLOCK)
        for h in range(A_HEADS_PER_GROUP):
            sl = slice(h * HEAD_DIM, (h + 1) * HEAD_DIM)
            s = _a_logits(q_ref, kp_ref, kc_ref, b_ref, h, ok)
            m = jnp.max(s, axis=-1, keepdims=True)
            p = jnp.exp(s - m)
            l = jnp.sum(p, axis=-1, keepdims=True)
            pb = p.astype(BF16)
            o = _dot(pb[:, :BLOCK], vp_ref[:, sl], _NN) + _dot(pb[:, BLOCK:], vc_ref[:, sl], _NN)
            o_ref[:, sl] = o / l
            l_ref[:, sl] = jnp.broadcast_to(m + jnp.log(l), (BLOCK, HEAD_DIM))

    blk = (BLOCK, A_GROUP_WIDTH)
    out_spec = pl.BlockSpec(blk, lambda r, n: (n, r))
    out, lse = pl.pallas_call(
        body, name=name, grid=(d, nb),
        in_specs=[pl.BlockSpec(blk, lambda r, n: (n, r * nlo + g)),
                  pl.BlockSpec(blk, lambda r, n: (n, r * nlo + 3 + g)),
                  pl.BlockSpec(blk, lambda r, n: (jnp.maximum(n - 1, 0), r * nlo + 3 + g)),
                  pl.BlockSpec(blk, lambda r, n: (n, r * nlo + 6 + g)),
                  pl.BlockSpec(blk, lambda r, n: (jnp.maximum(n - 1, 0), r * nlo + 6 + g)),
                  pl.BlockSpec((4, BLOCK, 2 * BLOCK), lambda r, n: (g, 0, 0))],
        out_specs=[out_spec, out_spec],
        out_shape=[jax.ShapeDtypeStruct((L, d * A_GROUP_WIDTH), F32)] * 2,
        compiler_params=_params("parallel", "arbitrary"),
    )(pv, pv, pv, pv, pv, bias_all)
    return out.reshape(T, A_GROUP_WIDTH), lse.reshape(T, A_GROUP_WIDTH)


def _a_attn_bwd(proj_lo, bias_all, out, lse, dout, dlse, g, name):
    T = proj_lo.shape[0]
    d = DIL_GROUPS[g][1]
    L = T // d
    nb = L // BLOCK
    nlo = LO_WIDTH // A_GROUP_WIDTH
    pv = proj_lo.reshape(L, d * LO_WIDTH)
    view = lambda t: t.reshape(L, d * A_GROUP_WIDTH)

    def body(q_ref, kc_ref, kp_ref, vc_ref, vp_ref, b_ref, o_ref, l_ref, do_ref, dl_ref,
             dq_ref, dk_ref, dv_ref, db_ref, ck, cv):
        r = pl.program_id(0)
        n = pl.program_id(1)

        @pl.when((r == 0) & (n == 0))
        def _():
            db_ref[...] = jnp.zeros_like(db_ref)

        @pl.when(n == 0)
        def _():
            ck[...] = jnp.zeros_like(ck)
            cv[...] = jnp.zeros_like(cv)

        @pl.when(n < nb)
        def _():
            col = lax.broadcasted_iota(jnp.int32, (BLOCK, 2 * BLOCK), 1)
            ok = col >= jnp.where(n > 0, 0, BLOCK)
            for h in range(A_HEADS_PER_GROUP):
                sl = slice(h * HEAD_DIM, (h + 1) * HEAD_DIM)
                s = _a_logits(q_ref, kp_ref, kc_ref, b_ref, h, ok)
                p = jnp.exp(s - l_ref[:, h * HEAD_DIM:h * HEAD_DIM + 1])
                do = do_ref[:, sl]
                dob = do.astype(BF16)
                delta = jnp.sum(do * o_ref[:, sl], axis=-1, keepdims=True)
                dl = jnp.sum(dl_ref[:, sl], axis=-1, keepdims=True)
                dp = jnp.concatenate([_dot(dob, vp_ref[:, sl], _NT), _dot(dob, vc_ref[:, sl], _NT)], axis=1)
                ds = p * (dp - delta + dl)
                db_ref[h] += ds
                dsb = (ds * A_SCALE).astype(BF16)
                pb = p.astype(BF16)
                qh = q_ref[:, sl]
                dq_ref[:, sl] = (_dot(dsb[:, :BLOCK], kp_ref[:, sl], _NN)
                                 + _dot(dsb[:, BLOCK:], kc_ref[:, sl], _NN)).astype(BF16)
                dk_ref[:, sl] = (ck[:, sl] + _dot(dsb[:, :BLOCK], qh, _TN)).astype(BF16)
                dv_ref[:, sl] = (cv[:, sl] + _dot(pb[:, :BLOCK], dob, _TN)).astype(BF16)
                ck[:, sl] = _dot(dsb[:, BLOCK:], qh, _TN)
                cv[:, sl] = _dot(pb[:, BLOCK:], dob, _TN)

        @pl.when(n == nb)
        def _():
            dk_ref[...] = ck[...].astype(BF16)
            dv_ref[...] = cv[...].astype(BF16)

    blk = (BLOCK, A_GROUP_WIDTH)
    nq = lambda n: jnp.minimum(n, nb - 1)
    prv = lambda n: jnp.maximum(jnp.minimum(n, nb - 1) - 1, 0)
    cur_o = pl.BlockSpec(blk, lambda r, n: (nq(n), r))
    lag_o = pl.BlockSpec(blk, lambda r, n: (jnp.maximum(n - 1, 0), r))
    dq, dk, dv, db = pl.pallas_call(
        body, name=name, grid=(d, nb + 1),
        in_specs=[pl.BlockSpec(blk, lambda r, n: (nq(n), r * nlo + g)),
                  pl.BlockSpec(blk, lambda r, n: (nq(n), r * nlo + 3 + g)),
                  pl.BlockSpec(blk, lambda r, n: (prv(n), r * nlo + 3 + g)),
                  pl.BlockSpec(blk, lambda r, n: (nq(n), r * nlo + 6 + g)),
                  pl.BlockSpec(blk, lambda r, n: (prv(n), r * nlo + 6 + g)),
                  pl.BlockSpec((4, BLOCK, 2 * BLOCK), lambda r, n: (g, 0, 0)),
                  cur_o, cur_o, cur_o, cur_o],
        out_specs=[cur_o, lag_o, lag_o, pl.BlockSpec((4, BLOCK, 2 * BLOCK), lambda r, n: (0, 0, 0))],
        out_shape=[jax.ShapeDtypeStruct((L, d * A_GROUP_WIDTH), BF16)] * 3
        + [jax.ShapeDtypeStruct((4, BLOCK, 2 * BLOCK), F32)],
        scratch_shapes=[pltpu.VMEM(blk, F32), pltpu.VMEM(blk, F32)],
        compiler_params=_params("arbitrary", "arbitrary"),
    )(pv, pv, pv, pv, pv, bias_all, view(out), view(lse), view(dout), view(dlse))
    return dq.reshape(T, -1), dk.reshape(T, -1), dv.reshape(T, -1), db


def _a_merge(outs, lses):
    m = jnp.maximum(jnp.maximum(lses[0], lses[1]), lses[2])
    e = [jnp.exp(l - m) for l in lses]
    inv = 1.0 / (e[0] + e[1] + e[2])
    return (e[0] * outs[0] + e[1] * outs[1] + e[2] * outs[2]) * inv


def _a_merge_fwd(outs, lses, name):
    T = outs[0].shape[0]
    tm = _tile(T, (512, 256, 128))

    def body(o0, o1, o2, l0, l1, l2, y_ref):
        y_ref[...] = _a_merge([o0[...], o1[...], o2[...]], [l0[...], l1[...], l2[...]]).astype(BF16)

    row = pl.BlockSpec((tm, A_GROUP_WIDTH), lambda i: (i, 0))
    return pl.pallas_call(
        body, name=name, grid=(T // tm,), in_specs=[row] * 6, out_specs=row,
        out_shape=jax.ShapeDtypeStruct((T, A_GROUP_WIDTH), BF16),
        compiler_params=_params("parallel"),
    )(*outs, *lses)


def _a_merge_bwd(outs, lses, dy, name):
    T = outs[0].shape[0]
    tm = _tile(T, (512, 256, 128))

    def body(o0, o1, o2, l0, l1, l2, dy_ref, *outs_ref):
        _, vjp = jax.vjp(_a_merge, [o0[...], o1[...], o2[...]], [l0[...], l1[...], l2[...]])
        do, dl = vjp(dy_ref[...])
        for ref, val in zip(outs_ref, list(do) + list(dl)):
            ref[...] = val

    row = pl.BlockSpec((tm, A_GROUP_WIDTH), lambda i: (i, 0))
    res = pl.pallas_call(
        body, name=name, grid=(T // tm,), in_specs=[row] * 7, out_specs=[row] * 6,
        out_shape=[jax.ShapeDtypeStruct((T, A_GROUP_WIDTH), F32)] * 6,
        compiler_params=_params("parallel"),
    )(*outs, *lses, dy)
    return res[:3], res[3:]


SB_TQ = 512
SB_KS = 256
SB_UNROLL = 4
SB_Q_COL, SB_K_COL, SB_V_COL = 18, 22, 26


def _sb_tri():
    return jnp.asarray(np.arange(SB_KS)[:, None] > np.arange(SB_KS)[None, :], BF16)


def _logit_parts(m):
    z = m * SB_SCALE
    e = jnp.exp2(jnp.abs(m) * (-SB_SCALE * 1.4426950408889634))
    return z, jnp.maximum(z, 0.0) + jnp.log(1.0 + e)


def _sb_sizes(T):
    nsub = SB_TQ // SB_KS
    assert T % SB_TQ == 0 and SB_TQ % SB_KS == 0 and SB_UNROLL in (nsub, 2 * nsub)
    return SB_TQ, SB_KS, nsub, SB_UNROLL


def _wide(c, width):
    return jnp.concatenate([c] * (width // BLOCK), axis=1)


def _sb_fwd(proj_lo, name):
    T = proj_lo.shape[0]
    TQ, KS, nsub, unroll = _sb_sizes(T)
    nq = T // TQ

    def body(q_ref, k_ref, v_ref, tri_ref, o_ref, c_ref, acc, cs):
        i = pl.program_id(1)
        q = q_ref[...]
        tri = tri_ref[...]
        row = i * TQ + lax.broadcasted_iota(jnp.int32, (TQ, KS), 0)
        col = lax.broadcasted_iota(jnp.int32, (TQ, KS), 1)

        def sub(n, masked, c, o):
            off = pl.multiple_of(n * KS, KS)
            z, nl = _logit_parts(_dot(q, k_ref[pl.ds(off, KS), :], _NT))
            if masked:
                valid = row > n * KS + col
                nl = jnp.where(valid, nl, 0.0)
            w = jnp.exp(z - nl - _dot(nl.astype(BF16), tri, _NN) - _wide(c, KS))
            if masked:
                w = jnp.where(valid, w, 0.0)
            o = o + _dot(w.astype(BF16), v_ref[pl.ds(off, KS), :], _NN)
            return c + jnp.sum(nl, axis=1, keepdims=True), o

        c = jnp.zeros((TQ, BLOCK), F32)
        o = jnp.zeros((TQ, BLOCK), F32)
        for j in reversed(range(nsub)):
            c, o = sub(i * nsub + j, True, c, o)
        cs[...] = c
        acc[...] = o

        def group(top, count):
            c = cs[...]
            o = jnp.zeros((TQ, BLOCK), F32)
            for u in range(count):
                c, o = sub(top - u, False, c, o)
            cs[...] = c
            acc[...] += o

        below = i * nsub
        odd = below % unroll

        @pl.when(odd != 0)
        def _():
            group(below - 1, nsub)

        def step(m, carry):
            group(below - odd - 1 - m * unroll, unroll)
            return carry

        lax.fori_loop(0, below // unroll, step, 0)
        o_ref[...] = acc[...]
        c_ref[...] = cs[...]

    qo = pl.BlockSpec((TQ, BLOCK), lambda h, i: (i, h))
    return pl.pallas_call(
        body, name=name, grid=(SB_HEADS, nq),
        in_specs=[pl.BlockSpec((TQ, BLOCK), lambda h, i: (i, SB_Q_COL + h)),
                  pl.BlockSpec((T, BLOCK), lambda h, i: (0, SB_K_COL + h)),
                  pl.BlockSpec((T, BLOCK), lambda h, i: (0, SB_V_COL + h)),
                  pl.BlockSpec((SB_KS, SB_KS), lambda h, i: (0, 0))],
        out_specs=[qo, qo],
        out_shape=[jax.ShapeDtypeStruct((T, SB_HEADS * BLOCK), F32)] * 2,
        scratch_shapes=[pltpu.VMEM((TQ, BLOCK), F32), pltpu.VMEM((TQ, BLOCK), F32)],
        compiler_params=_params("parallel", "arbitrary"),
    )(proj_lo, proj_lo, proj_lo, _sb_tri())


def _sb_bwd(proj_lo, ctot, dy, name):
    T = proj_lo.shape[0]
    TQ, KS, nsub, unroll = _sb_sizes(T)
    nq = T // TQ

    def body(q_ref, k_ref, v_ref, c_ref, do_ref, tri_ref, dq_ref, dk_hbm, dv_hbm,
             dka, dva, dqa, cps, pgs, sem):
        h = pl.program_id(0)
        i = pl.program_id(1)

        @pl.when(i == 0)
        def _():
            dka[...] = jnp.zeros_like(dka)
            dva[...] = jnp.zeros_like(dva)

        q = q_ref[...]
        tri = tri_ref[...]
        dob = do_ref[...].astype(BF16)
        ctot = c_ref[...]
        row = i * TQ + lax.broadcasted_iota(jnp.int32, (TQ, KS), 0)
        col = lax.broadcasted_iota(jnp.int32, (TQ, KS), 1)

        def sub(n, masked, cp, pg, dq):
            off = pl.multiple_of(n * KS, KS)
            kb = k_ref[pl.ds(off, KS), :]
            z, nl = _logit_parts(_dot(q, kb, _NT))
            if masked:
                valid = row > n * KS + col
                nl = jnp.where(valid, nl, 0.0)
            cp = cp + jnp.sum(nl, axis=1, keepdims=True)
            lb = z - nl
            w = jnp.exp(lb - _dot(nl.astype(BF16), tri, _NN) - _wide(ctot - cp, KS))
            if masked:
                w = jnp.where(valid, w, 0.0)
            g = w * _dot(dob, v_ref[pl.ds(off, KS), :], _NT)
            dz = g - jnp.exp(lb) * (g + _dot(g.astype(BF16), tri, _NT) + _wide(pg, KS))
            if masked:
                dz = jnp.where(valid, dz, 0.0)
            pg = pg + jnp.sum(g, axis=1, keepdims=True)
            dzb = dz.astype(BF16)
            dka[pl.ds(off, KS), :] += _dot(dzb, q, _TN)
            dva[pl.ds(off, KS), :] += _dot(w.astype(BF16), dob, _TN)
            return cp, pg, dq + _dot(dzb, kb, _NN)

        zero = jnp.zeros((TQ, BLOCK), F32)
        cps[...] = zero
        pgs[...] = zero
        dqa[...] = zero

        def group(first, count):
            cp, pg, dq = cps[...], pgs[...], zero
            for u in range(count):
                cp, pg, dq = sub(first + u, False, cp, pg, dq)
            cps[...] = cp
            pgs[...] = pg
            dqa[...] += dq

        below = i * nsub
        odd = below % unroll

        def step(m, carry):
            group(m * unroll, unroll)
            return carry

        lax.fori_loop(0, below // unroll, step, 0)

        @pl.when(odd != 0)
        def _():
            group(below - nsub, nsub)

        cp, pg, dq = cps[...], pgs[...], dqa[...]
        for j in range(nsub):
            cp, pg, dq = sub(i * nsub + j, True, cp, pg, dq)
        dq_ref[...] = (dq * SB_SCALE).astype(BF16)

        @pl.when(i == nq - 1)
        def _():
            def scale_rows(r, carry):
                rows = pl.ds(pl.multiple_of(r * TQ, TQ), TQ)
                dka[rows, :] = dka[rows, :] * SB_SCALE
                return carry

            lax.fori_loop(0, nq, scale_rows, 0)
            ck = pltpu.make_async_copy(dka, dk_hbm.at[h], sem.at[0])
            cv = pltpu.make_async_copy(dva, dv_hbm.at[h], sem.at[1])
            ck.start()
            cv.start()
            ck.wait()
            cv.wait()

    qo = pl.BlockSpec((TQ, BLOCK), lambda h, i: (i, h))
    return pl.pallas_call(
        body, name=name, grid=(SB_HEADS, nq),
        in_specs=[pl.BlockSpec((TQ, BLOCK), lambda h, i: (i, SB_Q_COL + h)),
                  pl.BlockSpec((T, BLOCK), lambda h, i: (0, SB_K_COL + h)),
                  pl.BlockSpec((T, BLOCK), lambda h, i: (0, SB_V_COL + h)),
                  qo, qo,
                  pl.BlockSpec((SB_KS, SB_KS), lambda h, i: (0, 0))],
        out_specs=[qo, pl.BlockSpec(memory_space=pl.ANY), pl.BlockSpec(memory_space=pl.ANY)],
        out_shape=[jax.ShapeDtypeStruct((T, SB_HEADS * BLOCK), BF16),
                   jax.ShapeDtypeStruct((SB_HEADS, T, BLOCK), F32),
                   jax.ShapeDtypeStruct((SB_HEADS, T, BLOCK), F32)],
        scratch_shapes=[pltpu.VMEM((T, BLOCK), F32), pltpu.VMEM((T, BLOCK), F32),
                        pltpu.VMEM((TQ, BLOCK), F32), pltpu.VMEM((TQ, BLOCK), F32),
                        pltpu.VMEM((TQ, BLOCK), F32), pltpu.SemaphoreType.DMA((2,))],
        compiler_params=_params("arbitrary", "arbitrary"),
    )(proj_lo, proj_lo, proj_lo, ctot, dy, _sb_tri())


def _ret_tables(T):
    half = RET_DK // 2
    inv = ROPE_BASE ** (-jnp.arange(half, dtype=F32) / half)
    ang = jnp.arange(T, dtype=jnp.int32).astype(F32)[:, None] * inv[None, :]
    cos, sin = jnp.cos(ang), jnp.sin(ang)
    cos_t = jnp.tile(cos, (1, 2 * RET_HEADS))
    sin_t = jnp.tile(jnp.concatenate([-sin, sin], axis=1), (1, RET_HEADS))
    lg = np.log1p(-np.exp2(-5.0 - np.arange(RET_HEADS)))
    n = np.arange(BLOCK)
    diff = n[:, None] - n[None, :]
    dmat = np.where(diff >= 0, np.exp(np.minimum(diff, BLOCK)[None] * lg[:, None, None]), 0.0)
    zeta = np.repeat(np.exp((BLOCK - 1 - n)[:, None] * lg[None, :]), RET_DK, axis=1)
    xi = np.repeat(np.exp((n + 1)[:, None] * lg[None, :]), RET_DK, axis=1)
    chunk_decay = [float(v) for v in np.exp(BLOCK * lg)]
    return (cos_t, sin_t, jnp.asarray(dmat, F32), jnp.asarray(xi, F32), jnp.asarray(zeta, F32)), chunk_decay


def _swap_halves(x):
    lane = lax.broadcasted_iota(jnp.int32, x.shape, 1)
    lower = (lane % RET_DK) < (RET_DK // 2)
    w = x.shape[1]
    return jnp.where(lower, pltpu.roll(x, w - RET_DK // 2, 1), pltpu.roll(x, RET_DK // 2, 1))


def _gn_gate(o, rg, w):
    mu = jnp.mean(o, axis=-1, keepdims=True)
    xc = o - mu
    var = jnp.mean(xc * xc, axis=-1, keepdims=True)
    return (rg * jax.nn.sigmoid(rg)) * (xc * lax.rsqrt(var + EPS) * w)


def _ret_fwd(proj_hi, ret_norm_w, tables, chunk_decay, name):
    T = proj_hi.shape[0]
    nc = T // BLOCK
    cos_t, sin_t, dmat, xi, zeta = tables

    def body(rq_ref, rk_ref, rv_ref, rg_ref, w_ref, cos_ref, sin_ref, d_ref, xi_ref, ze_ref,
             y_ref, o_ref, st_ref, rs):
        @pl.when(pl.program_id(0) == 0)
        def _():
            rs[...] = jnp.zeros_like(rs)

        cos, sin = cos_ref[...], sin_ref[...]
        rq, rk = rq_ref[...], rk_ref[...]
        q = rq * cos + _swap_halves(rq) * sin
        k = (rk * cos + _swap_halves(rk) * sin) * RET_SCALE
        qb, kb = q.astype(BF16), k.astype(BF16)
        qx, kz = (q * xi_ref[...]).astype(BF16), (k * ze_ref[...]).astype(BF16)
        for h in range(RET_HEADS):
            sl = slice(h * RET_DK, (h + 1) * RET_DK)
            sv = slice(h * RET_DV, (h + 1) * RET_DV)
            vb = rv_ref[:, sv].astype(BF16)
            r = rs[h]
            st_ref[0, h] = r
            intra = _dot(qb[:, sl], kb[:, sl], _NT) * d_ref[h]
            o = _dot(intra.astype(BF16), vb, _NN) + _dot(qx[:, sl], r.astype(BF16), _NN)
            rs[h] = r * chunk_decay[h] + _dot(kz[:, sl], vb, _TN)
            o_ref[:, sv] = o
            y_ref[:, sv] = _gn_gate(o, rg_ref[:, sv], w_ref[:, sv]).astype(BF16)

    qk = (BLOCK, RET_HEADS * RET_DK)
    vv = (BLOCK, RET_HEADS * RET_DV)
    const = lambda shape: pl.BlockSpec(shape, lambda n: (0,) * len(shape))
    return pl.pallas_call(
        body, name=name, grid=(nc,),
        in_specs=[pl.BlockSpec(qk, lambda n: (n, 0)), pl.BlockSpec(qk, lambda n: (n, 1)),
                  pl.BlockSpec(vv, lambda n: (n, 1)), pl.BlockSpec(vv, lambda n: (n, 2)),
                  const((1, RET_HEADS * RET_DV)),
                  pl.BlockSpec(qk, lambda n: (n, 0)), pl.BlockSpec(qk, lambda n: (n, 0)),
                  const((RET_HEADS, BLOCK, BLOCK)), const(qk), const(qk)],
        out_specs=[pl.BlockSpec(vv, lambda n: (n, 0)), pl.BlockSpec(vv, lambda n: (n, 0)),
                   pl.BlockSpec((1, RET_HEADS, RET_DK, RET_DV), lambda n: (n, 0, 0, 0))],
        out_shape=[jax.ShapeDtypeStruct((T, RET_HEADS * RET_DV), BF16),
                   jax.ShapeDtypeStruct((T, RET_HEADS * RET_DV), F32),
                   jax.ShapeDtypeStruct((nc, RET_HEADS, RET_DK, RET_DV), F32)],
        scratch_shapes=[pltpu.VMEM((RET_HEADS, RET_DK, RET_DV), F32)],
        compiler_params=_params("arbitrary"),
    )(proj_hi, proj_hi, proj_hi, proj_hi, ret_norm_w.reshape(1, -1), cos_t, sin_t, dmat, xi, zeta)


def _ret_bwd(proj_hi, ret_norm_w, o_r, states, dy, tables, chunk_decay, name):
    T = proj_hi.shape[0]
    nc = T // BLOCK
    cos_t, sin_t, dmat, xi, zeta = tables

    def body(rq_ref, rk_ref, rv_ref, rg_ref, w_ref, cos_ref, sin_ref, d_ref, xi_ref, ze_ref,
             o_ref, st_ref, dy_ref, dq_ref, dk_ref, dv_ref, dg_ref, dw_ref, drs, dqs, dks):
        @pl.when(pl.program_id(0) == 0)
        def _():
            drs[...] = jnp.zeros_like(drs)
            dw_ref[...] = jnp.zeros_like(dw_ref)

        cos, sin = cos_ref[...], sin_ref[...]
        rq, rk = rq_ref[...], rk_ref[...]
        q = rq * cos + _swap_halves(rq) * sin
        k = (rk * cos + _swap_halves(rk) * sin) * RET_SCALE
        qb, kb = q.astype(BF16), k.astype(BF16)
        qx, kz = (q * xi_ref[...]).astype(BF16), (k * ze_ref[...]).astype(BF16)
        for h in range(RET_HEADS):
            sl = slice(h * RET_DK, (h + 1) * RET_DK)
            sv = slice(h * RET_DV, (h + 1) * RET_DV)
            _, vjp = jax.vjp(_gn_gate, o_ref[:, sv], rg_ref[:, sv], w_ref[:, sv])
            do, dg, dw = vjp(dy_ref[:, sv])
            dg_ref[:, sv] = dg.astype(BF16)
            dw_ref[:, sv] += dw
            dob = do.astype(BF16)
            vb = rv_ref[:, sv].astype(BF16)
            rb = st_ref[0, h].astype(BF16)
            dr = drs[h]
            drb = dr.astype(BF16)
            dmat_h = d_ref[h]
            a = (_dot(dob, vb, _NT) * dmat_h).astype(BF16)
            p = (_dot(qb[:, sl], kb[:, sl], _NT) * dmat_h).astype(BF16)
            dqs[:, sl] = _dot(a, kb[:, sl], _NN) + _dot(dob, rb, _NT) * xi_ref[:, sl]
            dks[:, sl] = _dot(a, qb[:, sl], _TN) + _dot(vb, drb, _NT) * ze_ref[:, sl]
            dv_ref[:, sv] = (_dot(p, dob, _TN) + _dot(kz[:, sl], drb, _NN)).astype(BF16)
            drs[h] = dr * chunk_decay[h] + _dot(qx[:, sl], dob, _TN)
        dq = dqs[...]
        dk = dks[...] * RET_SCALE
        dq_ref[...] = (dq * cos + _swap_halves(dq * sin)).astype(BF16)
        dk_ref[...] = (dk * cos + _swap_halves(dk * sin)).astype(BF16)

    qk = (BLOCK, RET_HEADS * RET_DK)
    vv = (BLOCK, RET_HEADS * RET_DV)
    rev = lambda n: nc - 1 - n
    const = lambda shape: pl.BlockSpec(shape, lambda n: (0,) * len(shape))
    return pl.pallas_call(
        body, name=name, grid=(nc,),
        in_specs=[pl.BlockSpec(qk, lambda n: (rev(n), 0)), pl.BlockSpec(qk, lambda n: (rev(n), 1)),
                  pl.BlockSpec(vv, lambda n: (rev(n), 1)), pl.BlockSpec(vv, lambda n: (rev(n), 2)),
                  const((1, RET_HEADS * RET_DV)),
                  pl.BlockSpec(qk, lambda n: (rev(n), 0)), pl.BlockSpec(qk, lambda n: (rev(n), 0)),
                  const((RET_HEADS, BLOCK, BLOCK)), const(qk), const(qk),
                  pl.BlockSpec(vv, lambda n: (rev(n), 0)),
                  pl.BlockSpec((1, RET_HEADS, RET_DK, RET_DV), lambda n: (rev(n), 0, 0, 0)),
                  pl.BlockSpec(vv, lambda n: (rev(n), 0))],
        out_specs=[pl.BlockSpec(qk, lambda n: (rev(n), 0)), pl.BlockSpec(qk, lambda n: (rev(n), 0)),
                   pl.BlockSpec(vv, lambda n: (rev(n), 0)), pl.BlockSpec(vv, lambda n: (rev(n), 0)),
                   const((1, RET_HEADS * RET_DV))],
        out_shape=[jax.ShapeDtypeStruct((T, RET_HEADS * RET_DK), BF16)] * 2
        + [jax.ShapeDtypeStruct((T, RET_HEADS * RET_DV), BF16)] * 2
        + [jax.ShapeDtypeStruct((1, RET_HEADS * RET_DV), F32)],
        scratch_shapes=[pltpu.VMEM((RET_HEADS, RET_DK, RET_DV), F32), pltpu.VMEM(qk, F32), pltpu.VMEM(qk, F32)],
        compiler_params=_params("arbitrary"),
    )(proj_hi, proj_hi, proj_hi, proj_hi, ret_norm_w.reshape(1, -1), cos_t, sin_t, dmat, xi, zeta,
      o_r, states, dy)


GATE_BLK = 512
GATE_FIRST_BLK = 3


def _gated(g0, g1, g2, b0, b1, b2, pa, pb, pc):
    return jax.nn.sigmoid(g0 + b0) * pa + jax.nn.sigmoid(g1 + b1) * pb + jax.nn.sigmoid(g2 + b2) * pc


def _gate_specs(tm):
    return [pl.BlockSpec((tm, GATE_BLK), functools.partial(lambda i, c: (i, c), c=GATE_FIRST_BLK + j))
            for j in range(6)]


def _gate_args(g, bg_ref):
    gi = [jnp.concatenate([g[2 * j][...], g[2 * j + 1][...]], axis=1) for j in range(3)]
    return gi + [bg_ref[j:j + 1, :] for j in range(3)]


def _merge_fwd(proj_hi, b_gate, pa, pb, pc, name):
    T = proj_hi.shape[0]
    tm = _tile(T, (256, 128))

    def body(g0, g1, g2, g3, g4, g5, bg_ref, pa_ref, pb_ref, pc_ref, o_ref):
        args = _gate_args((g0, g1, g2, g3, g4, g5), bg_ref)
        o_ref[...] = _gated(*args, pa_ref[...], pb_ref[...], pc_ref[...]).astype(BF16)

    row = pl.BlockSpec((tm, D_MODEL), lambda i: (i, 0))
    return pl.pallas_call(
        body, name=name, grid=(T // tm,),
        in_specs=_gate_specs(tm) + [pl.BlockSpec((3, D_MODEL), lambda i: (0, 0)), row, row, row],
        out_specs=row, out_shape=jax.ShapeDtypeStruct((T, D_MODEL), BF16),
        compiler_params=_params("parallel"),
    )(*([proj_hi] * 6), b_gate, pa, pb, pc)


def _merge_bwd(proj_hi, b_gate, pa, pb, pc, dm, name):
    T = proj_hi.shape[0]
    tm = _tile(T, (256, 128))

    def body(g0, g1, g2, g3, g4, g5, bg_ref, pa_ref, pb_ref, pc_ref, dm_ref,
             dgi_ref, dbg_ref, dpa_ref, dpb_ref, dpc_ref):
        args = _gate_args((g0, g1, g2, g3, g4, g5), bg_ref)
        _, vjp = jax.vjp(_gated, *args, pa_ref[...], pb_ref[...], pc_ref[...])
        d = vjp(dm_ref[...])
        for j in range(3):
            dgi_ref[:, j * D_MODEL:(j + 1) * D_MODEL] = d[j].astype(BF16)
        dpa_ref[...] = d[6].astype(BF16)
        dpb_ref[...] = d[7].astype(BF16)
        dpc_ref[...] = d[8].astype(BF16)

        @pl.when(pl.program_id(0) == 0)
        def _():
            dbg_ref[...] = jnp.zeros_like(dbg_ref)

        for j in range(3):
            dbg_ref[j:j + 1, :] += d[3 + j]

    row = pl.BlockSpec((tm, D_MODEL), lambda i: (i, 0))
    vec = pl.BlockSpec((3, D_MODEL), lambda i: (0, 0))
    return pl.pallas_call(
        body, name=name, grid=(T // tm,),
        in_specs=_gate_specs(tm) + [vec, row, row, row, row],
        out_specs=[pl.BlockSpec((tm, 3 * D_MODEL), lambda i: (i, 0)), vec, row, row, row],
        out_shape=[jax.ShapeDtypeStruct((T, 3 * D_MODEL), BF16), jax.ShapeDtypeStruct((3, D_MODEL), F32)]
        + [jax.ShapeDtypeStruct((T, D_MODEL), BF16)] * 3,
        compiler_params=_params("arbitrary"),
    )(*([proj_hi] * 6), b_gate, pa, pb, pc, dm)


FFN_TM = 256


def _gelu(u):
    return 0.5 * u * (1.0 + jnp.tanh(0.7978845608028654 * (u + 0.044715 * (u * u * u))))


def _conv_taps(u0, prev8, first):
    prev8 = jnp.where(first, 0.0, prev8)
    row = lax.broadcasted_iota(jnp.int32, u0.shape, 0)
    s1 = jnp.where(row == 0, prev8[7:8], pltpu.roll(u0, 1, 0))
    s2 = jnp.where(row == 0, prev8[6:7], jnp.where(row == 1, prev8[7:8], pltpu.roll(u0, 2, 0)))
    return s1, s2


def _ffn_specs(T, tm):
    row = pl.BlockSpec((tm, D_FF), lambda i: (i, 0))
    prev = pl.BlockSpec((8, D_FF), lambda i: (jnp.maximum(i * (tm // 8) - 1, 0), 0))
    nxt = pl.BlockSpec((8, D_FF), lambda i: (jnp.minimum((i + 1) * (tm // 8), T // 8 - 1), 0))
    return row, prev, nxt


def _ffn_mid_fwd(u0, gt, cw, cb, name):
    T = u0.shape[0]
    tm = _tile(T, (FFN_TM, 128))
    row, prev, _ = _ffn_specs(T, tm)

    def body(u_ref, p_ref, g_ref, cw_ref, cb_ref, f_ref):
        u0 = u_ref[...]
        s1, s2 = _conv_taps(u0, p_ref[...], pl.program_id(0) == 0)
        cw = cw_ref[...]
        u = cw[0:1] * s2 + cw[1:2] * s1 + cw[2:3] * u0 + cb_ref[...]
        f_ref[...] = (_gelu(u) * g_ref[...]).astype(BF16)

    return pl.pallas_call(
        body, name=name, grid=(T // tm,),
        in_specs=[row, prev, row, pl.BlockSpec((3, D_FF), lambda i: (0, 0)), pl.BlockSpec((1, D_FF), lambda i: (0, 0))],
        out_specs=row, out_shape=jax.ShapeDtypeStruct((T, D_FF), BF16),
        compiler_params=_params("parallel"),
    )(u0, u0, gt, cw, cb.reshape(1, D_FF))


def _ffn_mid_bwd_a(u0, gt, cw, cb, df, name):
    T = u0.shape[0]
    tm = _tile(T, (FFN_TM, 128))
    row, prev, _ = _ffn_specs(T, tm)

    def body(u_ref, p_ref, g_ref, cw_ref, cb_ref, df_ref, du_ref, dg_ref, dcw_ref, dcb_ref):
        u0 = u_ref[...]
        s1, s2 = _conv_taps(u0, p_ref[...], pl.program_id(0) == 0)
        cw = cw_ref[...]
        u = cw[0:1] * s2 + cw[1:2] * s1 + cw[2:3] * u0 + cb_ref[...]
        a, vjp = jax.vjp(_gelu, u)
        df = df_ref[...]
        dg_ref[...] = (df * a).astype(BF16)
        du = vjp(df * g_ref[...])[0]
        du_ref[...] = du

        @pl.when(pl.program_id(0) == 0)
        def _():
            dcw_ref[...] = jnp.zeros_like(dcw_ref)
            dcb_ref[...] = jnp.zeros_like(dcb_ref)

        dcw_ref[0:1, :] += jnp.sum(du * s2, axis=0, keepdims=True)
        dcw_ref[1:2, :] += jnp.sum(du * s1, axis=0, keepdims=True)
        dcw_ref[2:3, :] += jnp.sum(du * u0, axis=0, keepdims=True)
        dcb_ref[...] += jnp.sum(du, axis=0, keepdims=True)

    c3 = pl.BlockSpec((3, D_FF), lambda i: (0, 0))
    c1 = pl.BlockSpec((1, D_FF), lambda i: (0, 0))
    return pl.pallas_call(
        body, name=name, grid=(T // tm,),
        in_specs=[row, prev, row, c3, c1, row], out_specs=[row, row, c3, c1],
        out_shape=[jax.ShapeDtypeStruct((T, D_FF), F32), jax.ShapeDtypeStruct((T, D_FF), BF16),
                   jax.ShapeDtypeStruct((3, D_FF), F32), jax.ShapeDtypeStruct((1, D_FF), F32)],
        compiler_params=_params("arbitrary"),
    )(u0, u0, gt, cw, cb.reshape(1, D_FF), df)


def _ffn_mid_bwd_b(du, cw, name):
    T = du.shape[0]
    tm = _tile(T, (FFN_TM, 128))
    row, _, nxt = _ffn_specs(T, tm)

    def body(du_ref, n_ref, cw_ref, o_ref):
        du = du_ref[...]
        nx = jnp.where(pl.program_id(0) == T // tm - 1, 0.0, n_ref[...])
        r = lax.broadcasted_iota(jnp.int32, du.shape, 0)
        u1 = jnp.where(r == tm - 1, nx[0:1], pltpu.roll(du, tm - 1, 0))
        u2 = jnp.where(r == tm - 1, nx[1:2], jnp.where(r == tm - 2, nx[0:1], pltpu.roll(du, tm - 2, 0)))
        cw = cw_ref[...]
        o_ref[...] = (cw[2:3] * du + cw[1:2] * u1 + cw[0:1] * u2).astype(BF16)

    return pl.pallas_call(
        body, name=name, grid=(T // tm,),
        in_specs=[row, nxt, pl.BlockSpec((3, D_FF), lambda i: (0, 0))],
        out_specs=row, out_shape=jax.ShapeDtypeStruct((T, D_FF), BF16),
        compiler_params=_params("parallel"),
    )(du, du, cw)


def _mesh_peers():
    x, y, c = lax.axis_index("x"), lax.axis_index("y"), lax.axis_index("c")
    peers = []
    for k in range(1, N_DEV):
        px = 1 - x if k & 4 else x
        py = 1 - y if k & 2 else y
        pc = 1 - c if k & 1 else c
        peers.append(((px, py, pc), 4 * px + 2 * py + pc))
    return 4 * x + 2 * y + c, peers


def _exchange(src, name, scatter):
    shape = src.shape[-2:]

    def body(s_ref, o_ref, send_sems, recv_sems, local_sem):
        me, peers = _mesh_peers()
        mine = pltpu.make_async_copy(s_ref.at[me] if scatter else s_ref, o_ref.at[me], local_sem)
        mine.start()
        sends, recvs = [], []
        for k, (dev, idx) in enumerate(peers):
            sends.append(pltpu.make_async_remote_copy(
                src_ref=s_ref.at[idx] if scatter else s_ref, dst_ref=o_ref.at[me],
                send_sem=send_sems.at[k], recv_sem=recv_sems.at[k],
                device_id=dev, device_id_type=pl.DeviceIdType.MESH))
            recvs.append(pltpu.make_async_remote_copy(
                src_ref=s_ref.at[idx] if scatter else s_ref, dst_ref=o_ref.at[idx],
                send_sem=send_sems.at[k], recv_sem=recv_sems.at[k],
                device_id=dev, device_id_type=pl.DeviceIdType.MESH))
        for cp in sends:
            cp.start()
        for cp in recvs:
            cp.wait_recv()
        for cp in sends:
            cp.wait_send()
        mine.wait()

    return pl.pallas_call(
        body, name=name,
        in_specs=[pl.BlockSpec(memory_space=pl.ANY)], out_specs=pl.BlockSpec(memory_space=pl.ANY),
        out_shape=jax.ShapeDtypeStruct((N_DEV,) + shape, src.dtype),
        scratch_shapes=[pltpu.SemaphoreType.DMA((N_DEV - 1,)), pltpu.SemaphoreType.DMA((N_DEV - 1,)),
                        pltpu.SemaphoreType.DMA],
    )(src)


def _sum_devices(parts, name):
    _, R, C = parts.shape
    tr = _tile(R, (256, 128, 64, 32, 16, 8))

    def body(p_ref, o_ref):
        acc = p_ref[0].astype(F32)
        for j in range(1, N_DEV):
            acc = acc + p_ref[j].astype(F32)
        o_ref[...] = acc

    return pl.pallas_call(
        body, name=name, grid=(R // tr,),
        in_specs=[pl.BlockSpec((N_DEV, tr, C), lambda i: (0, i, 0))],
        out_specs=pl.BlockSpec((tr, C), lambda i: (i, 0)),
        out_shape=jax.ShapeDtypeStruct((R, C), F32),
        compiler_params=_params("parallel"),
    )(parts)


def _adamw(w, g, m, v, name):
    shape = w.shape
    C = shape[-1]
    R = int(np.prod(shape[:-1])) if len(shape) > 1 else 1
    tr = _tile(R, (256, 128, 64, 32, 16, 8))

    def body(w_ref, g_ref, m_ref, v_ref, d_ref, nm_ref, nv_ref):
        g = g_ref[...]
        m = ADAM_B1 * m_ref[...] + (1.0 - ADAM_B1) * g
        v = ADAM_B2 * v_ref[...] + (1.0 - ADAM_B2) * (g * g)
        m_hat = m / (1.0 - ADAM_B1 ** ADAM_STEP)
        v_hat = v / (1.0 - ADAM_B2 ** ADAM_STEP)
        d_ref[...] = -ADAM_LR * (m_hat / (jnp.sqrt(v_hat) + ADAM_EPS) + ADAM_WD * w_ref[...])
        nm_ref[...] = m
        nv_ref[...] = v

    blk = pl.BlockSpec((tr, C), lambda i: (i, 0))
    outs = pl.pallas_call(
        body, name=name, grid=(R // tr,), in_specs=[blk] * 4, out_specs=[blk] * 3,
        out_shape=[jax.ShapeDtypeStruct((R, C), F32)] * 3,
        compiler_params=_params("parallel"),
    )(*[t.reshape(R, C) for t in (w, g, m, v)])
    return [o.reshape(shape) for o in outs]


PACK_ROWS = (1056, 352, 352, 128, 352, 32, 64, 64)
PACK_LAYER = sum(PACK_ROWS)
SMALL_SIZES = (("rel_bias", 384), ("norm_mix_w", 4096), ("ret_norm_w", 2048), ("norm_ffn_w", 4096),
               ("conv_b", 11264), ("final_norm_w", 1024), ("b_gate", 12288), ("conv_w", 33792))
SMALL_ROWS = 72


def _layer_forward(x, wl, tabs, l):
    sv = {"x_in": x}
    h = _rms_fwd(x, wl["norm_mix_w"], f"rms_mix_fwd_{l}")
    lo = _matmul(h, wl["win_lo"], "nt", f"in_proj_lo_{l}", out_dtype=BF16)
    hi = _matmul(h, wl["win_hi"], "nt", f"in_proj_hi_{l}")
    outs, lses = [], []
    for g in range(3):
        o, s = _a_attn_fwd(lo, tabs["bias"], g, f"a_fwd_{g}_{l}")
        outs.append(o)
        lses.append(s)
    y_a = _a_merge_fwd(outs, lses, f"a_merge_fwd_{l}")
    y_b, ctot = _sb_fwd(lo, f"sb_fwd_{l}")
    y_c, o_r, states = _ret_fwd(hi, wl["ret_norm_w"], tabs["ret"], tabs["decay"], f"ret_fwd_{l}")
    pa = _matmul(y_a, wl["wpa"], "nt", f"proj_a_{l}")
    pb = _matmul(y_b, wl["wpb"], "nt", f"proj_b_{l}")
    pc = _matmul(y_c, wl["wpc"], "nt", f"proj_c_{l}")
    merged = _merge_fwd(hi, wl["b_gate"], pa, pb, pc, f"merge_fwd_{l}")
    x_mid = _matmul(merged, wl["wout"], "nn", f"out_proj_{l}", add=x)
    h2 = _rms_fwd(x_mid, wl["norm_ffn_w"], f"rms_ffn_fwd_{l}")
    u0 = _matmul(h2, wl["wup"], "nt", f"ffn_up_{l}")
    gt = _matmul(h2, wl["wgate"], "nt", f"ffn_gate_{l}")
    f = _ffn_mid_fwd(u0, gt, wl["conv_w"], wl["conv_b"], f"ffn_mid_fwd_{l}")
    x_out = _matmul(f, wl["wdown"], "nn", f"ffn_down_{l}", add=x_mid)
    sv.update(h=h, lo=lo, hi=hi, outs=outs, lses=lses, y_a=y_a, y_b=y_b, ctot=ctot, y_c=y_c, o_r=o_r,
              states=states, pa=pa, pb=pb, pc=pc, merged=merged, x_mid=x_mid, h2=h2, u0=u0, gt=gt, f=f)
    return x_out, sv


def _layer_backward(dx, sv, wl, tabs, l):
    df = _matmul(dx, wl["wdown"], "nt", f"ffn_down_dx_{l}")
    d_wdown = _matmul(sv["f"], dx, "tn", f"ffn_down_dw_{l}")
    du, dgt, d_cw, d_cb = _ffn_mid_bwd_a(sv["u0"], sv["gt"], wl["conv_w"], wl["conv_b"], df, f"ffn_mid_bwd_a_{l}")
    du0 = _ffn_mid_bwd_b(du, wl["conv_w"], f"ffn_mid_bwd_b_{l}")
    dh2 = _matmul(du0, wl["wup"], "nn", f"ffn_up_dx_{l}")
    dh2 = _matmul(dgt, wl["wgate"], "nn", f"ffn_gate_dx_{l}", add=dh2)
    d_wup = _matmul(du0, sv["h2"], "tn", f"ffn_up_dw_{l}")
    d_wgate = _matmul(dgt, sv["h2"], "tn", f"ffn_gate_dw_{l}")
    dx_mid, d_nffn = _rms_bwd(sv["x_mid"], wl["norm_ffn_w"], dh2, dx, f"rms_ffn_bwd_{l}")
    dm = _matmul(dx_mid, wl["wout"], "nt", f"out_proj_dx_{l}")
    d_wout = _matmul(sv["merged"], dx_mid, "tn", f"out_proj_dw_{l}")
    dgi, d_bg, dpa, dpb, dpc = _merge_bwd(sv["hi"], wl["b_gate"], sv["pa"], sv["pb"], sv["pc"], dm, f"merge_bwd_{l}")
    dy_a = _matmul(dpa, wl["wpa"], "nn", f"proj_a_dx_{l}")
    dy_b = _matmul(dpb, wl["wpb"], "nn", f"proj_b_dx_{l}")
    dy_c = _matmul(dpc, wl["wpc"], "nn", f"proj_c_dx_{l}")
    d_wpa = _matmul(dpa, sv["y_a"], "tn", f"proj_a_dw_{l}")
    d_wpb = _matmul(dpb, sv["y_b"], "tn", f"proj_b_dw_{l}")
    d_wpc = _matmul(dpc, sv["y_c"], "tn", f"proj_c_dw_{l}")
    d_rq, d_rk, d_rv, d_rg, d_rnw = _ret_bwd(sv["hi"], wl["ret_norm_w"], sv["o_r"], sv["states"], dy_c,
                                             tabs["ret"], tabs["decay"], f"ret_bwd_{l}")
    d_sq, d_sk, d_sv = _sb_bwd(sv["lo"], sv["ctot"], dy_b, f"sb_bwd_{l}")
    douts, dlses = _a_merge_bwd(sv["outs"], sv["lses"], dy_a, f"a_merge_bwd_{l}")
    dqs, dks, dvs, dbs = [], [], [], []
    for g in range(3):
        dq, dk, dv, db = _a_attn_bwd(sv["lo"], tabs["bias"], sv["outs"][g], sv["lses"][g], douts[g], dlses[g],
                                     g, f"a_bwd_{g}_{l}")
        dqs.append(dq)
        dks.append(dk)
        dvs.append(dv)
        dbs.append(db)
    heads = lambda t: [t[i].astype(BF16) for i in range(SB_HEADS)]
    dlo = jnp.concatenate(dqs + dks + dvs + [d_sq] + heads(d_sk) + heads(d_sv), axis=1)
    dhi = jnp.concatenate([d_rq, d_rk, d_rv, d_rg, dgi], axis=1)
    dh = _matmul(dlo, wl["win_lo"], "nn", f"in_proj_lo_dx_{l}")
    dh = _matmul(dhi, wl["win_hi"], "nn", f"in_proj_hi_dx_{l}", add=dh)
    d_win = jnp.concatenate([_matmul(dlo, sv["h"], "tn", f"in_proj_lo_dw_{l}"),
                             _matmul(dhi, sv["h"], "tn", f"in_proj_hi_dw_{l}")], axis=0)
    dx_in, d_nmix = _rms_bwd(sv["x_in"], wl["norm_mix_w"], dh, dx_mid, f"rms_mix_bwd_{l}")
    chunks = lambda t: t.reshape(N_DEV, -1, D_MODEL)
    big = jnp.concatenate([chunks(d_win), chunks(d_wup), chunks(d_wgate), chunks(d_wout), chunks(d_wdown),
                           chunks(d_wpa), chunks(d_wpb), chunks(d_wpc)], axis=1)
    small = dict(norm_mix_w=d_nmix[0], ret_norm_w=d_rnw[0], norm_ffn_w=d_nffn[0], conv_b=d_cb[0],
                 b_gate=d_bg, conv_w=d_cw, dbias=jnp.concatenate(dbs, axis=0))
    return dx_in, big, small


def kernel(x, rel_bias, norm_mix_w, w_in, b_gate, ret_norm_w, w_proj_a, w_proj_b, w_proj_c, w_out, norm_ffn_w, w_up, w_gate, conv_w, conv_b, w_down, final_norm_w, loss_target, m_rel_bias, m_norm_mix_w, m_w_in, m_b_gate, m_ret_norm_w, m_w_proj_a, m_w_proj_b, m_w_proj_c, m_w_out, m_norm_ffn_w, m_w_up, m_w_gate, m_conv_w, m_conv_b, m_w_down, m_final_norm_w, v_rel_bias, v_norm_mix_w, v_w_in, v_b_gate, v_ret_norm_w, v_w_proj_a, v_w_proj_b, v_w_proj_c, v_w_out, v_norm_ffn_w, v_w_up, v_w_gate, v_conv_w, v_conv_b, v_w_down, v_final_norm_w):
    T = x.shape[1]
    me = 4 * lax.axis_index("x") + 2 * lax.axis_index("y") + lax.axis_index("c")

    rows = []
    for l in range(DEPTH):
        rows += [w_in[l].T, w_up[l].T, w_gate[l].T, w_out[l], w_down[l],
                 w_proj_a[l].T.reshape(-1, D_MODEL), w_proj_b[l].T.reshape(-1, D_MODEL),
                 w_proj_c[l].T.reshape(-1, D_MODEL)]
    wall = _exchange(jnp.concatenate(rows, axis=0).astype(BF16), "gather_weights", scatter=False)
    n_bg, n_cw = b_gate.size, conv_w.size
    tiny = jnp.concatenate([b_gate.reshape(-1), conv_w.reshape(-1), jnp.zeros((8 * D_MODEL - n_bg - n_cw,), F32)])
    tall = _exchange(tiny.reshape(8, D_MODEL), "gather_small_weights", scatter=False).reshape(N_DEV, -1)
    spread = lambda t, w: t.reshape(N_DEV, DEPTH, 3, w).transpose(1, 2, 0, 3).reshape(DEPTH, 3, N_DEV * w)
    b_gate_full = spread(tall[:, :n_bg], b_gate.shape[-1])
    conv_w_full = spread(tall[:, n_bg:n_bg + n_cw], conv_w.shape[-1])

    def layer_weights(l):
        base = l * PACK_LAYER
        offs = np.cumsum((0,) + PACK_ROWS)
        seg = lambda j: wall[:, base + offs[j]:base + offs[j + 1], :]
        win = seg(0).reshape(-1, D_MODEL)
        unpack = lambda j, k: seg(j).reshape(N_DEV, D_MODEL // N_DEV, k).reshape(D_MODEL, k)
        return dict(win_lo=win[:LO_WIDTH], win_hi=win[LO_WIDTH:], wup=seg(1).reshape(-1, D_MODEL),
                    wgate=seg(2).reshape(-1, D_MODEL), wout=seg(3).reshape(-1, D_MODEL),
                    wdown=seg(4).reshape(-1, D_MODEL), wpa=unpack(5, 256), wpb=unpack(6, 512), wpc=unpack(7, 512),
                    norm_mix_w=norm_mix_w[l], norm_ffn_w=norm_ffn_w[l], ret_norm_w=ret_norm_w[l],
                    b_gate=b_gate_full[l], conv_w=conv_w_full[l], conv_b=conv_b[l])

    buckets = _a_buckets()
    ret_tabs, decay = _ret_tables(T)
    tabs = dict(bias=_a_bias_tables(rel_bias, buckets), ret=ret_tabs, decay=decay)

    xs = x[0]
    saved, wls = [], []
    for l in range(DEPTH):
        wls.append(layer_weights(l))
        xs, sv = _layer_forward(xs, wls[l], tabs, l)
        saved.append(sv)
    loss_tile, dx, d_final = _loss_head(xs, final_norm_w, loss_target[0], "loss_head")
    loss = lax.psum(loss_tile[0, 0], ("x", "y", "c"))

    bigs, smalls = [None] * DEPTH, [None] * DEPTH
    for l in reversed(range(DEPTH)):
        dx, bigs[l], smalls[l] = _layer_backward(dx, saved[l], wls[l], tabs, l)
    grad_x = dx[None]

    parts = _exchange(jnp.concatenate(bigs, axis=1), "scatter_grads", scatter=True)
    mine = _sum_devices(parts, "sum_grads")
    dbias = smalls[0]["dbias"] + smalls[1]["dbias"] + smalls[2]["dbias"] + smalls[3]["dbias"]
    small_vals = dict(rel_bias=_a_bias_grad(dbias, buckets), final_norm_w=d_final[0])
    for name in ("norm_mix_w", "ret_norm_w", "norm_ffn_w", "conv_b", "b_gate", "conv_w"):
        small_vals[name] = jnp.stack([smalls[l][name] for l in range(DEPTH)])
    flat = jnp.concatenate([small_vals[n].reshape(-1) for n, _ in SMALL_SIZES])
    flat = jnp.concatenate([flat, jnp.zeros((SMALL_ROWS * D_MODEL - flat.shape[0],), F32)])
    sparts = _exchange(flat.reshape(SMALL_ROWS, D_MODEL), "gather_small_grads", scatter=False)
    ssum = _sum_devices(sparts, "sum_small_grads").reshape(-1)

    grads = {}
    off = 0
    for name, size in SMALL_SIZES:
        grads[name] = ssum[off:off + size]
        off += size
    grads["rel_bias"] = grads["rel_bias"].reshape(REL_BUCKETS, 12)
    for name in ("norm_mix_w", "norm_ffn_w"):
        grads[name] = grads[name].reshape(DEPTH, D_MODEL)
    grads["ret_norm_w"] = grads["ret_norm_w"].reshape(DEPTH, -1)
    grads["conv_b"] = grads["conv_b"].reshape(DEPTH, D_FF)
    bw, cwid = b_gate.shape[-1], conv_w.shape[-1]
    grads["b_gate"] = lax.dynamic_slice_in_dim(grads["b_gate"].reshape(DEPTH, 3, -1), me * bw, bw, axis=2)
    grads["conv_w"] = lax.dynamic_slice_in_dim(grads["conv_w"].reshape(DEPTH, 3, -1), me * cwid, cwid, axis=2)

    offs = np.cumsum((0,) + PACK_ROWS)
    per_layer = mine.reshape(DEPTH, PACK_LAYER, D_MODEL)
    seg = lambda j: per_layer[:, offs[j]:offs[j + 1], :]
    back = lambda j, k: seg(j).reshape(DEPTH, D_MODEL // N_DEV, k).transpose(0, 2, 1)
    grads["w_in"] = seg(0).transpose(0, 2, 1)
    grads["w_up"] = seg(1).transpose(0, 2, 1)
    grads["w_gate"] = seg(2).transpose(0, 2, 1)
    grads["w_out"] = seg(3)
    grads["w_down"] = seg(4)
    grads["w_proj_a"] = back(5, 256)
    grads["w_proj_b"] = back(6, 512)
    grads["w_proj_c"] = back(7, 512)

    order = ["rel_bias", "norm_mix_w", "w_in", "b_gate", "ret_norm_w", "w_proj_a", "w_proj_b", "w_proj_c",
             "w_out", "norm_ffn_w", "w_up", "w_gate", "conv_w", "conv_b", "w_down", "final_norm_w"]
    ws = dict(rel_bias=rel_bias, norm_mix_w=norm_mix_w, w_in=w_in, b_gate=b_gate, ret_norm_w=ret_norm_w,
              w_proj_a=w_proj_a, w_proj_b=w_proj_b, w_proj_c=w_proj_c, w_out=w_out, norm_ffn_w=norm_ffn_w,
              w_up=w_up, w_gate=w_gate, conv_w=conv_w, conv_b=conv_b, w_down=w_down, final_norm_w=final_norm_w)
    ms = dict(rel_bias=m_rel_bias, norm_mix_w=m_norm_mix_w, w_in=m_w_in, b_gate=m_b_gate, ret_norm_w=m_ret_norm_w,
              w_proj_a=m_w_proj_a, w_proj_b=m_w_proj_b, w_proj_c=m_w_proj_c, w_out=m_w_out,
              norm_ffn_w=m_norm_ffn_w, w_up=m_w_up, w_gate=m_w_gate, conv_w=m_conv_w, conv_b=m_conv_b,
              w_down=m_w_down, final_norm_w=m_final_norm_w)
    vs = dict(rel_bias=v_rel_bias, norm_mix_w=v_norm_mix_w, w_in=v_w_in, b_gate=v_b_gate, ret_norm_w=v_ret_norm_w,
              w_proj_a=v_w_proj_a, w_proj_b=v_w_proj_b, w_proj_c=v_w_proj_c, w_out=v_w_out,
              norm_ffn_w=v_norm_ffn_w, w_up=v_w_up, w_gate=v_w_gate, conv_w=v_conv_w, conv_b=v_conv_b,
              w_down=v_w_down, final_norm_w=v_final_norm_w)
    deltas, new_m, new_v = [], [], []
    for name in order:
        g = grads[name].reshape(ws[name].shape)
        grads[name] = g
        d, nm, nv = _adamw(ws[name], g, ms[name], vs[name], f"adamw_{name}")
        deltas.append(d)
        new_m.append(nm)
        new_v.append(nv)
    return (loss, grad_x, *[grads[n] for n in order], *deltas, *new_m, *new_v)
```

```python
import functools

import numpy as np
import jax
import jax.numpy as jnp
from jax import lax
from jax.experimental import pallas as pl
from jax.experimental.pallas import tpu as pltpu

F32 = jnp.float32
BF16 = jnp.bfloat16

D_MODEL = 1024
DEPTH = 4
N_DEV = 8
BLOCK = 128
DIL_GROUPS = ((128, 1), (512, 4), (2048, 16))
A_HEADS_PER_GROUP = 4
HEAD_DIM = 64
A_GROUP_WIDTH = A_HEADS_PER_GROUP * HEAD_DIM
SB_HEADS = 4
SB_HEAD_DIM = 128
RET_HEADS = 4
RET_DK = 64
RET_DV = 128
ROPE_BASE = 10000.0
REL_BUCKETS = 32
REL_MAX_DIST = 2048
D_FF = 2816
EPS = 1e-6
LO_WIDTH = 3840
HI_WIDTH = 4608
A_SCALE = HEAD_DIM ** -0.5
SB_SCALE = SB_HEAD_DIM ** -0.5
RET_SCALE = RET_DK ** -0.5
NEG = -1e30

ADAM_LR = 0.001
ADAM_B1 = 0.9
ADAM_B2 = 0.999
ADAM_EPS = 1e-08
ADAM_WD = 0.01
ADAM_STEP = 10

VMEM_LIMIT_V7X = 56 * 1024 * 1024

_NT = (((1,), (1,)), ((), ()))
_NN = (((1,), (0,)), ((), ()))
_TN = (((0,), (0,)), ((), ()))


def _dot(a, b, dims):
    return lax.dot_general(a, b, dims, preferred_element_type=F32)


def _tile(n, cands):
    for c in cands:
        if n % c == 0:
            return c
    return n


def _params(*sem):
    return pltpu.CompilerParams(dimension_semantics=sem, vmem_limit_bytes=VMEM_LIMIT_V7X)


def _matmul(a, b, mode, name, out_dtype=None, add=None):
    if out_dtype is None:
        out_dtype = BF16 if mode == "tn" else F32
    if mode == "tn":
        (K, M), N = a.shape, b.shape[1]
    elif mode == "nn":
        (M, K), N = a.shape, b.shape[1]
    else:
        (M, K), N = a.shape, b.shape[0]
    wide = (1536, 1408, 1280, 1152, 1024, 768, 512, 384, 256, 128)
    if mode == "tn":
        tm, tn, tk = _tile(M, wide), _tile(N, wide[4:]), _tile(K, (1024, 512, 256, 128))
    else:
        tm, tn, tk = _tile(M, (512, 256, 128)), _tile(N, wide), K
    nk = K // tk
    if mode == "tn":
        a_spec = pl.BlockSpec((tk, tm), lambda i, j, k: (k, i))
    else:
        a_spec = pl.BlockSpec((tm, tk), lambda i, j, k: (i, k))
    if mode == "nt":
        b_spec = pl.BlockSpec((tn, tk), lambda i, j, k: (j, k))
    else:
        b_spec = pl.BlockSpec((tk, tn), lambda i, j, k: (k, j))
    dims = {"nt": _NT, "nn": _NN, "tn": _TN}[mode]
    o_spec = pl.BlockSpec((tm, tn), lambda i, j, k: (i, j))
    has_add = add is not None

    def finish(refs, r):
        if has_add:
            r = r + refs[2][...]
        refs[-1 if nk == 1 else -2][...] = r.astype(out_dtype)

    def body(*refs):
        part = _dot(refs[0][...].astype(BF16), refs[1][...].astype(BF16), dims)
        if nk == 1:
            finish(refs, part)
            return
        acc = refs[-1]
        k = pl.program_id(2)

        @pl.when(k == 0)
        def _():
            acc[...] = part

        @pl.when(k > 0)
        def _():
            acc[...] += part

        @pl.when(k == nk - 1)
        def _():
            finish(refs, acc[...])

    ins = [a, b] + ([add] if has_add else [])
    specs = [a_spec, b_spec] + ([o_spec] if has_add else [])
    return pl.pallas_call(
        body, name=name, grid=(M // tm, N // tn, nk),
        in_specs=specs, out_specs=o_spec,
        out_shape=jax.ShapeDtypeStruct((M, N), out_dtype),
        scratch_shapes=[] if nk == 1 else [pltpu.VMEM((tm, tn), F32)],
        compiler_params=_params("parallel", "parallel", "arbitrary"),
    )(*ins)


def _rms(x, w):
    return x * lax.rsqrt(jnp.mean(x * x, axis=-1, keepdims=True) + EPS) * w


def _rms_fwd(x, w, name):
    T = x.shape[0]
    tm = _tile(T, (512, 256, 128))

    def body(x_ref, w_ref, o_ref):
        o_ref[...] = _rms(x_ref[...], w_ref[...]).astype(BF16)

    return pl.pallas_call(
        body, name=name, grid=(T // tm,),
        in_specs=[pl.BlockSpec((tm, D_MODEL), lambda i: (i, 0)), pl.BlockSpec((1, D_MODEL), lambda i: (0, 0))],
        out_specs=pl.BlockSpec((tm, D_MODEL), lambda i: (i, 0)),
        out_shape=jax.ShapeDtypeStruct((T, D_MODEL), BF16),
        compiler_params=_params("parallel"),
    )(x, w.reshape(1, D_MODEL))


def _rms_bwd(x, w, dh, res, name):
    T = x.shape[0]
    tm = _tile(T, (512, 256, 128))

    def body(x_ref, w_ref, dh_ref, res_ref, dx_ref, dw_ref):
        _, vjp = jax.vjp(_rms, x_ref[...], w_ref[...])
        dx, dw = vjp(dh_ref[...])
        dx_ref[...] = dx + res_ref[...]

        @pl.when(pl.program_id(0) == 0)
        def _():
            dw_ref[...] = jnp.zeros_like(dw_ref)

        dw_ref[...] += dw

    row = pl.BlockSpec((tm, D_MODEL), lambda i: (i, 0))
    vec = pl.BlockSpec((1, D_MODEL), lambda i: (0, 0))
    return pl.pallas_call(
        body, name=name, grid=(T // tm,),
        in_specs=[row, vec, row, row], out_specs=[row, vec],
        out_shape=[jax.ShapeDtypeStruct((T, D_MODEL), F32), jax.ShapeDtypeStruct((1, D_MODEL), F32)],
        compiler_params=_params("arbitrary"),
    )(x, w.reshape(1, D_MODEL), dh, res)


def _loss_head(x, w, tgt, name):
    T = x.shape[0]
    tm = _tile(T, (512, 256, 128))

    def body(x_ref, w_ref, t_ref, l_ref, dx_ref, dw_ref):
        y, vjp = jax.vjp(_rms, x_ref[...], w_ref[...])
        e = y - t_ref[...]
        dx, dw = vjp(e * (1.0 / D_MODEL))
        dx_ref[...] = dx

        @pl.when(pl.program_id(0) == 0)
        def _():
            dw_ref[...] = jnp.zeros_like(dw_ref)
            l_ref[...] = jnp.zeros_like(l_ref)

        dw_ref[...] += dw
        l_ref[...] += 0.5 * jnp.sum(jnp.mean(e * e, axis=-1, keepdims=True))

    row = pl.BlockSpec((tm, D_MODEL), lambda i: (i, 0))
    vec = pl.BlockSpec((1, D_MODEL), lambda i: (0, 0))
    return pl.pallas_call(
        body, name=name, grid=(T // tm,),
        in_specs=[row, vec, row],
        out_specs=[pl.BlockSpec((8, 128), lambda i: (0, 0)), row, vec],
        out_shape=[jax.ShapeDtypeStruct((8, 128), F32), jax.ShapeDtypeStruct((T, D_MODEL), F32),
                   jax.ShapeDtypeStruct((1, D_MODEL), F32)],
        compiler_params=_params("arbitrary"),
    )(x, w.reshape(1, D_MODEL), tgt)


def _t5_bucket(dist):
    max_exact = REL_BUCKETS // 2
    n = np.maximum(dist, 0)
    large = max_exact + (np.log(np.maximum(n, 1) / max_exact) / np.log(REL_MAX_DIST / max_exact)
                         * (REL_BUCKETS - max_exact)).astype(np.int64)
    large = np.minimum(large, REL_BUCKETS - 1)
    return np.where(n < max_exact, n, large).astype(np.int32)


def _a_buckets():
    steps = np.arange(BLOCK)[:, None] + BLOCK - np.arange(2 * BLOCK)[None, :]
    out = []
    for window, dilation in DIL_GROUPS:
        in_band = (steps >= 0) & (steps <= window // dilation)
        out.append(np.where(in_band, _t5_bucket(steps * dilation), -1))
    return jnp.asarray(np.stack(out).astype(np.int32))


def _a_bias_tables(rel_bias, buckets):
    def body(rb_ref, bk_ref, o_ref):
        hh = pl.program_id(0)
        bk = bk_ref[0]
        acc = jnp.full((BLOCK, 2 * BLOCK), NEG, F32)
        for b in range(REL_BUCKETS):
            acc = jnp.where(bk == b, rb_ref[b, hh], acc)
        o_ref[0] = acc

    return pl.pallas_call(
        body, name="a_bias_tables", grid=(12,),
        in_specs=[pl.BlockSpec(memory_space=pltpu.SMEM),
                  pl.BlockSpec((1, BLOCK, 2 * BLOCK), lambda h: (h // 4, 0, 0))],
        out_specs=pl.BlockSpec((1, BLOCK, 2 * BLOCK), lambda h: (h, 0, 0)),
        out_shape=jax.ShapeDtypeStruct((12, BLOCK, 2 * BLOCK), F32),
        compiler_params=_params("parallel"),
    )(rel_bias, buckets)


def _a_bias_grad(dbias, buckets):
    def body(db_ref, bk_ref, o_ref):
        bk = bk_ref[0]
        db = db_ref[0]
        lane = lax.broadcasted_iota(jnp.int32, (8, 128), 1)
        acc = jnp.zeros((8, 128), F32)
        for b in range(REL_BUCKETS):
            acc = jnp.where(lane == b, jnp.sum(jnp.where(bk == b, db, 0.0)), acc)
        o_ref[0] = acc

    out = pl.pallas_call(
        body, name="a_bias_grad", grid=(12,),
        in_specs=[pl.BlockSpec((1, BLOCK, 2 * BLOCK), lambda h: (h, 0, 0)),
                  pl.BlockSpec((1, BLOCK, 2 * BLOCK), lambda h: (h // 4, 0, 0))],
        out_specs=pl.BlockSpec((1, 8, 128), lambda h: (h, 0, 0)),
        out_shape=jax.ShapeDtypeStruct((12, 8, 128), F32),
        compiler_params=_params("parallel"),
    )(dbias, buckets)
    return out[:, 0, :REL_BUCKETS].T


def _a_logits(q_ref, kp_ref, kc_ref, b_ref, h, ok):
    sl = slice(h * HEAD_DIM, (h + 1) * HEAD_DIM)
    qh = q_ref[:, sl]
    s = jnp.concatenate([_dot(qh, kp_ref[:, sl], _NT), _dot(qh, kc_ref[:, sl], _NT)], axis=1)
    s = s * A_SCALE + b_ref[h]
    return jnp.where(ok, s, NEG)


def _a_attn_fwd(proj_lo, bias_all, g, name):
    T = proj_lo.shape[0]
    d = DIL_GROUPS[g][1]
    L = T // d
    nb = L // BLOCK
    nlo = LO_WIDTH // A_GROUP_WIDTH
    pv = proj_lo.reshape(L, d * LO_WIDTH)

    def body(q_ref, kc_ref, kp_ref, vc_ref, vp_ref, b_ref, o_ref, l_ref):
        n = pl.program_id(1)
        col = lax.broadcasted_iota(jnp.int32, (BLOCK, 2 * BLOCK), 1)
        ok = col >= jnp.where(n > 0, 0, BLOCK)
        heads = range(A_HEADS_PER_GROUP)
        sls = [slice(h * HEAD_DIM, (h + 1) * HEAD_DIM) for h in heads]
        s = [_a_logits(q_ref, kp_ref, kc_ref, b_ref, h, ok) for h in heads]
        m = [jnp.max(s[h], axis=-1, keepdims=True) for h in heads]
        p = [jnp.exp(s[h] - m[h]) for h in heads]
        l = [jnp.sum(p[h], axis=-1, keepdims=True) for h in heads]
        pb = [p[h].astype(BF16) for h in heads]
        o = [_dot(pb[h][:, :BLOCK], vp_ref[:, sls[h]], _NN) + _dot(pb[h][:, BLOCK:], vc_ref[:, sls[h]], _NN)
             for h in heads]
        for h in heads:
            o_ref[:, sls[h]] = o[h] / l[h]
            l_ref[:, sls[h]] = jnp.broadcast_to(m[h] + jnp.log(l[h]), (BLOCK, HEAD_DIM))

    blk = (BLOCK, A_GROUP_WIDTH)
    out_spec = pl.BlockSpec(blk, lambda r, n: (n, r))
    out, lse = pl.pallas_call(
        body, name=name, grid=(d, nb),
        in_specs=[pl.BlockSpec(blk, lambda r, n: (n, r * nlo + g)),
                  pl.BlockSpec(blk, lambda r, n: (n, r * nlo + 3 + g)),
                  pl.BlockSpec(blk, lambda r, n: (jnp.maximum(n - 1, 0), r * nlo + 3 + g)),
                  pl.BlockSpec(blk, lambda r, n: (n, r * nlo + 6 + g)),
                  pl.BlockSpec(blk, lambda r, n: (jnp.maximum(n - 1, 0), r * nlo + 6 + g)),
                  pl.BlockSpec((4, BLOCK, 2 * BLOCK), lambda r, n: (g, 0, 0))],
        out_specs=[out_spec, out_spec],
        out_shape=[jax.ShapeDtypeStruct((L, d * A_GROUP_WIDTH), F32)] * 2,
        compiler_params=_params("parallel", "arbitrary"),
    )(pv, pv, pv, pv, pv, bias_all)
    return out.reshape(T, A_GROUP_WIDTH), lse.reshape(T, A_GROUP_WIDTH), (out, lse)


def _a_attn_bwd(proj_lo, bias_all, out, lse, dout, dlse, g, name):
    T = proj_lo.shape[0]
    d = DIL_GROUPS[g][1]
    L = T // d
    nb = L // BLOCK
    nlo = LO_WIDTH // A_GROUP_WIDTH
    pv = proj_lo.reshape(L, d * LO_WIDTH)
    view = lambda t: t.reshape(L, d * A_GROUP_WIDTH)

    def body(q_ref, kc_ref, kp_ref, vc_ref, vp_ref, b_ref, o_ref, l_ref, do_ref, dl_ref,
             dq_ref, dk_ref, dv_ref, db_ref, ck, cv):
        r = pl.program_id(0)
        n = pl.program_id(1)

        @pl.when((r == 0) & (n == 0))
        def _():
            db_ref[...] = jnp.zeros_like(db_ref)

        @pl.when(n == 0)
        def _():
            ck[...] = jnp.zeros_like(ck)
            cv[...] = jnp.zeros_like(cv)

        @pl.when(n < nb)
        def _():
            col = lax.broadcasted_iota(jnp.int32, (BLOCK, 2 * BLOCK), 1)
            ok = col >= jnp.where(n > 0, 0, BLOCK)
            heads = range(A_HEADS_PER_GROUP)
            sls = [slice(h * HEAD_DIM, (h + 1) * HEAD_DIM) for h in heads]
            s = [_a_logits(q_ref, kp_ref, kc_ref, b_ref, h, ok) for h in heads]
            p = [jnp.exp(s[h] - l_ref[:, h * HEAD_DIM:h * HEAD_DIM + 1]) for h in heads]
            do = [do_ref[:, sls[h]] for h in heads]
            dob = [do[h].astype(BF16) for h in heads]
            shift = [jnp.sum(dl_ref[:, sls[h]] - do[h] * o_ref[:, sls[h]], axis=-1, keepdims=True) for h in heads]
            dp = [jnp.concatenate([_dot(dob[h], vp_ref[:, sls[h]], _NT), _dot(dob[h], vc_ref[:, sls[h]], _NT)],
                                  axis=1) for h in heads]
            ds = [p[h] * (dp[h] + shift[h]) for h in heads]
            for h in heads:
                db_ref[h] += ds[h]
            dsb = [(ds[h] * A_SCALE).astype(BF16) for h in heads]
            pb = [p[h].astype(BF16) for h in heads]
            qh = [q_ref[:, sls[h]] for h in heads]
            dq = [_dot(dsb[h][:, :BLOCK], kp_ref[:, sls[h]], _NN) + _dot(dsb[h][:, BLOCK:], kc_ref[:, sls[h]], _NN)
                  for h in heads]
            dkp = [_dot(dsb[h][:, :BLOCK], qh[h], _TN) for h in heads]
            dvp = [_dot(pb[h][:, :BLOCK], dob[h], _TN) for h in heads]
            dkc = [_dot(dsb[h][:, BLOCK:], qh[h], _TN) for h in heads]
            dvc = [_dot(pb[h][:, BLOCK:], dob[h], _TN) for h in heads]
            for h in heads:
                dq_ref[:, sls[h]] = dq[h].astype(BF16)
                dk_ref[:, sls[h]] = (ck[:, sls[h]] + dkp[h]).astype(BF16)
                dv_ref[:, sls[h]] = (cv[:, sls[h]] + dvp[h]).astype(BF16)
                ck[:, sls[h]] = dkc[h]
                cv[:, sls[h]] = dvc[h]

        @pl.when(n == nb)
        def _():
            dk_ref[...] = ck[...].astype(BF16)
            dv_ref[...] = cv[...].astype(BF16)

    blk = (BLOCK, A_GROUP_WIDTH)
    nq = lambda n: jnp.minimum(n, nb - 1)
    prv = lambda n: jnp.maximum(jnp.minimum(n, nb - 1) - 1, 0)
    cur_o = pl.BlockSpec(blk, lambda r, n: (nq(n), r))
    lag_o = pl.BlockSpec(blk, lambda r, n: (jnp.maximum(n - 1, 0), r))
    dq, dk, dv, db = pl.pallas_call(
        body, name=name, grid=(d, nb + 1),
        in_specs=[pl.BlockSpec(blk, lambda r, n: (nq(n), r * nlo + g)),
                  pl.BlockSpec(blk, lambda r, n: (nq(n), r * nlo + 3 + g)),
                  pl.BlockSpec(blk, lambda r, n: (prv(n), r * nlo + 3 + g)),
                  pl.BlockSpec(blk, lambda r, n: (nq(n), r * nlo + 6 + g)),
                  pl.BlockSpec(blk, lambda r, n: (prv(n), r * nlo + 6 + g)),
                  pl.BlockSpec((4, BLOCK, 2 * BLOCK), lambda r, n: (g, 0, 0)),
                  cur_o, cur_o, cur_o, cur_o],
        out_specs=[cur_o, lag_o, lag_o, pl.BlockSpec((4, BLOCK, 2 * BLOCK), lambda r, n: (0, 0, 0))],
        out_shape=[jax.ShapeDtypeStruct((L, d * A_GROUP_WIDTH), BF16)] * 3
        + [jax.ShapeDtypeStruct((4, BLOCK, 2 * BLOCK), F32)],
        scratch_shapes=[pltpu.VMEM(blk, F32), pltpu.VMEM(blk, F32)],
        compiler_params=_params("arbitrary", "arbitrary"),
    )(pv, pv, pv, pv, pv, bias_all, out, lse, view(dout), view(dlse))
    return dq.reshape(T, -1), dk.reshape(T, -1), dv.reshape(T, -1), db


def _a_merge(outs, lses):
    m = jnp.maximum(jnp.maximum(lses[0], lses[1]), lses[2])
    e = [jnp.exp(l - m) for l in lses]
    inv = 1.0 / (e[0] + e[1] + e[2])
    return (e[0] * outs[0] + e[1] * outs[1] + e[2] * outs[2]) * inv


def _a_merge_fwd(outs, lses, name):
    T = outs[0].shape[0]
    tm = _tile(T, (512, 256, 128))

    def body(o0, o1, o2, l0, l1, l2, y_ref):
        y_ref[...] = _a_merge([o0[...], o1[...], o2[...]], [l0[...], l1[...], l2[...]]).astype(BF16)

    row = pl.BlockSpec((tm, A_GROUP_WIDTH), lambda i: (i, 0))
    return pl.pallas_call(
        body, name=name, grid=(T // tm,), in_specs=[row] * 6, out_specs=row,
        out_shape=jax.ShapeDtypeStruct((T, A_GROUP_WIDTH), BF16),
        compiler_params=_params("parallel"),
    )(*outs, *lses)


def _a_merge_bwd(outs, lses, dy, name):
    T = outs[0].shape[0]
    tm = _tile(T, (512, 256, 128))

    def body(o0, o1, o2, l0, l1, l2, dy_ref, *outs_ref):
        _, vjp = jax.vjp(_a_merge, [o0[...], o1[...], o2[...]], [l0[...], l1[...], l2[...]])
        do, dl = vjp(dy_ref[...])
        for ref, val in zip(outs_ref, list(do) + list(dl)):
            ref[...] = val

    row = pl.BlockSpec((tm, A_GROUP_WIDTH), lambda i: (i, 0))
    res = pl.pallas_call(
        body, name=name, grid=(T // tm,), in_specs=[row] * 7, out_specs=[row] * 6,
        out_shape=[jax.ShapeDtypeStruct((T, A_GROUP_WIDTH), F32)] * 6,
        compiler_params=_params("parallel"),
    )(*outs, *lses, dy)
    return res[:3], res[3:]


SB_TQ = 512
SB_KS = 256
SB_UNROLL = 4
SB_Q_COL, SB_K_COL, SB_V_COL = 18, 22, 26


def _sb_tri():
    return jnp.asarray(np.arange(SB_KS)[:, None] > np.arange(SB_KS)[None, :], BF16)


def _logit_parts(m):
    z = m * SB_SCALE
    e = jnp.exp2(jnp.abs(m) * (-SB_SCALE * 1.4426950408889634))
    return z, jnp.maximum(z, 0.0) + jnp.log(1.0 + e)


def _sb_sizes(T):
    nsub = SB_TQ // SB_KS
    assert T % SB_TQ == 0 and SB_TQ % SB_KS == 0 and SB_UNROLL in (nsub, 2 * nsub)
    return SB_TQ, SB_KS, nsub, SB_UNROLL


def _wide(c, width):
    return jnp.concatenate([c] * (width // BLOCK), axis=1)


def _sb_fwd(proj_lo, name):
    T = proj_lo.shape[0]
    TQ, KS, nsub, unroll = _sb_sizes(T)
    nq = T // TQ

    def body(q_ref, k_ref, v_ref, tri_ref, o_ref, c_ref, acc, cs):
        i = pl.program_id(1)
        q = q_ref[...]
        tri = tri_ref[...]
        row = i * TQ + lax.broadcasted_iota(jnp.int32, (TQ, KS), 0)
        col = lax.broadcasted_iota(jnp.int32, (TQ, KS), 1)

        def sub(n, masked, c, o):
            off = pl.multiple_of(n * KS, KS)
            z, nl = _logit_parts(_dot(q, k_ref[pl.ds(off, KS), :], _NT))
            if masked:
                valid = row > n * KS + col
                nl = jnp.where(valid, nl, 0.0)
            w = jnp.exp(z - nl - _dot(nl.astype(BF16), tri, _NN) - _wide(c, KS))
            if masked:
                w = jnp.where(valid, w, 0.0)
            o = o + _dot(w.astype(BF16), v_ref[pl.ds(off, KS), :], _NN)
            return c + jnp.sum(nl, axis=1, keepdims=True), o

        c = jnp.zeros((TQ, BLOCK), F32)
        o = jnp.zeros((TQ, BLOCK), F32)
        for j in reversed(range(nsub)):
            c, o = sub(i * nsub + j, True, c, o)
        cs[...] = c
        acc[...] = o

        def group(top, count):
            c = cs[...]
            o = jnp.zeros((TQ, BLOCK), F32)
            for u in range(count):
                c, o = sub(top - u, False, c, o)
            cs[...] = c
            acc[...] += o

        below = i * nsub
        odd = below % unroll

        @pl.when(odd != 0)
        def _():
            group(below - 1, nsub)

        def step(m, carry):
            group(below - odd - 1 - m * unroll, unroll)
            return carry

        lax.fori_loop(0, below // unroll, step, 0)
        o_ref[...] = acc[...]
        c_ref[...] = cs[...]

    qo = pl.BlockSpec((TQ, BLOCK), lambda h, i: (i, h))
    return pl.pallas_call(
        body, name=name, grid=(SB_HEADS, nq),
        in_specs=[pl.BlockSpec((TQ, BLOCK), lambda h, i: (i, SB_Q_COL + h)),
                  pl.BlockSpec((T, BLOCK), lambda h, i: (0, SB_K_COL + h)),
                  pl.BlockSpec((T, BLOCK), lambda h, i: (0, SB_V_COL + h)),
                  pl.BlockSpec((SB_KS, SB_KS), lambda h, i: (0, 0))],
        out_specs=[qo, qo],
        out_shape=[jax.ShapeDtypeStruct((T, SB_HEADS * BLOCK), F32)] * 2,
        scratch_shapes=[pltpu.VMEM((TQ, BLOCK), F32), pltpu.VMEM((TQ, BLOCK), F32)],
        compiler_params=_params("parallel", "arbitrary"),
    )(proj_lo, proj_lo, proj_lo, _sb_tri())


def _sb_bwd(proj_lo, ctot, dy, name):
    T = proj_lo.shape[0]
    TQ, KS, nsub, unroll = _sb_sizes(T)
    nq = T // TQ

    def body(q_ref, k_ref, v_ref, c_ref, do_ref, tri_ref, dq_ref, dk_hbm, dv_hbm,
             dka, dva, dqa, cps, pgs, sem):
        h = pl.program_id(0)
        i = pl.program_id(1)

        @pl.when(i == 0)
        def _():
            dka[...] = jnp.zeros_like(dka)
            dva[...] = jnp.zeros_like(dva)

        q = q_ref[...]
        tri = tri_ref[...]
        dob = do_ref[...].astype(BF16)
        ctot = c_ref[...]
        row = i * TQ + lax.broadcasted_iota(jnp.int32, (TQ, KS), 0)
        col = lax.broadcasted_iota(jnp.int32, (TQ, KS), 1)

        def sub(n, masked, cp, pg, dq):
            off = pl.multiple_of(n * KS, KS)
            kb = k_ref[pl.ds(off, KS), :]
            z, nl = _logit_parts(_dot(q, kb, _NT))
            if masked:
                valid = row > n * KS + col
                nl = jnp.where(valid, nl, 0.0)
            cp = cp + jnp.sum(nl, axis=1, keepdims=True)
            lb = z - nl
            w = jnp.exp(lb - _dot(nl.astype(BF16), tri, _NN) - _wide(ctot - cp, KS))
            if masked:
                w = jnp.where(valid, w, 0.0)
            g = w * _dot(dob, v_ref[pl.ds(off, KS), :], _NT)
            dz = g - jnp.exp(lb) * (g + _dot(g.astype(BF16), tri, _NT) + _wide(pg, KS))
            if masked:
                dz = jnp.where(valid, dz, 0.0)
            pg = pg + jnp.sum(g, axis=1, keepdims=True)
            dzb = dz.astype(BF16)
            dka[pl.ds(off, KS), :] += _dot(dzb, q, _TN)
            dva[pl.ds(off, KS), :] += _dot(w.astype(BF16), dob, _TN)
            return cp, pg, dq + _dot(dzb, kb, _NN)

        zero = jnp.zeros((TQ, BLOCK), F32)
        cps[...] = zero
        pgs[...] = zero
        dqa[...] = zero

        def group(first, count):
            cp, pg, dq = cps[...], pgs[...], zero
            for u in range(count):
                cp, pg, dq = sub(first + u, False, cp, pg, dq)
            cps[...] = cp
            pgs[...] = pg
            dqa[...] += dq

        below = i * nsub
        odd = below % unroll

        def step(m, carry):
            group(m * unroll, unroll)
            return carry

        lax.fori_loop(0, below // unroll, step, 0)

        @pl.when(odd != 0)
        def _():
            group(below - nsub, nsub)

        cp, pg, dq = cps[...], pgs[...], dqa[...]
        for j in range(nsub):
            cp, pg, dq = sub(i * nsub + j, True, cp, pg, dq)
        dq_ref[...] = (dq * SB_SCALE).astype(BF16)

        @pl.when(i == nq - 1)
        def _():
            def scale_rows(r, carry):
                rows = pl.ds(pl.multiple_of(r * TQ, TQ), TQ)
                dka[rows, :] = dka[rows, :] * SB_SCALE
                return carry

            lax.fori_loop(0, nq, scale_rows, 0)
            ck = pltpu.make_async_copy(dka, dk_hbm.at[h], sem.at[0])
            cv = pltpu.make_async_copy(dva, dv_hbm.at[h], sem.at[1])
            ck.start()
            cv.start()
            ck.wait()
            cv.wait()

    qo = pl.BlockSpec((TQ, BLOCK), lambda h, i: (i, h))
    return pl.pallas_call(
        body, name=name, grid=(SB_HEADS, nq),
        in_specs=[pl.BlockSpec((TQ, BLOCK), lambda h, i: (i, SB_Q_COL + h)),
                  pl.BlockSpec((T, BLOCK), lambda h, i: (0, SB_K_COL + h)),
                  pl.BlockSpec((T, BLOCK), lambda h, i: (0, SB_V_COL + h)),
                  qo, qo,
                  pl.BlockSpec((SB_KS, SB_KS), lambda h, i: (0, 0))],
        out_specs=[qo, pl.BlockSpec(memory_space=pl.ANY), pl.BlockSpec(memory_space=pl.ANY)],
        out_shape=[jax.ShapeDtypeStruct((T, SB_HEADS * BLOCK), BF16),
                   jax.ShapeDtypeStruct((SB_HEADS, T, BLOCK), F32),
                   jax.ShapeDtypeStruct((SB_HEADS, T, BLOCK), F32)],
        scratch_shapes=[pltpu.VMEM((T, BLOCK), F32), pltpu.VMEM((T, BLOCK), F32),
                        pltpu.VMEM((TQ, BLOCK), F32), pltpu.VMEM((TQ, BLOCK), F32),
                        pltpu.VMEM((TQ, BLOCK), F32), pltpu.SemaphoreType.DMA((2,))],
        compiler_params=_params("arbitrary", "arbitrary"),
    )(proj_lo, proj_lo, proj_lo, ctot, dy, _sb_tri())


def _ret_tables(T):
    half = RET_DK // 2
    inv = ROPE_BASE ** (-jnp.arange(half, dtype=F32) / half)
    ang = jnp.arange(T, dtype=jnp.int32).astype(F32)[:, None] * inv[None, :]
    cos, sin = jnp.cos(ang), jnp.sin(ang)
    cos_t = jnp.tile(cos, (1, 2 * RET_HEADS))
    sin_t = jnp.tile(jnp.concatenate([-sin, sin], axis=1), (1, RET_HEADS))
    lg = np.log1p(-np.exp2(-5.0 - np.arange(RET_HEADS)))
    n = np.arange(BLOCK)
    diff = n[:, None] - n[None, :]
    dmat = np.where(diff >= 0, np.exp(np.minimum(diff, BLOCK)[None] * lg[:, None, None]), 0.0)
    zeta = np.repeat(np.exp((BLOCK - 1 - n)[:, None] * lg[None, :]), RET_DK, axis=1)
    xi = np.repeat(np.exp((n + 1)[:, None] * lg[None, :]), RET_DK, axis=1)
    chunk_decay = [float(v) for v in np.exp(BLOCK * lg)]
    return (cos_t, sin_t, jnp.asarray(dmat, F32), jnp.asarray(xi, F32), jnp.asarray(zeta, F32)), chunk_decay


def _swap_halves(x):
    lane = lax.broadcasted_iota(jnp.int32, x.shape, 1)
    lower = (lane % RET_DK) < (RET_DK // 2)
    w = x.shape[1]
    return jnp.where(lower, pltpu.roll(x, w - RET_DK // 2, 1), pltpu.roll(x, RET_DK // 2, 1))


def _gn_gate(o, rg, w):
    mu = jnp.mean(o, axis=-1, keepdims=True)
    xc = o - mu
    var = jnp.mean(xc * xc, axis=-1, keepdims=True)
    return (rg * jax.nn.sigmoid(rg)) * (xc * lax.rsqrt(var + EPS) * w)


def _ret_fwd(proj_hi, ret_norm_w, tables, chunk_decay, name):
    T = proj_hi.shape[0]
    nc = T // BLOCK
    cos_t, sin_t, dmat, xi, zeta = tables

    def body(rq_ref, rk_ref, rv_ref, rg_ref, w_ref, cos_ref, sin_ref, d_ref, xi_ref, ze_ref,
             y_ref, o_ref, st_ref, rs):
        @pl.when(pl.program_id(0) == 0)
        def _():
            rs[...] = jnp.zeros_like(rs)

        cos, sin = cos_ref[...], sin_ref[...]
        rq, rk = rq_ref[...], rk_ref[...]
        q = rq * cos + _swap_halves(rq) * sin
        k = (rk * cos + _swap_halves(rk) * sin) * RET_SCALE
        qb, kb = q.astype(BF16), k.astype(BF16)
        qx, kz = (q * xi_ref[...]).astype(BF16), (k * ze_ref[...]).astype(BF16)
        heads = range(RET_HEADS)
        sl = [slice(h * RET_DK, (h + 1) * RET_DK) for h in heads]
        sv = [slice(h * RET_DV, (h + 1) * RET_DV) for h in heads]
        vb = [rv_ref[:, sv[h]].astype(BF16) for h in heads]
        r = [rs[h] for h in heads]
        intra = [(_dot(qb[:, sl[h]], kb[:, sl[h]], _NT) * d_ref[h]).astype(BF16) for h in heads]
        o = [_dot(intra[h], vb[h], _NN) + _dot(qx[:, sl[h]], r[h].astype(BF16), _NN) for h in heads]
        new_r = [r[h] * chunk_decay[h] + _dot(kz[:, sl[h]], vb[h], _TN) for h in heads]
        for h in heads:
            st_ref[0, h] = r[h]
            rs[h] = new_r[h]
            o_ref[:, sv[h]] = o[h]
            y_ref[:, sv[h]] = _gn_gate(o[h], rg_ref[:, sv[h]], w_ref[:, sv[h]]).astype(BF16)

    qk = (BLOCK, RET_HEADS * RET_DK)
    vv = (BLOCK, RET_HEADS * RET_DV)
    const = lambda shape: pl.BlockSpec(shape, lambda n: (0,) * len(shape))
    return pl.pallas_call(
        body, name=name, grid=(nc,),
        in_specs=[pl.BlockSpec(qk, lambda n: (n, 0)), pl.BlockSpec(qk, lambda n: (n, 1)),
                  pl.BlockSpec(vv, lambda n: (n, 1)), pl.BlockSpec(vv, lambda n: (n, 2)),
                  const((1, RET_HEADS * RET_DV)),
                  pl.BlockSpec(qk, lambda n: (n, 0)), pl.BlockSpec(qk, lambda n: (n, 0)),
                  const((RET_HEADS, BLOCK, BLOCK)), const(qk), const(qk)],
        out_specs=[pl.BlockSpec(vv, lambda n: (n, 0)), pl.BlockSpec(vv, lambda n: (n, 0)),
                   pl.BlockSpec((1, RET_HEADS, RET_DK, RET_DV), lambda n: (n, 0, 0, 0))],
        out_shape=[jax.ShapeDtypeStruct((T, RET_HEADS * RET_DV), BF16),
                   jax.ShapeDtypeStruct((T, RET_HEADS * RET_DV), F32),
                   jax.ShapeDtypeStruct((nc, RET_HEADS, RET_DK, RET_DV), F32)],
        scratch_shapes=[pltpu.VMEM((RET_HEADS, RET_DK, RET_DV), F32)],
        compiler_params=_params("arbitrary"),
    )(proj_hi, proj_hi, proj_hi, proj_hi, ret_norm_w.reshape(1, -1), cos_t, sin_t, dmat, xi, zeta)


def _ret_bwd(proj_hi, ret_norm_w, o_r, states, dy, tables, chunk_decay, name):
    T = proj_hi.shape[0]
    nc = T // BLOCK
    cos_t, sin_t, dmat, xi, zeta = tables

    def body(rq_ref, rk_ref, rv_ref, rg_ref, w_ref, cos_ref, sin_ref, d_ref, xi_ref, ze_ref,
             o_ref, st_ref, dy_ref, dq_ref, dk_ref, dv_ref, dg_ref, dw_ref, drs, dqs, dks):
        @pl.when(pl.program_id(0) == 0)
        def _():
            drs[...] = jnp.zeros_like(drs)
            dw_ref[...] = jnp.zeros_like(dw_ref)

        cos, sin = cos_ref[...], sin_ref[...]
        rq, rk = rq_ref[...], rk_ref[...]
        q = rq * cos + _swap_halves(rq) * sin
        k = (rk * cos + _swap_halves(rk) * sin) * RET_SCALE
        qb, kb = q.astype(BF16), k.astype(BF16)
        qx, kz = (q * xi_ref[...]).astype(BF16), (k * ze_ref[...]).astype(BF16)
        heads = range(RET_HEADS)
        sl = [slice(h * RET_DK, (h + 1) * RET_DK) for h in heads]
        sv = [slice(h * RET_DV, (h + 1) * RET_DV) for h in heads]
        grads = [jax.vjp(_gn_gate, o_ref[:, sv[h]], rg_ref[:, sv[h]], w_ref[:, sv[h]])[1](dy_ref[:, sv[h]])
                 for h in heads]
        dob = [grads[h][0].astype(BF16) for h in heads]
        vb = [rv_ref[:, sv[h]].astype(BF16) for h in heads]
        rb = [st_ref[0, h].astype(BF16) for h in heads]
        dr = [drs[h] for h in heads]
        drb = [dr[h].astype(BF16) for h in heads]
        a = [(_dot(dob[h], vb[h], _NT) * d_ref[h]).astype(BF16) for h in heads]
        p = [(_dot(qb[:, sl[h]], kb[:, sl[h]], _NT) * d_ref[h]).astype(BF16) for h in heads]
        dqh = [_dot(a[h], kb[:, sl[h]], _NN) + _dot(dob[h], rb[h], _NT) * xi_ref[:, sl[h]] for h in heads]
        dkh = [_dot(a[h], qb[:, sl[h]], _TN) + _dot(vb[h], drb[h], _NT) * ze_ref[:, sl[h]] for h in heads]
        dvh = [_dot(p[h], dob[h], _TN) + _dot(kz[:, sl[h]], drb[h], _NN) for h in heads]
        new_dr = [dr[h] * chunk_decay[h] + _dot(qx[:, sl[h]], dob[h], _TN) for h in heads]
        for h in heads:
            dg_ref[:, sv[h]] = grads[h][1].astype(BF16)
            dw_ref[:, sv[h]] += grads[h][2]
            dqs[:, sl[h]] = dqh[h]
            dks[:, sl[h]] = dkh[h]
            dv_ref[:, sv[h]] = dvh[h].astype(BF16)
            drs[h] = new_dr[h]
        dq = dqs[...]
        dk = dks[...] * RET_SCALE
        dq_ref[...] = (dq * cos + _swap_halves(dq * sin)).astype(BF16)
        dk_ref[...] = (dk * cos + _swap_halves(dk * sin)).astype(BF16)

    qk = (BLOCK, RET_HEADS * RET_DK)
    vv = (BLOCK, RET_HEADS * RET_DV)
    rev = lambda n: nc - 1 - n
    const = lambda shape: pl.BlockSpec(shape, lambda n: (0,) * len(shape))
    return pl.pallas_call(
        body, name=name, grid=(nc,),
        in_specs=[pl.BlockSpec(qk, lambda n: (rev(n), 0)), pl.BlockSpec(qk, lambda n: (rev(n), 1)),
                  pl.BlockSpec(vv, lambda n: (rev(n), 1)), pl.BlockSpec(vv, lambda n: (rev(n), 2)),
                  const((1, RET_HEADS * RET_DV)),
                  pl.BlockSpec(qk, lambda n: (rev(n), 0)), pl.BlockSpec(qk, lambda n: (rev(n), 0)),
                  const((RET_HEADS, BLOCK, BLOCK)), const(qk), const(qk),
                  pl.BlockSpec(vv, lambda n: (rev(n), 0)),
                  pl.BlockSpec((1, RET_HEADS, RET_DK, RET_DV), lambda n: (rev(n), 0, 0, 0)),
                  pl.BlockSpec(vv, lambda n: (rev(n), 0))],
        out_specs=[pl.BlockSpec(qk, lambda n: (rev(n), 0)), pl.BlockSpec(qk, lambda n: (rev(n), 0)),
                   pl.BlockSpec(vv, lambda n: (rev(n), 0)), pl.BlockSpec(vv, lambda n: (rev(n), 0)),
                   const((1, RET_HEADS * RET_DV))],
        out_shape=[jax.ShapeDtypeStruct((T, RET_HEADS * RET_DK), BF16)] * 2
        + [jax.ShapeDtypeStruct((T, RET_HEADS * RET_DV), BF16)] * 2
        + [jax.ShapeDtypeStruct((1, RET_HEADS * RET_DV), F32)],
        scratch_shapes=[pltpu.VMEM((RET_HEADS, RET_DK, RET_DV), F32), pltpu.VMEM(qk, F32), pltpu.VMEM(qk, F32)],
        compiler_params=_params("arbitrary"),
    )(proj_hi, proj_hi, proj_hi, proj_hi, ret_norm_w.reshape(1, -1), cos_t, sin_t, dmat, xi, zeta,
      o_r, states, dy)


GATE_BLK = 512
GATE_FIRST_BLK = 3


def _gated(g0, g1, g2, b0, b1, b2, pa, pb, pc):
    return jax.nn.sigmoid(g0 + b0) * pa + jax.nn.sigmoid(g1 + b1) * pb + jax.nn.sigmoid(g2 + b2) * pc


def _gate_specs(tm):
    return [pl.BlockSpec((tm, GATE_BLK), functools.partial(lambda i, c: (i, c), c=GATE_FIRST_BLK + j))
            for j in range(6)]


def _gate_args(g, bg_ref):
    gi = [jnp.concatenate([g[2 * j][...], g[2 * j + 1][...]], axis=1) for j in range(3)]
    return gi + [bg_ref[j:j + 1, :] for j in range(3)]


def _merge_fwd(proj_hi, b_gate, pa, pb, pc, name):
    T = proj_hi.shape[0]
    tm = _tile(T, (256, 128))

    def body(g0, g1, g2, g3, g4, g5, bg_ref, pa_ref, pb_ref, pc_ref, o_ref):
        args = _gate_args((g0, g1, g2, g3, g4, g5), bg_ref)
        o_ref[...] = _gated(*args, pa_ref[...], pb_ref[...], pc_ref[...]).astype(BF16)

    row = pl.BlockSpec((tm, D_MODEL), lambda i: (i, 0))
    return pl.pallas_call(
        body, name=name, grid=(T // tm,),
        in_specs=_gate_specs(tm) + [pl.BlockSpec((3, D_MODEL), lambda i: (0, 0)), row, row, row],
        out_specs=row, out_shape=jax.ShapeDtypeStruct((T, D_MODEL), BF16),
        compiler_params=_params("parallel"),
    )(*([proj_hi] * 6), b_gate, pa, pb, pc)


def _merge_bwd(proj_hi, b_gate, pa, pb, pc, dm, name):
    T = proj_hi.shape[0]
    tm = _tile(T, (256, 128))

    def body(g0, g1, g2, g3, g4, g5, bg_ref, pa_ref, pb_ref, pc_ref, dm_ref,
             dgi_ref, dbg_ref, dpa_ref, dpb_ref, dpc_ref):
        args = _gate_args((g0, g1, g2, g3, g4, g5), bg_ref)
        _, vjp = jax.vjp(_gated, *args, pa_ref[...], pb_ref[...], pc_ref[...])
        d = vjp(dm_ref[...])
        for j in range(3):
            dgi_ref[:, j * D_MODEL:(j + 1) * D_MODEL] = d[j].astype(BF16)
        dpa_ref[...] = d[6].astype(BF16)
        dpb_ref[...] = d[7].astype(BF16)
        dpc_ref[...] = d[8].astype(BF16)

        @pl.when(pl.program_id(0) == 0)
        def _():
            dbg_ref[...] = jnp.zeros_like(dbg_ref)

        for j in range(3):
            dbg_ref[j:j + 1, :] += d[3 + j]

    row = pl.BlockSpec((tm, D_MODEL), lambda i: (i, 0))
    vec = pl.BlockSpec((3, D_MODEL), lambda i: (0, 0))
    return pl.pallas_call(
        body, name=name, grid=(T // tm,),
        in_specs=_gate_specs(tm) + [vec, row, row, row, row],
        out_specs=[pl.BlockSpec((tm, 3 * D_MODEL), lambda i: (i, 0)), vec, row, row, row],
        out_shape=[jax.ShapeDtypeStruct((T, 3 * D_MODEL), BF16), jax.ShapeDtypeStruct((3, D_MODEL), F32)]
        + [jax.ShapeDtypeStruct((T, D_MODEL), BF16)] * 3,
        compiler_params=_params("arbitrary"),
    )(*([proj_hi] * 6), b_gate, pa, pb, pc, dm)


FFN_TM = 256


def _gelu(u):
    return 0.5 * u * (1.0 + jnp.tanh(0.7978845608028654 * (u + 0.044715 * (u * u * u))))


def _conv_taps(u0, prev8, first):
    prev8 = jnp.where(first, 0.0, prev8)
    row = lax.broadcasted_iota(jnp.int32, u0.shape, 0)
    s1 = jnp.where(row == 0, prev8[7:8], pltpu.roll(u0, 1, 0))
    s2 = jnp.where(row == 0, prev8[6:7], jnp.where(row == 1, prev8[7:8], pltpu.roll(u0, 2, 0)))
    return s1, s2


def _ffn_specs(T, tm):
    row = pl.BlockSpec((tm, D_FF), lambda i: (i, 0))
    prev = pl.BlockSpec((8, D_FF), lambda i: (jnp.maximum(i * (tm // 8) - 1, 0), 0))
    nxt = pl.BlockSpec((8, D_FF), lambda i: (jnp.minimum((i + 1) * (tm // 8), T // 8 - 1), 0))
    return row, prev, nxt


def _ffn_mid_fwd(u0, gt, cw, cb, name):
    T = u0.shape[0]
    tm = _tile(T, (FFN_TM, 128))
    row, prev, _ = _ffn_specs(T, tm)

    def body(u_ref, p_ref, g_ref, cw_ref, cb_ref, f_ref):
        u0 = u_ref[...]
        s1, s2 = _conv_taps(u0, p_ref[...], pl.program_id(0) == 0)
        cw = cw_ref[...]
        u = cw[0:1] * s2 + cw[1:2] * s1 + cw[2:3] * u0 + cb_ref[...]
        f_ref[...] = (_gelu(u) * g_ref[...]).astype(BF16)

    return pl.pallas_call(
        body, name=name, grid=(T // tm,),
        in_specs=[row, prev, row, pl.BlockSpec((3, D_FF), lambda i: (0, 0)), pl.BlockSpec((1, D_FF), lambda i: (0, 0))],
        out_specs=row, out_shape=jax.ShapeDtypeStruct((T, D_FF), BF16),
        compiler_params=_params("parallel"),
    )(u0, u0, gt, cw, cb.reshape(1, D_FF))


def _ffn_mid_bwd_a(u0, gt, cw, cb, df, name):
    T = u0.shape[0]
    tm = _tile(T, (FFN_TM, 128))
    row, prev, _ = _ffn_specs(T, tm)

    def body(u_ref, p_ref, g_ref, cw_ref, cb_ref, df_ref, du_ref, dg_ref, dcw_ref, dcb_ref):
        u0 = u_ref[...]
        s1, s2 = _conv_taps(u0, p_ref[...], pl.program_id(0) == 0)
        cw = cw_ref[...]
        u = cw[0:1] * s2 + cw[1:2] * s1 + cw[2:3] * u0 + cb_ref[...]
        a, vjp = jax.vjp(_gelu, u)
        df = df_ref[...]
        dg_ref[...] = (df * a).astype(BF16)
        du = vjp(df * g_ref[...])[0]
        du_ref[...] = du

        @pl.when(pl.program_id(0) == 0)
        def _():
            dcw_ref[...] = jnp.zeros_like(dcw_ref)
            dcb_ref[...] = jnp.zeros_like(dcb_ref)

        dcw_ref[0:1, :] += jnp.sum(du * s2, axis=0, keepdims=True)
        dcw_ref[1:2, :] += jnp.sum(du * s1, axis=0, keepdims=True)
        dcw_ref[2:3, :] += jnp.sum(du * u0, axis=0, keepdims=True)
        dcb_ref[...] += jnp.sum(du, axis=0, keepdims=True)

    c3 = pl.BlockSpec((3, D_FF), lambda i: (0, 0))
    c1 = pl.BlockSpec((1, D_FF), lambda i: (0, 0))
    return pl.pallas_call(
        body, name=name, grid=(T // tm,),
        in_specs=[row, prev, row, c3, c1, row], out_specs=[row, row, c3, c1],
        out_shape=[jax.ShapeDtypeStruct((T, D_FF), F32), jax.ShapeDtypeStruct((T, D_FF), BF16),
                   jax.ShapeDtypeStruct((3, D_FF), F32), jax.ShapeDtypeStruct((1, D_FF), F32)],
        compiler_params=_params("arbitrary"),
    )(u0, u0, gt, cw, cb.reshape(1, D_FF), df)


def _ffn_mid_bwd_b(du, cw, name):
    T = du.shape[0]
    tm = _tile(T, (FFN_TM, 128))
    row, _, nxt = _ffn_specs(T, tm)

    def body(du_ref, n_ref, cw_ref, o_ref):
        du = du_ref[...]
        nx = jnp.where(pl.program_id(0) == T // tm - 1, 0.0, n_ref[...])
        r = lax.broadcasted_iota(jnp.int32, du.shape, 0)
        u1 = jnp.where(r == tm - 1, nx[0:1], pltpu.roll(du, tm - 1, 0))
        u2 = jnp.where(r == tm - 1, nx[1:2], jnp.where(r == tm - 2, nx[0:1], pltpu.roll(du, tm - 2, 0)))
        cw = cw_ref[...]
        o_ref[...] = (cw[2:3] * du + cw[1:2] * u1 + cw[0:1] * u2).astype(BF16)

    return pl.pallas_call(
        body, name=name, grid=(T // tm,),
        in_specs=[row, nxt, pl.BlockSpec((3, D_FF), lambda i: (0, 0))],
        out_specs=row, out_shape=jax.ShapeDtypeStruct((T, D_FF), BF16),
        compiler_params=_params("parallel"),
    )(du, du, cw)


def _mesh_peers():
    x, y, c = lax.axis_index("x"), lax.axis_index("y"), lax.axis_index("c")
    peers = []
    for k in range(1, N_DEV):
        px = 1 - x if k & 4 else x
        py = 1 - y if k & 2 else y
        pc = 1 - c if k & 1 else c
        peers.append(((px, py, pc), 4 * px + 2 * py + pc))
    return 4 * x + 2 * y + c, peers


def _exchange(src, name, scatter):
    shape = src.shape[-2:]

    def body(s_ref, o_ref, send_sems, recv_sems, local_sem):
        me, peers = _mesh_peers()
        mine = pltpu.make_async_copy(s_ref.at[me] if scatter else s_ref, o_ref.at[me], local_sem)
        mine.start()
        sends, recvs = [], []
        for k, (dev, idx) in enumerate(peers):
            sends.append(pltpu.make_async_remote_copy(
                src_ref=s_ref.at[idx] if scatter else s_ref, dst_ref=o_ref.at[me],
                send_sem=send_sems.at[k], recv_sem=recv_sems.at[k],
                device_id=dev, device_id_type=pl.DeviceIdType.MESH))
            recvs.append(pltpu.make_async_remote_copy(
                src_ref=s_ref.at[idx] if scatter else s_ref, dst_ref=o_ref.at[idx],
                send_sem=send_sems.at[k], recv_sem=recv_sems.at[k],
                device_id=dev, device_id_type=pl.DeviceIdType.MESH))
        for cp in sends:
            cp.start()
        for cp in recvs:
            cp.wait_recv()
        for cp in sends:
            cp.wait_send()
        mine.wait()

    return pl.pallas_call(
        body, name=name,
        in_specs=[pl.BlockSpec(memory_space=pl.ANY)], out_specs=pl.BlockSpec(memory_space=pl.ANY),
        out_shape=jax.ShapeDtypeStruct((N_DEV,) + shape, src.dtype),
        scratch_shapes=[pltpu.SemaphoreType.DMA((N_DEV - 1,)), pltpu.SemaphoreType.DMA((N_DEV - 1,)),
                        pltpu.SemaphoreType.DMA],
    )(src)


def _sum_devices(parts, name):
    _, R, C = parts.shape
    tr = _tile(R, (256, 128, 64, 32, 16, 8))

    def body(p_ref, o_ref):
        acc = p_ref[0].astype(F32)
        for j in range(1, N_DEV):
            acc = acc + p_ref[j].astype(F32)
        o_ref[...] = acc

    return pl.pallas_call(
        body, name=name, grid=(R // tr,),
        in_specs=[pl.BlockSpec((N_DEV, tr, C), lambda i: (0, i, 0))],
        out_specs=pl.BlockSpec((tr, C), lambda i: (i, 0)),
        out_shape=jax.ShapeDtypeStruct((R, C), F32),
        compiler_params=_params("parallel"),
    )(parts)


def _adamw(w, g, m, v, name):
    shape = w.shape
    C = shape[-1]
    R = int(np.prod(shape[:-1])) if len(shape) > 1 else 1
    tr = _tile(R, (256, 128, 64, 32, 16, 8))

    def body(w_ref, g_ref, m_ref, v_ref, d_ref, nm_ref, nv_ref):
        g = g_ref[...]
        m = ADAM_B1 * m_ref[...] + (1.0 - ADAM_B1) * g
        v = ADAM_B2 * v_ref[...] + (1.0 - ADAM_B2) * (g * g)
        m_hat = m / (1.0 - ADAM_B1 ** ADAM_STEP)
        v_hat = v / (1.0 - ADAM_B2 ** ADAM_STEP)
        d_ref[...] = -ADAM_LR * (m_hat / (jnp.sqrt(v_hat) + ADAM_EPS) + ADAM_WD * w_ref[...])
        nm_ref[...] = m
        nv_ref[...] = v

    blk = pl.BlockSpec((tr, C), lambda i: (i, 0))
    outs = pl.pallas_call(
        body, name=name, grid=(R // tr,), in_specs=[blk] * 4, out_specs=[blk] * 3,
        out_shape=[jax.ShapeDtypeStruct((R, C), F32)] * 3,
        compiler_params=_params("parallel"),
    )(*[t.reshape(R, C) for t in (w, g, m, v)])
    return [o.reshape(shape) for o in outs]


PACK_ROWS = (1056, 352, 352, 128, 352, 32, 64, 64)
PACK_LAYER = sum(PACK_ROWS)
SMALL_SIZES = (("rel_bias", 384), ("norm_mix_w", 4096), ("ret_norm_w", 2048), ("norm_ffn_w", 4096),
               ("conv_b", 11264), ("final_norm_w", 1024), ("b_gate", 12288), ("conv_w", 33792))
SMALL_ROWS = 72


def _layer_forward(x, wl, tabs, l):
    sv = {"x_in": x}
    h = _rms_fwd(x, wl["norm_mix_w"], f"rms_mix_fwd_{l}")
    lo = _matmul(h, wl["win_lo"], "nt", f"in_proj_lo_{l}", out_dtype=BF16)
    hi = _matmul(h, wl["win_hi"], "nt", f"in_proj_hi_{l}")
    outs, lses, a_views = [], [], []
    for g in range(3):
        o, s, views = _a_attn_fwd(lo, tabs["bias"], g, f"a_fwd_{g}_{l}")
        outs.append(o)
        lses.append(s)
        a_views.append(views)
    y_a = _a_merge_fwd(outs, lses, f"a_merge_fwd_{l}")
    y_b, ctot = _sb_fwd(lo, f"sb_fwd_{l}")
    y_c, o_r, states = _ret_fwd(hi, wl["ret_norm_w"], tabs["ret"], tabs["decay"], f"ret_fwd_{l}")
    pa = _matmul(y_a, wl["wpa"], "nt", f"proj_a_{l}")
    pb = _matmul(y_b, wl["wpb"], "nt", f"proj_b_{l}")
    pc = _matmul(y_c, wl["wpc"], "nt", f"proj_c_{l}")
    merged = _merge_fwd(hi, wl["b_gate"], pa, pb, pc, f"merge_fwd_{l}")
    x_mid = _matmul(merged, wl["wout"], "nn", f"out_proj_{l}", add=x)
    h2 = _rms_fwd(x_mid, wl["norm_ffn_w"], f"rms_ffn_fwd_{l}")
    u0 = _matmul(h2, wl["wup"], "nt", f"ffn_up_{l}")
    gt = _matmul(h2, wl["wgate"], "nt", f"ffn_gate_{l}")
    f = _ffn_mid_fwd(u0, gt, wl["conv_w"], wl["conv_b"], f"ffn_mid_fwd_{l}")
    x_out = _matmul(f, wl["wdown"], "nn", f"ffn_down_{l}", add=x_mid)
    sv.update(h=h, lo=lo, hi=hi, outs=outs, lses=lses, a_views=a_views, y_a=y_a, y_b=y_b, ctot=ctot, y_c=y_c, o_r=o_r,
              states=states, pa=pa, pb=pb, pc=pc, merged=merged, x_mid=x_mid, h2=h2, u0=u0, gt=gt, f=f)
    return x_out, sv


def _layer_backward(dx, sv, wl, tabs, l):
    df = _matmul(dx, wl["wdown"], "nt", f"ffn_down_dx_{l}")
    d_wdown = _matmul(sv["f"], dx, "tn", f"ffn_down_dw_{l}")
    du, dgt, d_cw, d_cb = _ffn_mid_bwd_a(sv["u0"], sv["gt"], wl["conv_w"], wl["conv_b"], df, f"ffn_mid_bwd_a_{l}")
    du0 = _ffn_mid_bwd_b(du, wl["conv_w"], f"ffn_mid_bwd_b_{l}")
    dh2 = _matmul(du0, wl["wup"], "nn", f"ffn_up_dx_{l}")
    dh2 = _matmul(dgt, wl["wgate"], "nn", f"ffn_gate_dx_{l}", add=dh2)
    d_wup = _matmul(du0, sv["h2"], "tn", f"ffn_up_dw_{l}")
    d_wgate = _matmul(dgt, sv["h2"], "tn", f"ffn_gate_dw_{l}")
    dx_mid, d_nffn = _rms_bwd(sv["x_mid"], wl["norm_ffn_w"], dh2, dx, f"rms_ffn_bwd_{l}")
    dm = _matmul(dx_mid, wl["wout"], "nt", f"out_proj_dx_{l}")
    d_wout = _matmul(sv["merged"], dx_mid, "tn", f"out_proj_dw_{l}")
    dgi, d_bg, dpa, dpb, dpc = _merge_bwd(sv["hi"], wl["b_gate"], sv["pa"], sv["pb"], sv["pc"], dm, f"merge_bwd_{l}")
    dy_a = _matmul(dpa, wl["wpa"], "nn", f"proj_a_dx_{l}")
    dy_b = _matmul(dpb, wl["wpb"], "nn", f"proj_b_dx_{l}")
    dy_c = _matmul(dpc, wl["wpc"], "nn", f"proj_c_dx_{l}")
    d_wpa = _matmul(dpa, sv["y_a"], "tn", f"proj_a_dw_{l}")
    d_wpb = _matmul(dpb, sv["y_b"], "tn", f"proj_b_dw_{l}")
    d_wpc = _matmul(dpc, sv["y_c"], "tn", f"proj_c_dw_{l}")
    d_rq, d_rk, d_rv, d_rg, d_rnw = _ret_bwd(sv["hi"], wl["ret_norm_w"], sv["o_r"], sv["states"], dy_c,
                                             tabs["ret"], tabs["decay"], f"ret_bwd_{l}")
    d_sq, d_sk, d_sv = _sb_bwd(sv["lo"], sv["ctot"], dy_b, f"sb_bwd_{l}")
    douts, dlses = _a_merge_bwd(sv["outs"], sv["lses"], dy_a, f"a_merge_bwd_{l}")
    dqs, dks, dvs, dbs = [], [], [], []
    for g in range(3):
        dq, dk, dv, db = _a_attn_bwd(sv["lo"], tabs["bias"], *sv["a_views"][g], douts[g], dlses[g],
                                     g, f"a_bwd_{g}_{l}")
        dqs.append(dq)
        dks.append(dk)
        dvs.append(dv)
        dbs.append(db)
    heads = lambda t: [t[i].astype(BF16) for i in range(SB_HEADS)]
    dlo = jnp.concatenate(dqs + dks + dvs + [d_sq] + heads(d_sk) + heads(d_sv), axis=1)
    dhi = jnp.concatenate([d_rq, d_rk, d_rv, d_rg, dgi], axis=1)
    dh = _matmul(dlo, wl["win_lo"], "nn", f"in_proj_lo_dx_{l}")
    dh = _matmul(dhi, wl["win_hi"], "nn", f"in_proj_hi_dx_{l}", add=dh)
    d_win = jnp.concatenate([_matmul(dlo, sv["h"], "tn", f"in_proj_lo_dw_{l}"),
                             _matmul(dhi, sv["h"], "tn", f"in_proj_hi_dw_{l}")], axis=0)
    dx_in, d_nmix = _rms_bwd(sv["x_in"], wl["norm_mix_w"], dh, dx_mid, f"rms_mix_bwd_{l}")
    chunks = lambda t: t.reshape(N_DEV, -1, D_MODEL)
    big = jnp.concatenate([chunks(d_win), chunks(d_wup), chunks(d_wgate), chunks(d_wout), chunks(d_wdown),
                           chunks(d_wpa), chunks(d_wpb), chunks(d_wpc)], axis=1)
    small = dict(norm_mix_w=d_nmix[0], ret_norm_w=d_rnw[0], norm_ffn_w=d_nffn[0], conv_b=d_cb[0],
                 b_gate=d_bg, conv_w=d_cw, dbias=jnp.concatenate(dbs, axis=0))
    return dx_in, big, small


def kernel(x, rel_bias, norm_mix_w, w_in, b_gate, ret_norm_w, w_proj_a, w_proj_b, w_proj_c, w_out, norm_ffn_w, w_up, w_gate, conv_w, conv_b, w_down, final_norm_w, loss_target, m_rel_bias, m_norm_mix_w, m_w_in, m_b_gate, m_ret_norm_w, m_w_proj_a, m_w_proj_b, m_w_proj_c, m_w_out, m_norm_ffn_w, m_w_up, m_w_gate, m_conv_w, m_conv_b, m_w_down, m_final_norm_w, v_rel_bias, v_norm_mix_w, v_w_in, v_b_gate, v_ret_norm_w, v_w_proj_a, v_w_proj_b, v_w_proj_c, v_w_out, v_norm_ffn_w, v_w_up, v_w_gate, v_conv_w, v_conv_b, v_w_down, v_final_norm_w):
    T = x.shape[1]
    me = 4 * lax.axis_index("x") + 2 * lax.axis_index("y") + lax.axis_index("c")

    rows = []
    for l in range(DEPTH):
        rows += [w_in[l].T, w_up[l].T, w_gate[l].T, w_out[l], w_down[l],
                 w_proj_a[l].T.reshape(-1, D_MODEL), w_proj_b[l].T.reshape(-1, D_MODEL),
                 w_proj_c[l].T.reshape(-1, D_MODEL)]
    wall = _exchange(jnp.concatenate(rows, axis=0).astype(BF16), "gather_weights", scatter=False)
    n_bg, n_cw = b_gate.size, conv_w.size
    tiny = jnp.concatenate([b_gate.reshape(-1), conv_w.reshape(-1), jnp.zeros((8 * D_MODEL - n_bg - n_cw,), F32)])
    tall = _exchange(tiny.reshape(8, D_MODEL), "gather_small_weights", scatter=False).reshape(N_DEV, -1)
    spread = lambda t, w: t.reshape(N_DEV, DEPTH, 3, w).transpose(1, 2, 0, 3).reshape(DEPTH, 3, N_DEV * w)
    b_gate_full = spread(tall[:, :n_bg], b_gate.shape[-1])
    conv_w_full = spread(tall[:, n_bg:n_bg + n_cw], conv_w.shape[-1])

    def layer_weights(l):
        base = l * PACK_LAYER
        offs = np.cumsum((0,) + PACK_ROWS)
        seg = lambda j: wall[:, base + offs[j]:base + offs[j + 1], :]
        win = seg(0).reshape(-1, D_MODEL)
        unpack = lambda j, k: seg(j).reshape(N_DEV, D_MODEL // N_DEV, k).reshape(D_MODEL, k)
        return dict(win_lo=win[:LO_WIDTH], win_hi=win[LO_WIDTH:], wup=seg(1).reshape(-1, D_MODEL),
                    wgate=seg(2).reshape(-1, D_MODEL), wout=seg(3).reshape(-1, D_MODEL),
                    wdown=seg(4).reshape(-1, D_MODEL), wpa=unpack(5, 256), wpb=unpack(6, 512), wpc=unpack(7, 512),
                    norm_mix_w=norm_mix_w[l], norm_ffn_w=norm_ffn_w[l], ret_norm_w=ret_norm_w[l],
                    b_gate=b_gate_full[l], conv_w=conv_w_full[l], conv_b=conv_b[l])

    buckets = _a_buckets()
    ret_tabs, decay = _ret_tables(T)
    tabs = dict(bias=_a_bias_tables(rel_bias, buckets), ret=ret_tabs, decay=decay)

    xs = x[0]
    saved, wls = [], []
    for l in range(DEPTH):
        wls.append(layer_weights(l))
        xs, sv = _layer_forward(xs, wls[l], tabs, l)
        saved.append(sv)
    loss_tile, dx, d_final = _loss_head(xs, final_norm_w, loss_target[0], "loss_head")
    loss = lax.psum(loss_tile[0, 0], ("x", "y", "c"))

    bigs, smalls = [None] * DEPTH, [None] * DEPTH
    for l in reversed(range(DEPTH)):
        dx, bigs[l], smalls[l] = _layer_backward(dx, saved[l], wls[l], tabs, l)
    grad_x = dx[None]

    parts = _exchange(jnp.concatenate(bigs, axis=1), "scatter_grads", scatter=True)
    mine = _sum_devices(parts, "sum_grads")
    dbias = smalls[0]["dbias"] + smalls[1]["dbias"] + smalls[2]["dbias"] + smalls[3]["dbias"]
    small_vals = dict(rel_bias=_a_bias_grad(dbias, buckets), final_norm_w=d_final[0])
    for name in ("norm_mix_w", "ret_norm_w", "norm_ffn_w", "conv_b", "b_gate", "conv_w"):
        small_vals[name] = jnp.stack([smalls[l][name] for l in range(DEPTH)])
    flat = jnp.concatenate([small_vals[n].reshape(-1) for n, _ in SMALL_SIZES])
    flat = jnp.concatenate([flat, jnp.zeros((SMALL_ROWS * D_MODEL - flat.shape[0],), F32)])
    sparts = _exchange(flat.reshape(SMALL_ROWS, D_MODEL), "gather_small_grads", scatter=False)
    ssum = _sum_devices(sparts, "sum_small_grads").reshape(-1)

    grads = {}
    off = 0
    for name, size in SMALL_SIZES:
        grads[name] = ssum[off:off + size]
        off += size
    grads["rel_bias"] = grads["rel_bias"].reshape(REL_BUCKETS, 12)
    for name in ("norm_mix_w", "norm_ffn_w"):
        grads[name] = grads[name].reshape(DEPTH, D_MODEL)
    grads["ret_norm_w"] = grads["ret_norm_w"].reshape(DEPTH, -1)
    grads["conv_b"] = grads["conv_b"].reshape(DEPTH, D_FF)
    bw, cwid = b_gate.shape[-1], conv_w.shape[-1]
    grads["b_gate"] = lax.dynamic_slice_in_dim(grads["b_gate"].reshape(DEPTH, 3, -1), me * bw, bw, axis=2)
    grads["conv_w"] = lax.dynamic_slice_in_dim(grads["conv_w"].reshape(DEPTH, 3, -1), me * cwid, cwid, axis=2)

    offs = np.cumsum((0,) + PACK_ROWS)
    per_layer = mine.reshape(DEPTH, PACK_LAYER, D_MODEL)
    seg = lambda j: per_layer[:, offs[j]:offs[j + 1], :]
    back = lambda j, k: seg(j).reshape(DEPTH, D_MODEL // N_DEV, k).transpose(0, 2, 1)
    grads["w_in"] = seg(0).transpose(0, 2, 1)
    grads["w_up"] = seg(1).transpose(0, 2, 1)
    grads["w_gate"] = seg(2).transpose(0, 2, 1)
    grads["w_out"] = seg(3)
    grads["w_down"] = seg(4)
    grads["w_proj_a"] = back(5, 256)
    grads["w_proj_b"] = back(6, 512)
    grads["w_proj_c"] = back(7, 512)

    order = ["rel_bias", "norm_mix_w", "w_in", "b_gate", "ret_norm_w", "w_proj_a", "w_proj_b", "w_proj_c",
             "w_out", "norm_ffn_w", "w_up", "w_gate", "conv_w", "conv_b", "w_down", "final_norm_w"]
    ws = dict(rel_bias=rel_bias, norm_mix_w=norm_mix_w, w_in=w_in, b_gate=b_gate, ret_norm_w=ret_norm_w,
              w_proj_a=w_proj_a, w_proj_b=w_proj_b, w_proj_c=w_proj_c, w_out=w_out, norm_ffn_w=norm_ffn_w,
              w_up=w_up, w_gate=w_gate, conv_w=conv_w, conv_b=conv_b, w_down=w_down, final_norm_w=final_norm_w)
    ms = dict(rel_bias=m_rel_bias, norm_mix_w=m_norm_mix_w, w_in=m_w_in, b_gate=m_b_gate, ret_norm_w=m_ret_norm_w,
              w_proj_a=m_w_proj_a, w_proj_b=m_w_proj_b, w_proj_c=m_w_proj_c, w_out=m_w_out,
              norm_ffn_w=m_norm_ffn_w, w_up=m_w_up, w_gate=m_w_gate, conv_w=m_conv_w, conv_b=m_conv_b,
              w_down=m_w_down, final_norm_w=m_final_norm_w)
    vs = dict(rel_bias=v_rel_bias, norm_mix_w=v_norm_mix_w, w_in=v_w_in, b_gate=v_b_gate, ret_norm_w=v_ret_norm_w,
              w_proj_a=v_w_proj_a, w_proj_b=v_w_proj_b, w_proj_c=v_w_proj_c, w_out=v_w_out,
              norm_ffn_w=v_norm_ffn_w, w_up=v_w_up, w_gate=v_w_gate, conv_w=v_conv_w, conv_b=v_conv_b,
              w_down=v_w_down, final_norm_w=v_final_norm_w)
    deltas, new_m, new_v = [], [], []
    for name in order:
        g = grads[name].reshape(ws[name].shape)
        grads[name] = g
        d, nm, nv = _adamw(ws[name], g, ms[name], vs[name], f"adamw_{name}")
        deltas.append(d)
        new_m.append(nm)
        new_v.append(nv)
    return (loss, grad_x, *[grads[n] for n in order], *deltas, *new_m, *new_v)
```

```python
import functools

import numpy as np
import jax
import jax.numpy as jnp
from jax import lax
from jax.experimental import pallas as pl
from jax.experimental.pallas import tpu as pltpu

F32 = jnp.float32
BF16 = jnp.bfloat16

D_MODEL = 1024
DEPTH = 4
N_DEV = 8
BLOCK = 128
DIL_GROUPS = ((128, 1), (512, 4), (2048, 16))
A_HEADS_PER_GROUP = 4
HEAD_DIM = 64
A_GROUP_WIDTH = A_HEADS_PER_GROUP * HEAD_DIM
SB_HEADS = 4
SB_HEAD_DIM = 128
RET_HEADS = 4
RET_DK = 64
RET_DV = 128
ROPE_BASE = 10000.0
REL_BUCKETS = 32
REL_MAX_DIST = 2048
D_FF = 2816
EPS = 1e-6
LO_WIDTH = 3840
HI_WIDTH = 4608
A_SCALE = HEAD_DIM ** -0.5
SB_SCALE = SB_HEAD_DIM ** -0.5
RET_SCALE = RET_DK ** -0.5
NEG = -1e30

ADAM_LR = 0.001
ADAM_B1 = 0.9
ADAM_B2 = 0.999
ADAM_EPS = 1e-08
ADAM_WD = 0.01
ADAM_STEP = 10

VMEM_LIMIT_V7X = 56 * 1024 * 1024

_NT = (((1,), (1,)), ((), ()))
_NN = (((1,), (0,)), ((), ()))
_TN = (((0,), (0,)), ((), ()))


def _dot(a, b, dims):
    return lax.dot_general(a, b, dims, preferred_element_type=F32)


def _tile(n, cands):
    for c in cands:
        if n % c == 0:
            return c
    return n


def _params(*sem):
    return pltpu.CompilerParams(dimension_semantics=sem, vmem_limit_bytes=VMEM_LIMIT_V7X)


def _matmul(a, b, mode, name, out_dtype=None, add=None):
    if out_dtype is None:
        out_dtype = BF16 if mode == "tn" else F32
    if mode == "tn":
        (K, M), N = a.shape, b.shape[1]
    elif mode == "nn":
        (M, K), N = a.shape, b.shape[1]
    else:
        (M, K), N = a.shape, b.shape[0]
    wide = (1536, 1408, 1280, 1152, 1024, 768, 512, 384, 256, 128)
    if mode == "tn":
        tm, tn, tk = _tile(M, wide), _tile(N, wide[4:]), _tile(K, (1024, 512, 256, 128))
    else:
        tm, tn, tk = _tile(M, (512, 256, 128)), _tile(N, wide), K
    nk = K // tk
    if mode == "tn":
        grid, ij = (M // tm, N // tn, nk), (lambda g0, g1: (g0, g1))
    else:
        grid, ij = (N // tn, M // tm, nk), (lambda g0, g1: (g1, g0))
    at = lambda f: (lambda g0, g1, k: f(*ij(g0, g1), k))
    if mode == "tn":
        a_spec = pl.BlockSpec((tk, tm), at(lambda i, j, k: (k, i)))
    else:
        a_spec = pl.BlockSpec((tm, tk), at(lambda i, j, k: (i, k)))
    if mode == "nt":
        b_spec = pl.BlockSpec((tn, tk), at(lambda i, j, k: (j, k)))
    else:
        b_spec = pl.BlockSpec((tk, tn), at(lambda i, j, k: (k, j)))
    dims = {"nt": _NT, "nn": _NN, "tn": _TN}[mode]
    o_spec = pl.BlockSpec((tm, tn), at(lambda i, j, k: (i, j)))
    has_add = add is not None

    def finish(refs, r):
        if has_add:
            r = r + refs[2][...]
        refs[-1 if nk == 1 else -2][...] = r.astype(out_dtype)

    def body(*refs):
        part = _dot(refs[0][...].astype(BF16), refs[1][...].astype(BF16), dims)
        if nk == 1:
            finish(refs, part)
            return
        acc = refs[-1]
        k = pl.program_id(2)

        @pl.when(k == 0)
        def _():
            acc[...] = part

        @pl.when(k > 0)
        def _():
            acc[...] += part

        @pl.when(k == nk - 1)
        def _():
            finish(refs, acc[...])

    ins = [a, b] + ([add] if has_add else [])
    specs = [a_spec, b_spec] + ([o_spec] if has_add else [])
    return pl.pallas_call(
        body, name=name, grid=grid,
        in_specs=specs, out_specs=o_spec,
        out_shape=jax.ShapeDtypeStruct((M, N), out_dtype),
        scratch_shapes=[] if nk == 1 else [pltpu.VMEM((tm, tn), F32)],
        compiler_params=_params("parallel", "parallel", "arbitrary"),
    )(*ins)


def _rms(x, w):
    return x * lax.rsqrt(jnp.mean(x * x, axis=-1, keepdims=True) + EPS) * w


def _rms_fwd(x, w, name):
    T = x.shape[0]
    tm = _tile(T, (512, 256, 128))

    def body(x_ref, w_ref, o_ref):
        o_ref[...] = _rms(x_ref[...], w_ref[...]).astype(BF16)

    return pl.pallas_call(
        body, name=name, grid=(T // tm,),
        in_specs=[pl.BlockSpec((tm, D_MODEL), lambda i: (i, 0)), pl.BlockSpec((1, D_MODEL), lambda i: (0, 0))],
        out_specs=pl.BlockSpec((tm, D_MODEL), lambda i: (i, 0)),
        out_shape=jax.ShapeDtypeStruct((T, D_MODEL), BF16),
        compiler_params=_params("parallel"),
    )(x, w.reshape(1, D_MODEL))


def _rms_bwd(x, w, dh, res, name):
    T = x.shape[0]
    tm = _tile(T, (512, 256, 128))

    def body(x_ref, w_ref, dh_ref, res_ref, dx_ref, dw_ref):
        _, vjp = jax.vjp(_rms, x_ref[...], w_ref[...])
        dx, dw = vjp(dh_ref[...])
        dx_ref[...] = dx + res_ref[...]

        @pl.when(pl.program_id(0) == 0)
        def _():
            dw_ref[...] = jnp.zeros_like(dw_ref)

        dw_ref[...] += dw

    row = pl.BlockSpec((tm, D_MODEL), lambda i: (i, 0))
    vec = pl.BlockSpec((1, D_MODEL), lambda i: (0, 0))
    return pl.pallas_call(
        body, name=name, grid=(T // tm,),
        in_specs=[row, vec, row, row], out_specs=[row, vec],
        out_shape=[jax.ShapeDtypeStruct((T, D_MODEL), F32), jax.ShapeDtypeStruct((1, D_MODEL), F32)],
        compiler_params=_params("arbitrary"),
    )(x, w.reshape(1, D_MODEL), dh, res)


def _loss_head(x, w, tgt, name):
    T = x.shape[0]
    tm = _tile(T, (512, 256, 128))

    def body(x_ref, w_ref, t_ref, l_ref, dx_ref, dw_ref):
        y, vjp = jax.vjp(_rms, x_ref[...], w_ref[...])
        e = y - t_ref[...]
        dx, dw = vjp(e * (1.0 / D_MODEL))
        dx_ref[...] = dx

        @pl.when(pl.program_id(0) == 0)
        def _():
            dw_ref[...] = jnp.zeros_like(dw_ref)
            l_ref[...] = jnp.zeros_like(l_ref)

        dw_ref[...] += dw
        l_ref[...] += 0.5 * jnp.sum(jnp.mean(e * e, axis=-1, keepdims=True))

    row = pl.BlockSpec((tm, D_MODEL), lambda i: (i, 0))
    vec = pl.BlockSpec((1, D_MODEL), lambda i: (0, 0))
    return pl.pallas_call(
        body, name=name, grid=(T // tm,),
        in_specs=[row, vec, row],
        out_specs=[pl.BlockSpec((8, 128), lambda i: (0, 0)), row, vec],
        out_shape=[jax.ShapeDtypeStruct((8, 128), F32), jax.ShapeDtypeStruct((T, D_MODEL), F32),
                   jax.ShapeDtypeStruct((1, D_MODEL), F32)],
        compiler_params=_params("arbitrary"),
    )(x, w.reshape(1, D_MODEL), tgt)


def _t5_bucket(dist):
    max_exact = REL_BUCKETS // 2
    n = np.maximum(dist, 0)
    large = max_exact + (np.log(np.maximum(n, 1) / max_exact) / np.log(REL_MAX_DIST / max_exact)
                         * (REL_BUCKETS - max_exact)).astype(np.int64)
    large = np.minimum(large, REL_BUCKETS - 1)
    return np.where(n < max_exact, n, large).astype(np.int32)


def _a_buckets():
    steps = np.arange(BLOCK)[:, None] + BLOCK - np.arange(2 * BLOCK)[None, :]
    out = []
    for window, dilation in DIL_GROUPS:
        in_band = (steps >= 0) & (steps <= window // dilation)
        out.append(np.where(in_band, _t5_bucket(steps * dilation), -1))
    return jnp.asarray(np.stack(out).astype(np.int32))


def _a_bias_tables(rel_bias, buckets):
    def body(rb_ref, bk_ref, o_ref):
        hh = pl.program_id(0)
        bk = bk_ref[0]
        acc = jnp.full((BLOCK, 2 * BLOCK), NEG, F32)
        for b in range(REL_BUCKETS):
            acc = jnp.where(bk == b, rb_ref[b, hh], acc)
        o_ref[0] = acc

    return pl.pallas_call(
        body, name="a_bias_tables", grid=(12,),
        in_specs=[pl.BlockSpec(memory_space=pltpu.SMEM),
                  pl.BlockSpec((1, BLOCK, 2 * BLOCK), lambda h: (h // 4, 0, 0))],
        out_specs=pl.BlockSpec((1, BLOCK, 2 * BLOCK), lambda h: (h, 0, 0)),
        out_shape=jax.ShapeDtypeStruct((12, BLOCK, 2 * BLOCK), F32),
        compiler_params=_params("parallel"),
    )(rel_bias, buckets)


def _a_bias_grad(dbias, buckets):
    def body(db_ref, bk_ref, o_ref):
        bk = bk_ref[0]
        db = db_ref[0]
        lane = lax.broadcasted_iota(jnp.int32, (8, 128), 1)
        acc = jnp.zeros((8, 128), F32)
        for b in range(REL_BUCKETS):
            acc = jnp.where(lane == b, jnp.sum(jnp.where(bk == b, db, 0.0)), acc)
        o_ref[0] = acc

    out = pl.pallas_call(
        body, name="a_bias_grad", grid=(12,),
        in_specs=[pl.BlockSpec((1, BLOCK, 2 * BLOCK), lambda h: (h, 0, 0)),
                  pl.BlockSpec((1, BLOCK, 2 * BLOCK), lambda h: (h // 4, 0, 0))],
        out_specs=pl.BlockSpec((1, 8, 128), lambda h: (h, 0, 0)),
        out_shape=jax.ShapeDtypeStruct((12, 8, 128), F32),
        compiler_params=_params("parallel"),
    )(dbias, buckets)
    return out[:, 0, :REL_BUCKETS].T


def _a_logits(q_ref, kp_ref, kc_ref, b_ref, h, ok):
    sl = slice(h * HEAD_DIM, (h + 1) * HEAD_DIM)
    qh = q_ref[:, sl]
    s = jnp.concatenate([_dot(qh, kp_ref[:, sl], _NT), _dot(qh, kc_ref[:, sl], _NT)], axis=1)
    s = s * A_SCALE + b_ref[h]
    return jnp.where(ok, s, NEG)


def _group_major(w):
    n = 3 * 3 * A_GROUP_WIDTH
    head = w[:n].reshape(3, 3, A_GROUP_WIDTH, -1).transpose(1, 0, 2, 3).reshape(n, -1)
    return jnp.concatenate([head, w[n:]], axis=0)


def _a_view(proj_lo, g):
    if g == 0:
        return proj_lo, proj_lo.shape[1] // A_GROUP_WIDTH
    T, d, w = proj_lo.shape[0], DIL_GROUPS[g][1], 3 * A_GROUP_WIDTH
    return proj_lo[:, g * w:(g + 1) * w].reshape(T // d, d * w), 3


def _a_attn_fwd(proj_lo, bias_all, g, name):
    T = proj_lo.shape[0]
    d = DIL_GROUPS[g][1]
    L = T // d
    nb = L // BLOCK
    pv, nlo = _a_view(proj_lo, g)

    def body(q_ref, kc_ref, kp_ref, vc_ref, vp_ref, b_ref, o_ref, l_ref):
        n = pl.program_id(1)
        col = lax.broadcasted_iota(jnp.int32, (BLOCK, 2 * BLOCK), 1)
        ok = col >= jnp.where(n > 0, 0, BLOCK)
        heads = range(A_HEADS_PER_GROUP)
        sls = [slice(h * HEAD_DIM, (h + 1) * HEAD_DIM) for h in heads]
        s = [_a_logits(q_ref, kp_ref, kc_ref, b_ref, h, ok) for h in heads]
        m = [jnp.max(s[h], axis=-1, keepdims=True) for h in heads]
        p = [jnp.exp(s[h] - m[h]) for h in heads]
        l = [jnp.sum(p[h], axis=-1, keepdims=True) for h in heads]
        pb = [p[h].astype(BF16) for h in heads]
        o = [_dot(pb[h][:, :BLOCK], vp_ref[:, sls[h]], _NN) + _dot(pb[h][:, BLOCK:], vc_ref[:, sls[h]], _NN)
             for h in heads]
        for h in heads:
            o_ref[:, sls[h]] = o[h] / l[h]
            l_ref[:, sls[h]] = jnp.broadcast_to(m[h] + jnp.log(l[h]), (BLOCK, HEAD_DIM))

    blk = (BLOCK, A_GROUP_WIDTH)
    out_spec = pl.BlockSpec(blk, lambda r, n: (n, r))
    out, lse = pl.pallas_call(
        body, name=name, grid=(d, nb),
        in_specs=[pl.BlockSpec(blk, lambda r, n: (n, r * nlo)),
                  pl.BlockSpec(blk, lambda r, n: (n, r * nlo + 1)),
                  pl.BlockSpec(blk, lambda r, n: (jnp.maximum(n - 1, 0), r * nlo + 1)),
                  pl.BlockSpec(blk, lambda r, n: (n, r * nlo + 2)),
                  pl.BlockSpec(blk, lambda r, n: (jnp.maximum(n - 1, 0), r * nlo + 2)),
                  pl.BlockSpec((4, BLOCK, 2 * BLOCK), lambda r, n: (g, 0, 0))],
        out_specs=[out_spec, out_spec],
        out_shape=[jax.ShapeDtypeStruct((L, d * A_GROUP_WIDTH), F32)] * 2,
        compiler_params=_params("parallel", "arbitrary"),
    )(pv, pv, pv, pv, pv, bias_all)
    return out.reshape(T, A_GROUP_WIDTH), lse.reshape(T, A_GROUP_WIDTH), (out, lse)


def _a_attn_bwd(proj_lo, bias_all, out, lse, dout, dlse, g, name):
    T = proj_lo.shape[0]
    d = DIL_GROUPS[g][1]
    L = T // d
    nb = L // BLOCK
    pv, nlo = _a_view(proj_lo, g)
    view = lambda t: t.reshape(L, d * A_GROUP_WIDTH)

    def body(q_ref, kc_ref, kp_ref, vc_ref, vp_ref, b_ref, o_ref, l_ref, do_ref, dl_ref,
             dq_ref, dk_ref, dv_ref, db_ref, ck, cv):
        r = pl.program_id(0)
        n = pl.program_id(1)

        @pl.when((r == 0) & (n == 0))
        def _():
            db_ref[...] = jnp.zeros_like(db_ref)

        @pl.when(n == 0)
        def _():
            ck[...] = jnp.zeros_like(ck)
            cv[...] = jnp.zeros_like(cv)

        @pl.when(n < nb)
        def _():
            col = lax.broadcasted_iota(jnp.int32, (BLOCK, 2 * BLOCK), 1)
            ok = col >= jnp.where(n > 0, 0, BLOCK)
            heads = range(A_HEADS_PER_GROUP)
            sls = [slice(h * HEAD_DIM, (h + 1) * HEAD_DIM) for h in heads]
            s = [_a_logits(q_ref, kp_ref, kc_ref, b_ref, h, ok) for h in heads]
            p = [jnp.exp(s[h] - l_ref[:, h * HEAD_DIM:h * HEAD_DIM + 1]) for h in heads]
            do = [do_ref[:, sls[h]] for h in heads]
            dob = [do[h].astype(BF16) for h in heads]
            shift = [jnp.sum(dl_ref[:, sls[h]] - do[h] * o_ref[:, sls[h]], axis=-1, keepdims=True) for h in heads]
            dp = [jnp.concatenate([_dot(dob[h], vp_ref[:, sls[h]], _NT), _dot(dob[h], vc_ref[:, sls[h]], _NT)],
                                  axis=1) for h in heads]
            ds = [p[h] * (dp[h] + shift[h]) for h in heads]
            for h in heads:
                db_ref[h] += ds[h]
            dsb = [(ds[h] * A_SCALE).astype(BF16) for h in heads]
            pb = [p[h].astype(BF16) for h in heads]
            qh = [q_ref[:, sls[h]] for h in heads]
            dq = [_dot(dsb[h][:, :BLOCK], kp_ref[:, sls[h]], _NN) + _dot(dsb[h][:, BLOCK:], kc_ref[:, sls[h]], _NN)
                  for h in heads]
            dkp = [_dot(dsb[h][:, :BLOCK], qh[h], _TN) for h in heads]
            dvp = [_dot(pb[h][:, :BLOCK], dob[h], _TN) for h in heads]
            dkc = [_dot(dsb[h][:, BLOCK:], qh[h], _TN) for h in heads]
            dvc = [_dot(pb[h][:, BLOCK:], dob[h], _TN) for h in heads]
            for h in heads:
                dq_ref[:, sls[h]] = dq[h].astype(BF16)
                dk_ref[:, sls[h]] = (ck[:, sls[h]] + dkp[h]).astype(BF16)
                dv_ref[:, sls[h]] = (cv[:, sls[h]] + dvp[h]).astype(BF16)
                ck[:, sls[h]] = dkc[h]
                cv[:, sls[h]] = dvc[h]

        @pl.when(n == nb)
        def _():
            dk_ref[...] = ck[...].astype(BF16)
            dv_ref[...] = cv[...].astype(BF16)

    blk = (BLOCK, A_GROUP_WIDTH)
    nq = lambda n: jnp.minimum(n, nb - 1)
    prv = lambda n: jnp.maximum(jnp.minimum(n, nb - 1) - 1, 0)
    cur_o = pl.BlockSpec(blk, lambda r, n: (nq(n), r))
    lag_o = pl.BlockSpec(blk, lambda r, n: (jnp.maximum(n - 1, 0), r))
    dq, dk, dv, db = pl.pallas_call(
        body, name=name, grid=(d, nb + 1),
        in_specs=[pl.BlockSpec(blk, lambda r, n: (nq(n), r * nlo)),
                  pl.BlockSpec(blk, lambda r, n: (nq(n), r * nlo + 1)),
                  pl.BlockSpec(blk, lambda r, n: (prv(n), r * nlo + 1)),
                  pl.BlockSpec(blk, lambda r, n: (nq(n), r * nlo + 2)),
                  pl.BlockSpec(blk, lambda r, n: (prv(n), r * nlo + 2)),
                  pl.BlockSpec((4, BLOCK, 2 * BLOCK), lambda r, n: (g, 0, 0)),
                  cur_o, cur_o, cur_o, cur_o],
        out_specs=[cur_o, lag_o, lag_o, pl.BlockSpec((4, BLOCK, 2 * BLOCK), lambda r, n: (0, 0, 0))],
        out_shape=[jax.ShapeDtypeStruct((L, d * A_GROUP_WIDTH), BF16)] * 3
        + [jax.ShapeDtypeStruct((4, BLOCK, 2 * BLOCK), F32)],
        scratch_shapes=[pltpu.VMEM(blk, F32), pltpu.VMEM(blk, F32)],
        compiler_params=_params("arbitrary", "arbitrary"),
    )(pv, pv, pv, pv, pv, bias_all, out, lse, view(dout), view(dlse))
    return dq.reshape(T, -1), dk.reshape(T, -1), dv.reshape(T, -1), db


def _a_merge(outs, lses):
    m = jnp.maximum(jnp.maximum(lses[0], lses[1]), lses[2])
    e = [jnp.exp(l - m) for l in lses]
    inv = 1.0 / (e[0] + e[1] + e[2])
    return (e[0] * outs[0] + e[1] * outs[1] + e[2] * outs[2]) * inv


def _a_merge_fwd(outs, lses, name):
    T = outs[0].shape[0]
    tm = _tile(T, (512, 256, 128))

    def body(o0, o1, o2, l0, l1, l2, y_ref):
        y_ref[...] = _a_merge([o0[...], o1[...], o2[...]], [l0[...], l1[...], l2[...]]).astype(BF16)

    row = pl.BlockSpec((tm, A_GROUP_WIDTH), lambda i: (i, 0))
    return pl.pallas_call(
        body, name=name, grid=(T // tm,), in_specs=[row] * 6, out_specs=row,
        out_shape=jax.ShapeDtypeStruct((T, A_GROUP_WIDTH), BF16),
        compiler_params=_params("parallel"),
    )(*outs, *lses)


def _a_merge_bwd(outs, lses, dy, name):
    T = outs[0].shape[0]
    tm = _tile(T, (512, 256, 128))

    def body(o0, o1, o2, l0, l1, l2, dy_ref, *outs_ref):
        _, vjp = jax.vjp(_a_merge, [o0[...], o1[...], o2[...]], [l0[...], l1[...], l2[...]])
        do, dl = vjp(dy_ref[...])
        for ref, val in zip(outs_ref, list(do) + list(dl)):
            ref[...] = val

    row = pl.BlockSpec((tm, A_GROUP_WIDTH), lambda i: (i, 0))
    res = pl.pallas_call(
        body, name=name, grid=(T // tm,), in_specs=[row] * 7, out_specs=[row] * 6,
        out_shape=[jax.ShapeDtypeStruct((T, A_GROUP_WIDTH), F32)] * 6,
        compiler_params=_params("parallel"),
    )(*outs, *lses, dy)
    return res[:3], res[3:]


SB_TQ = 512
SB_KS = 256
SB_UNROLL = 4
SB_Q_COL, SB_K_COL, SB_V_COL = 18, 22, 26


def _sb_tri():
    return jnp.asarray(np.arange(SB_KS)[:, None] > np.arange(SB_KS)[None, :], BF16)


def _logit_parts(m):
    z = m * SB_SCALE
    e = jnp.exp2(jnp.abs(m) * (-SB_SCALE * 1.4426950408889634))
    return z, jnp.maximum(z, 0.0) + jnp.log(1.0 + e)


def _sb_sizes(T):
    nsub = SB_TQ // SB_KS
    assert T % SB_TQ == 0 and SB_TQ % SB_KS == 0 and SB_UNROLL in (nsub, 2 * nsub)
    return SB_TQ, SB_KS, nsub, SB_UNROLL


def _wide(c, width):
    return jnp.concatenate([c] * (width // BLOCK), axis=1)


def _sb_fwd(proj_lo, name):
    T = proj_lo.shape[0]
    TQ, KS, nsub, unroll = _sb_sizes(T)
    nq = T // TQ

    def body(q_ref, k_ref, v_ref, tri_ref, o_ref, c_ref, acc, cs):
        i = pl.program_id(1)
        q = q_ref[...]
        tri = tri_ref[...]
        row = i * TQ + lax.broadcasted_iota(jnp.int32, (TQ, KS), 0)
        col = lax.broadcasted_iota(jnp.int32, (TQ, KS), 1)

        def sub(n, masked, c, o):
            off = pl.multiple_of(n * KS, KS)
            z, nl = _logit_parts(_dot(q, k_ref[pl.ds(off, KS), :], _NT))
            if masked:
                valid = row > n * KS + col
                nl = jnp.where(valid, nl, 0.0)
            w = jnp.exp(z - nl - _dot(nl.astype(BF16), tri, _NN) - _wide(c, KS))
            if masked:
                w = jnp.where(valid, w, 0.0)
            o = o + _dot(w.astype(BF16), v_ref[pl.ds(off, KS), :], _NN)
            return c + jnp.sum(nl, axis=1, keepdims=True), o

        c = jnp.zeros((TQ, BLOCK), F32)
        o = jnp.zeros((TQ, BLOCK), F32)
        for j in reversed(range(nsub)):
            c, o = sub(i * nsub + j, True, c, o)
        cs[...] = c
        acc[...] = o

        def group(top, count):
            c = cs[...]
            o = jnp.zeros((TQ, BLOCK), F32)
            for u in range(count):
                c, o = sub(top - u, False, c, o)
            cs[...] = c
            acc[...] += o

        below = i * nsub
        odd = below % unroll

        @pl.when(odd != 0)
        def _():
            group(below - 1, nsub)

        def step(m, carry):
            group(below - odd - 1 - m * unroll, unroll)
            return carry

        lax.fori_loop(0, below // unroll, step, 0)
        o_ref[...] = acc[...]
        c_ref[...] = cs[...]

    qo = pl.BlockSpec((TQ, BLOCK), lambda h, i: (i, h))
    return pl.pallas_call(
        body, name=name, grid=(SB_HEADS, nq),
        in_specs=[pl.BlockSpec((TQ, BLOCK), lambda h, i: (i, SB_Q_COL + h)),
                  pl.BlockSpec((T, BLOCK), lambda h, i: (0, SB_K_COL + h)),
                  pl.BlockSpec((T, BLOCK), lambda h, i: (0, SB_V_COL + h)),
                  pl.BlockSpec((SB_KS, SB_KS), lambda h, i: (0, 0))],
        out_specs=[qo, qo],
        out_shape=[jax.ShapeDtypeStruct((T, SB_HEADS * BLOCK), F32)] * 2,
        scratch_shapes=[pltpu.VMEM((TQ, BLOCK), F32), pltpu.VMEM((TQ, BLOCK), F32)],
        compiler_params=_params("parallel", "arbitrary"),
    )(proj_lo, proj_lo, proj_lo, _sb_tri())


def _sb_bwd(proj_lo, ctot, dy, name):
    T = proj_lo.shape[0]
    TQ, KS, nsub, unroll = _sb_sizes(T)
    nq = T // TQ

    def body(q_ref, k_ref, v_ref, c_ref, do_ref, tri_ref, dq_ref, dk_hbm, dv_hbm,
             dka, dva, dqa, cps, pgs, sem):
        h = pl.program_id(0)
        i = pl.program_id(1)

        @pl.when(i == 0)
        def _():
            dka[...] = jnp.zeros_like(dka)
            dva[...] = jnp.zeros_like(dva)

        q = q_ref[...]
        tri = tri_ref[...]
        dob = do_ref[...].astype(BF16)
        ctot = c_ref[...]
        row = i * TQ + lax.broadcasted_iota(jnp.int32, (TQ, KS), 0)
        col = lax.broadcasted_iota(jnp.int32, (TQ, KS), 1)

        def sub(n, masked, cp, pg, dq):
            off = pl.multiple_of(n * KS, KS)
            kb = k_ref[pl.ds(off, KS), :]
            z, nl = _logit_parts(_dot(q, kb, _NT))
            if masked:
                valid = row > n * KS + col
                nl = jnp.where(valid, nl, 0.0)
            cp = cp + jnp.sum(nl, axis=1, keepdims=True)
            lb = z - nl
            w = jnp.exp(lb - _dot(nl.astype(BF16), tri, _NN) - _wide(ctot - cp, KS))
            if masked:
                w = jnp.where(valid, w, 0.0)
            g = w * _dot(dob, v_ref[pl.ds(off, KS), :], _NT)
            dz = g - jnp.exp(lb) * (g + _dot(g.astype(BF16), tri, _NT) + _wide(pg, KS))
            if masked:
                dz = jnp.where(valid, dz, 0.0)
            pg = pg + jnp.sum(g, axis=1, keepdims=True)
            dzb = dz.astype(BF16)
            dka[pl.ds(off, KS), :] += _dot(dzb, q, _TN)
            dva[pl.ds(off, KS), :] += _dot(w.astype(BF16), dob, _TN)
            return cp, pg, dq + _dot(dzb, kb, _NN)

        zero = jnp.zeros((TQ, BLOCK), F32)
        cps[...] = zero
        pgs[...] = zero
        dqa[...] = zero

        def group(first, count):
            cp, pg, dq = cps[...], pgs[...], zero
            for u in range(count):
                cp, pg, dq = sub(first + u, False, cp, pg, dq)
            cps[...] = cp
            pgs[...] = pg
            dqa[...] += dq

        below = i * nsub
        odd = below % unroll

        def step(m, carry):
            group(m * unroll, unroll)
            return carry

        lax.fori_loop(0, below // unroll, step, 0)

        @pl.when(odd != 0)
        def _():
            group(below - nsub, nsub)

        cp, pg, dq = cps[...], pgs[...], dqa[...]
        for j in range(nsub):
            cp, pg, dq = sub(i * nsub + j, True, cp, pg, dq)
        dq_ref[...] = (dq * SB_SCALE).astype(BF16)

        @pl.when(i == nq - 1)
        def _():
            def scale_rows(r, carry):
                rows = pl.ds(pl.multiple_of(r * TQ, TQ), TQ)
                dka[rows, :] = dka[rows, :] * SB_SCALE
                return carry

            lax.fori_loop(0, nq, scale_rows, 0)
            ck = pltpu.make_async_copy(dka, dk_hbm.at[h], sem.at[0])
            cv = pltpu.make_async_copy(dva, dv_hbm.at[h], sem.at[1])
            ck.start()
            cv.start()
            ck.wait()
            cv.wait()

    qo = pl.BlockSpec((TQ, BLOCK), lambda h, i: (i, h))
    return pl.pallas_call(
        body, name=name, grid=(SB_HEADS, nq),
        in_specs=[pl.BlockSpec((TQ, BLOCK), lambda h, i: (i, SB_Q_COL + h)),
                  pl.BlockSpec((T, BLOCK), lambda h, i: (0, SB_K_COL + h)),
                  pl.BlockSpec((T, BLOCK), lambda h, i: (0, SB_V_COL + h)),
                  qo, qo,
                  pl.BlockSpec((SB_KS, SB_KS), lambda h, i: (0, 0))],
        out_specs=[qo, pl.BlockSpec(memory_space=pl.ANY), pl.BlockSpec(memory_space=pl.ANY)],
        out_shape=[jax.ShapeDtypeStruct((T, SB_HEADS * BLOCK), BF16),
                   jax.ShapeDtypeStruct((SB_HEADS, T, BLOCK), F32),
                   jax.ShapeDtypeStruct((SB_HEADS, T, BLOCK), F32)],
        scratch_shapes=[pltpu.VMEM((T, BLOCK), F32), pltpu.VMEM((T, BLOCK), F32),
                        pltpu.VMEM((TQ, BLOCK), F32), pltpu.VMEM((TQ, BLOCK), F32),
                        pltpu.VMEM((TQ, BLOCK), F32), pltpu.SemaphoreType.DMA((2,))],
        compiler_params=_params("arbitrary", "arbitrary"),
    )(proj_lo, proj_lo, proj_lo, ctot, dy, _sb_tri())


def _ret_tables(T):
    half = RET_DK // 2
    inv = ROPE_BASE ** (-jnp.arange(half, dtype=F32) / half)
    ang = jnp.arange(T, dtype=jnp.int32).astype(F32)[:, None] * inv[None, :]
    cos, sin = jnp.cos(ang), jnp.sin(ang)
    cos_t = jnp.tile(cos, (1, 2 * RET_HEADS))
    sin_t = jnp.tile(jnp.concatenate([-sin, sin], axis=1), (1, RET_HEADS))
    lg = np.log1p(-np.exp2(-5.0 - np.arange(RET_HEADS)))
    n = np.arange(BLOCK)
    diff = n[:, None] - n[None, :]
    dmat = np.where(diff >= 0, np.exp(np.minimum(diff, BLOCK)[None] * lg[:, None, None]), 0.0)
    zeta = np.repeat(np.exp((BLOCK - 1 - n)[:, None] * lg[None, :]), RET_DK, axis=1)
    xi = np.repeat(np.exp((n + 1)[:, None] * lg[None, :]), RET_DK, axis=1)
    chunk_decay = [float(v) for v in np.exp(BLOCK * lg)]
    return (cos_t, sin_t, jnp.asarray(dmat, F32), jnp.asarray(xi, F32), jnp.asarray(zeta, F32)), chunk_decay


def _swap_halves(x):
    lane = lax.broadcasted_iota(jnp.int32, x.shape, 1)
    lower = (lane % RET_DK) < (RET_DK // 2)
    w = x.shape[1]
    return jnp.where(lower, pltpu.roll(x, w - RET_DK // 2, 1), pltpu.roll(x, RET_DK // 2, 1))


def _gn_gate(o, rg, w):
    mu = jnp.mean(o, axis=-1, keepdims=True)
    xc = o - mu
    var = jnp.mean(xc * xc, axis=-1, keepdims=True)
    return (rg * jax.nn.sigmoid(rg)) * (xc * lax.rsqrt(var + EPS) * w)


def _ret_fwd(proj_hi, ret_norm_w, tables, chunk_decay, name):
    T = proj_hi.shape[0]
    nc = T // BLOCK
    cos_t, sin_t, dmat, xi, zeta = tables

    def body(rq_ref, rk_ref, rv_ref, rg_ref, w_ref, cos_ref, sin_ref, d_ref, xi_ref, ze_ref,
             y_ref, o_ref, st_ref, rs):
        @pl.when(pl.program_id(0) == 0)
        def _():
            rs[...] = jnp.zeros_like(rs)

        cos, sin = cos_ref[...], sin_ref[...]
        rq, rk = rq_ref[...], rk_ref[...]
        q = rq * cos + _swap_halves(rq) * sin
        k = (rk * cos + _swap_halves(rk) * sin) * RET_SCALE
        qb, kb = q.astype(BF16), k.astype(BF16)
        qx, kz = (q * xi_ref[...]).astype(BF16), (k * ze_ref[...]).astype(BF16)
        heads = range(RET_HEADS)
        sl = [slice(h * RET_DK, (h + 1) * RET_DK) for h in heads]
        sv = [slice(h * RET_DV, (h + 1) * RET_DV) for h in heads]
        vb = [rv_ref[:, sv[h]].astype(BF16) for h in heads]
        r = [rs[h] for h in heads]
        intra = [(_dot(qb[:, sl[h]], kb[:, sl[h]], _NT) * d_ref[h]).astype(BF16) for h in heads]
        o = [_dot(intra[h], vb[h], _NN) + _dot(qx[:, sl[h]], r[h].astype(BF16), _NN) for h in heads]
        new_r = [r[h] * chunk_decay[h] + _dot(kz[:, sl[h]], vb[h], _TN) for h in heads]
        for h in heads:
            st_ref[0, h] = r[h]
            rs[h] = new_r[h]
            o_ref[:, sv[h]] = o[h]
            y_ref[:, sv[h]] = _gn_gate(o[h], rg_ref[:, sv[h]], w_ref[:, sv[h]]).astype(BF16)

    qk = (BLOCK, RET_HEADS * RET_DK)
    vv = (BLOCK, RET_HEADS * RET_DV)
    const = lambda shape: pl.BlockSpec(shape, lambda n: (0,) * len(shape))
    return pl.pallas_call(
        body, name=name, grid=(nc,),
        in_specs=[pl.BlockSpec(qk, lambda n: (n, 0)), pl.BlockSpec(qk, lambda n: (n, 1)),
                  pl.BlockSpec(vv, lambda n: (n, 1)), pl.BlockSpec(vv, lambda n: (n, 2)),
                  const((1, RET_HEADS * RET_DV)),
                  pl.BlockSpec(qk, lambda n: (n, 0)), pl.BlockSpec(qk, lambda n: (n, 0)),
                  const((RET_HEADS, BLOCK, BLOCK)), const(qk), const(qk)],
        out_specs=[pl.BlockSpec(vv, lambda n: (n, 0)), pl.BlockSpec(vv, lambda n: (n, 0)),
                   pl.BlockSpec((1, RET_HEADS, RET_DK, RET_DV), lambda n: (n, 0, 0, 0))],
        out_shape=[jax.ShapeDtypeStruct((T, RET_HEADS * RET_DV), BF16),
                   jax.ShapeDtypeStruct((T, RET_HEADS * RET_DV), F32),
                   jax.ShapeDtypeStruct((nc, RET_HEADS, RET_DK, RET_DV), F32)],
        scratch_shapes=[pltpu.VMEM((RET_HEADS, RET_DK, RET_DV), F32)],
        compiler_params=_params("arbitrary"),
    )(proj_hi, proj_hi, proj_hi, proj_hi, ret_norm_w.reshape(1, -1), cos_t, sin_t, dmat, xi, zeta)


def _ret_bwd(proj_hi, ret_norm_w, o_r, states, dy, tables, chunk_decay, name):
    T = proj_hi.shape[0]
    nc = T // BLOCK
    cos_t, sin_t, dmat, xi, zeta = tables

    def body(rq_ref, rk_ref, rv_ref, rg_ref, w_ref, cos_ref, sin_ref, d_ref, xi_ref, ze_ref,
             o_ref, st_ref, dy_ref, dq_ref, dk_ref, dv_ref, dg_ref, dw_ref, drs, dqs, dks):
        @pl.when(pl.program_id(0) == 0)
        def _():
            drs[...] = jnp.zeros_like(drs)
            dw_ref[...] = jnp.zeros_like(dw_ref)

        cos, sin = cos_ref[...], sin_ref[...]
        rq, rk = rq_ref[...], rk_ref[...]
        q = rq * cos + _swap_halves(rq) * sin
        k = (rk * cos + _swap_halves(rk) * sin) * RET_SCALE
        qb, kb = q.astype(BF16), k.astype(BF16)
        qx, kz = (q * xi_ref[...]).astype(BF16), (k * ze_ref[...]).astype(BF16)
        heads = range(RET_HEADS)
        sl = [slice(h * RET_DK, (h + 1) * RET_DK) for h in heads]
        sv = [slice(h * RET_DV, (h + 1) * RET_DV) for h in heads]
        grads = [jax.vjp(_gn_gate, o_ref[:, sv[h]], rg_ref[:, sv[h]], w_ref[:, sv[h]])[1](dy_ref[:, sv[h]])
                 for h in heads]
        dob = [grads[h][0].astype(BF16) for h in heads]
        vb = [rv_ref[:, sv[h]].astype(BF16) for h in heads]
        rb = [st_ref[0, h].astype(BF16) for h in heads]
        dr = [drs[h] for h in heads]
        drb = [dr[h].astype(BF16) for h in heads]
        a = [(_dot(dob[h], vb[h], _NT) * d_ref[h]).astype(BF16) for h in heads]
        p = [(_dot(qb[:, sl[h]], kb[:, sl[h]], _NT) * d_ref[h]).astype(BF16) for h in heads]
        dqh = [_dot(a[h], kb[:, sl[h]], _NN) + _dot(dob[h], rb[h], _NT) * xi_ref[:, sl[h]] for h in heads]
        dkh = [_dot(a[h], qb[:, sl[h]], _TN) + _dot(vb[h], drb[h], _NT) * ze_ref[:, sl[h]] for h in heads]
        dvh = [_dot(p[h], dob[h], _TN) + _dot(kz[:, sl[h]], drb[h], _NN) for h in heads]
        new_dr = [dr[h] * chunk_decay[h] + _dot(qx[:, sl[h]], dob[h], _TN) for h in heads]
        for h in heads:
            dg_ref[:, sv[h]] = grads[h][1].astype(BF16)
            dw_ref[:, sv[h]] += grads[h][2]
            dqs[:, sl[h]] = dqh[h]
            dks[:, sl[h]] = dkh[h]
            dv_ref[:, sv[h]] = dvh[h].astype(BF16)
            drs[h] = new_dr[h]
        dq = dqs[...]
        dk = dks[...] * RET_SCALE
        dq_ref[...] = (dq * cos + _swap_halves(dq * sin)).astype(BF16)
        dk_ref[...] = (dk * cos + _swap_halves(dk * sin)).astype(BF16)

    qk = (BLOCK, RET_HEADS * RET_DK)
    vv = (BLOCK, RET_HEADS * RET_DV)
    rev = lambda n: nc - 1 - n
    const = lambda shape: pl.BlockSpec(shape, lambda n: (0,) * len(shape))
    return pl.pallas_call(
        body, name=name, grid=(nc,),
        in_specs=[pl.BlockSpec(qk, lambda n: (rev(n), 0)), pl.BlockSpec(qk, lambda n: (rev(n), 1)),
                  pl.BlockSpec(vv, lambda n: (rev(n), 1)), pl.BlockSpec(vv, lambda n: (rev(n), 2)),
                  const((1, RET_HEADS * RET_DV)),
                  pl.BlockSpec(qk, lambda n: (rev(n), 0)), pl.BlockSpec(qk, lambda n: (rev(n), 0)),
                  const((RET_HEADS, BLOCK, BLOCK)), const(qk), const(qk),
                  pl.BlockSpec(vv, lambda n: (rev(n), 0)),
                  pl.BlockSpec((1, RET_HEADS, RET_DK, RET_DV), lambda n: (rev(n), 0, 0, 0)),
                  pl.BlockSpec(vv, lambda n: (rev(n), 0))],
        out_specs=[pl.BlockSpec(qk, lambda n: (rev(n), 0)), pl.BlockSpec(qk, lambda n: (rev(n), 0)),
                   pl.BlockSpec(vv, lambda n: (rev(n), 0)), pl.BlockSpec(vv, lambda n: (rev(n), 0)),
                   const((1, RET_HEADS * RET_DV))],
        out_shape=[jax.ShapeDtypeStruct((T, RET_HEADS * RET_DK), BF16)] * 2
        + [jax.ShapeDtypeStruct((T, RET_HEADS * RET_DV), BF16)] * 2
        + [jax.ShapeDtypeStruct((1, RET_HEADS * RET_DV), F32)],
        scratch_shapes=[pltpu.VMEM((RET_HEADS, RET_DK, RET_DV), F32), pltpu.VMEM(qk, F32), pltpu.VMEM(qk, F32)],
        compiler_params=_params("arbitrary"),
    )(proj_hi, proj_hi, proj_hi, proj_hi, ret_norm_w.reshape(1, -1), cos_t, sin_t, dmat, xi, zeta,
      o_r, states, dy)


GATE_BLK = 512
GATE_FIRST_BLK = 3


def _gated(g0, g1, g2, b0, b1, b2, pa, pb, pc):
    return jax.nn.sigmoid(g0 + b0) * pa + jax.nn.sigmoid(g1 + b1) * pb + jax.nn.sigmoid(g2 + b2) * pc


def _gate_specs(tm):
    return [pl.BlockSpec((tm, GATE_BLK), functools.partial(lambda i, c: (i, c), c=GATE_FIRST_BLK + j))
            for j in range(6)]


def _gate_args(g, bg_ref):
    gi = [jnp.concatenate([g[2 * j][...], g[2 * j + 1][...]], axis=1) for j in range(3)]
    return gi + [bg_ref[j:j + 1, :] for j in range(3)]


def _merge_fwd(proj_hi, b_gate, pa, pb, pc, name):
    T = proj_hi.shape[0]
    tm = _tile(T, (256, 128))

    def body(g0, g1, g2, g3, g4, g5, bg_ref, pa_ref, pb_ref, pc_ref, o_ref):
        args = _gate_args((g0, g1, g2, g3, g4, g5), bg_ref)
        o_ref[...] = _gated(*args, pa_ref[...], pb_ref[...], pc_ref[...]).astype(BF16)

    row = pl.BlockSpec((tm, D_MODEL), lambda i: (i, 0))
    return pl.pallas_call(
        body, name=name, grid=(T // tm,),
        in_specs=_gate_specs(tm) + [pl.BlockSpec((3, D_MODEL), lambda i: (0, 0)), row, row, row],
        out_specs=row, out_shape=jax.ShapeDtypeStruct((T, D_MODEL), BF16),
        compiler_params=_params("parallel"),
    )(*([proj_hi] * 6), b_gate, pa, pb, pc)


def _merge_bwd(proj_hi, b_gate, pa, pb, pc, dm, name):
    T = proj_hi.shape[0]
    tm = _tile(T, (256, 128))

    def body(g0, g1, g2, g3, g4, g5, bg_ref, pa_ref, pb_ref, pc_ref, dm_ref,
             dgi_ref, dbg_ref, dpa_ref, dpb_ref, dpc_ref):
        args = _gate_args((g0, g1, g2, g3, g4, g5), bg_ref)
        _, vjp = jax.vjp(_gated, *args, pa_ref[...], pb_ref[...], pc_ref[...])
        d = vjp(dm_ref[...])
        for j in range(3):
            dgi_ref[:, j * D_MODEL:(j + 1) * D_MODEL] = d[j].astype(BF16)
        dpa_ref[...] = d[6].astype(BF16)
        dpb_ref[...] = d[7].astype(BF16)
        dpc_ref[...] = d[8].astype(BF16)

        @pl.when(pl.program_id(0) == 0)
        def _():
            dbg_ref[...] = jnp.zeros_like(dbg_ref)

        for j in range(3):
            dbg_ref[j:j + 1, :] += d[3 + j]

    row = pl.BlockSpec((tm, D_MODEL), lambda i: (i, 0))
    vec = pl.BlockSpec((3, D_MODEL), lambda i: (0, 0))
    return pl.pallas_call(
        body, name=name, grid=(T // tm,),
        in_specs=_gate_specs(tm) + [vec, row, row, row, row],
        out_specs=[pl.BlockSpec((tm, 3 * D_MODEL), lambda i: (i, 0)), vec, row, row, row],
        out_shape=[jax.ShapeDtypeStruct((T, 3 * D_MODEL), BF16), jax.ShapeDtypeStruct((3, D_MODEL), F32)]
        + [jax.ShapeDtypeStruct((T, D_MODEL), BF16)] * 3,
        compiler_params=_params("arbitrary"),
    )(*([proj_hi] * 6), b_gate, pa, pb, pc, dm)


FFN_TM = 256


def _gelu(u):
    return 0.5 * u * (1.0 + jnp.tanh(0.7978845608028654 * (u + 0.044715 * (u * u * u))))


def _conv_taps(u0, prev8, first):
    prev8 = jnp.where(first, 0.0, prev8)
    row = lax.broadcasted_iota(jnp.int32, u0.shape, 0)
    s1 = jnp.where(row == 0, prev8[7:8], pltpu.roll(u0, 1, 0))
    s2 = jnp.where(row == 0, prev8[6:7], jnp.where(row == 1, prev8[7:8], pltpu.roll(u0, 2, 0)))
    return s1, s2


def _ffn_specs(T, tm):
    row = pl.BlockSpec((tm, D_FF), lambda i: (i, 0))
    prev = pl.BlockSpec((8, D_FF), lambda i: (jnp.maximum(i * (tm // 8) - 1, 0), 0))
    nxt = pl.BlockSpec((8, D_FF), lambda i: (jnp.minimum((i + 1) * (tm // 8), T // 8 - 1), 0))
    return row, prev, nxt


def _ffn_mid_fwd(u0, gt, cw, cb, name):
    T = u0.shape[0]
    tm = _tile(T, (FFN_TM, 128))
    row, prev, _ = _ffn_specs(T, tm)

    def body(u_ref, p_ref, g_ref, cw_ref, cb_ref, f_ref):
        u0 = u_ref[...]
        s1, s2 = _conv_taps(u0, p_ref[...], pl.program_id(0) == 0)
        cw = cw_ref[...]
        u = cw[0:1] * s2 + cw[1:2] * s1 + cw[2:3] * u0 + cb_ref[...]
        f_ref[...] = (_gelu(u) * g_ref[...]).astype(BF16)

    return pl.pallas_call(
        body, name=name, grid=(T // tm,),
        in_specs=[row, prev, row, pl.BlockSpec((3, D_FF), lambda i: (0, 0)), pl.BlockSpec((1, D_FF), lambda i: (0, 0))],
        out_specs=row, out_shape=jax.ShapeDtypeStruct((T, D_FF), BF16),
        compiler_params=_params("parallel"),
    )(u0, u0, gt, cw, cb.reshape(1, D_FF))


def _ffn_mid_bwd_a(u0, gt, cw, cb, df, name):
    T = u0.shape[0]
    tm = _tile(T, (FFN_TM, 128))
    row, prev, _ = _ffn_specs(T, tm)

    def body(u_ref, p_ref, g_ref, cw_ref, cb_ref, df_ref, du_ref, dg_ref, dcw_ref, dcb_ref):
        u0 = u_ref[...]
        s1, s2 = _conv_taps(u0, p_ref[...], pl.program_id(0) == 0)
        cw = cw_ref[...]
        u = cw[0:1] * s2 + cw[1:2] * s1 + cw[2:3] * u0 + cb_ref[...]
        a, vjp = jax.vjp(_gelu, u)
        df = df_ref[...]
        dg_ref[...] = (df * a).astype(BF16)
        du = vjp(df * g_ref[...])[0]
        du_ref[...] = du

        @pl.when(pl.program_id(0) == 0)
        def _():
            dcw_ref[...] = jnp.zeros_like(dcw_ref)
            dcb_ref[...] = jnp.zeros_like(dcb_ref)

        dcw_ref[0:1, :] += jnp.sum(du * s2, axis=0, keepdims=True)
        dcw_ref[1:2, :] += jnp.sum(du * s1, axis=0, keepdims=True)
        dcw_ref[2:3, :] += jnp.sum(du * u0, axis=0, keepdims=True)
        dcb_ref[...] += jnp.sum(du, axis=0, keepdims=True)

    c3 = pl.BlockSpec((3, D_FF), lambda i: (0, 0))
    c1 = pl.BlockSpec((1, D_FF), lambda i: (0, 0))
    return pl.pallas_call(
        body, name=name, grid=(T // tm,),
        in_specs=[row, prev, row, c3, c1, row], out_specs=[row, row, c3, c1],
        out_shape=[jax.ShapeDtypeStruct((T, D_FF), F32), jax.ShapeDtypeStruct((T, D_FF), BF16),
                   jax.ShapeDtypeStruct((3, D_FF), F32), jax.ShapeDtypeStruct((1, D_FF), F32)],
        compiler_params=_params("arbitrary"),
    )(u0, u0, gt, cw, cb.reshape(1, D_FF), df)


def _ffn_mid_bwd_b(du, cw, name):
    T = du.shape[0]
    tm = _tile(T, (FFN_TM, 128))
    row, _, nxt = _ffn_specs(T, tm)

    def body(du_ref, n_ref, cw_ref, o_ref):
        du = du_ref[...]
        nx = jnp.where(pl.program_id(0) == T // tm - 1, 0.0, n_ref[...])
        r = lax.broadcasted_iota(jnp.int32, du.shape, 0)
        u1 = jnp.where(r == tm - 1, nx[0:1], pltpu.roll(du, tm - 1, 0))
        u2 = jnp.where(r == tm - 1, nx[1:2], jnp.where(r == tm - 2, nx[0:1], pltpu.roll(du, tm - 2, 0)))
        cw = cw_ref[...]
        o_ref[...] = (cw[2:3] * du + cw[1:2] * u1 + cw[0:1] * u2).astype(BF16)

    return pl.pallas_call(
        body, name=name, grid=(T // tm,),
        in_specs=[row, nxt, pl.BlockSpec((3, D_FF), lambda i: (0, 0))],
        out_specs=row, out_shape=jax.ShapeDtypeStruct((T, D_FF), BF16),
        compiler_params=_params("parallel"),
    )(du, du, cw)


def _mesh_peers():
    x, y, c = lax.axis_index("x"), lax.axis_index("y"), lax.axis_index("c")
    peers = []
    for k in range(1, N_DEV):
        px = 1 - x if k & 4 else x
        py = 1 - y if k & 2 else y
        pc = 1 - c if k & 1 else c
        peers.append(((px, py, pc), 4 * px + 2 * py + pc))
    return 4 * x + 2 * y + c, peers


def _exchange(src, name, scatter):
    shape = src.shape[-2:]

    def body(s_ref, o_ref, send_sems, recv_sems, local_sem):
        me, peers = _mesh_peers()
        mine = pltpu.make_async_copy(s_ref.at[me] if scatter else s_ref, o_ref.at[me], local_sem)
        mine.start()
        sends, recvs = [], []
        for k, (dev, idx) in enumerate(peers):
            sends.append(pltpu.make_async_remote_copy(
                src_ref=s_ref.at[idx] if scatter else s_ref, dst_ref=o_ref.at[me],
                send_sem=send_sems.at[k], recv_sem=recv_sems.at[k],
                device_id=dev, device_id_type=pl.DeviceIdType.MESH))
            recvs.append(pltpu.make_async_remote_copy(
                src_ref=s_ref.at[idx] if scatter else s_ref, dst_ref=o_ref.at[idx],
                send_sem=send_sems.at[k], recv_sem=recv_sems.at[k],
                device_id=dev, device_id_type=pl.DeviceIdType.MESH))
        for cp in sends:
            cp.start()
        for cp in recvs:
            cp.wait_recv()
        for cp in sends:
            cp.wait_send()
        mine.wait()

    return pl.pallas_call(
        body, name=name,
        in_specs=[pl.BlockSpec(memory_space=pl.ANY)], out_specs=pl.BlockSpec(memory_space=pl.ANY),
        out_shape=jax.ShapeDtypeStruct((N_DEV,) + shape, src.dtype),
        scratch_shapes=[pltpu.SemaphoreType.DMA((N_DEV - 1,)), pltpu.SemaphoreType.DMA((N_DEV - 1,)),
                        pltpu.SemaphoreType.DMA],
    )(src)


def _sum_devices(parts, name):
    _, R, C = parts.shape
    tr = _tile(R, (256, 128, 64, 32, 16, 8))

    def body(p_ref, o_ref):
        acc = p_ref[0].astype(F32)
        for j in range(1, N_DEV):
            acc = acc + p_ref[j].astype(F32)
        o_ref[...] = acc

    return pl.pallas_call(
        body, name=name, grid=(R // tr,),
        in_specs=[pl.BlockSpec((N_DEV, tr, C), lambda i: (0, i, 0))],
        out_specs=pl.BlockSpec((tr, C), lambda i: (i, 0)),
        out_shape=jax.ShapeDtypeStruct((R, C), F32),
        compiler_params=_params("parallel"),
    )(parts)


def _adamw(w, g, m, v, name):
    shape = w.shape
    C = shape[-1]
    R = int(np.prod(shape[:-1])) if len(shape) > 1 else 1
    tr = _tile(R, (256, 128, 64, 32, 16, 8))

    def body(w_ref, g_ref, m_ref, v_ref, d_ref, nm_ref, nv_ref):
        g = g_ref[...]
        m = ADAM_B1 * m_ref[...] + (1.0 - ADAM_B1) * g
        v = ADAM_B2 * v_ref[...] + (1.0 - ADAM_B2) * (g * g)
        m_hat = m / (1.0 - ADAM_B1 ** ADAM_STEP)
        v_hat = v / (1.0 - ADAM_B2 ** ADAM_STEP)
        d_ref[...] = -ADAM_LR * (m_hat / (jnp.sqrt(v_hat) + ADAM_EPS) + ADAM_WD * w_ref[...])
        nm_ref[...] = m
        nv_ref[...] = v

    blk = pl.BlockSpec((tr, C), lambda i: (i, 0))
    outs = pl.pallas_call(
        body, name=name, grid=(R // tr,), in_specs=[blk] * 4, out_specs=[blk] * 3,
        out_shape=[jax.ShapeDtypeStruct((R, C), F32)] * 3,
        compiler_params=_params("parallel"),
    )(*[t.reshape(R, C) for t in (w, g, m, v)])
    return [o.reshape(shape) for o in outs]


PACK_ROWS = (1056, 352, 352, 128, 352, 32, 64, 64)
PACK_LAYER = sum(PACK_ROWS)
SMALL_SIZES = (("rel_bias", 384), ("norm_mix_w", 4096), ("ret_norm_w", 2048), ("norm_ffn_w", 4096),
               ("conv_b", 11264), ("final_norm_w", 1024), ("b_gate", 12288), ("conv_w", 33792))
SMALL_ROWS = 72


def _layer_forward(x, wl, tabs, l):
    sv = {"x_in": x}
    h = _rms_fwd(x, wl["norm_mix_w"], f"rms_mix_fwd_{l}")
    lo = _matmul(h, wl["win_lo"], "nt", f"in_proj_lo_{l}", out_dtype=BF16)
    hi = _matmul(h, wl["win_hi"], "nt", f"in_proj_hi_{l}")
    outs, lses, a_views = [], [], []
    for g in range(3):
        o, s, views = _a_attn_fwd(lo, tabs["bias"], g, f"a_fwd_{g}_{l}")
        outs.append(o)
        lses.append(s)
        a_views.append(views)
    y_a = _a_merge_fwd(outs, lses, f"a_merge_fwd_{l}")
    y_b, ctot = _sb_fwd(lo, f"sb_fwd_{l}")
    y_c, o_r, states = _ret_fwd(hi, wl["ret_norm_w"], tabs["ret"], tabs["decay"], f"ret_fwd_{l}")
    pa = _matmul(y_a, wl["wpa"], "nt", f"proj_a_{l}")
    pb = _matmul(y_b, wl["wpb"], "nt", f"proj_b_{l}")
    pc = _matmul(y_c, wl["wpc"], "nt", f"proj_c_{l}")
    merged = _merge_fwd(hi, wl["b_gate"], pa, pb, pc, f"merge_fwd_{l}")
    x_mid = _matmul(merged, wl["wout"], "nn", f"out_proj_{l}", add=x)
    h2 = _rms_fwd(x_mid, wl["norm_ffn_w"], f"rms_ffn_fwd_{l}")
    u0 = _matmul(h2, wl["wup"], "nt", f"ffn_up_{l}")
    gt = _matmul(h2, wl["wgate"], "nt", f"ffn_gate_{l}")
    f = _ffn_mid_fwd(u0, gt, wl["conv_w"], wl["conv_b"], f"ffn_mid_fwd_{l}")
    x_out = _matmul(f, wl["wdown"], "nn", f"ffn_down_{l}", add=x_mid)
    sv.update(h=h, lo=lo, hi=hi, outs=outs, lses=lses, a_views=a_views, y_a=y_a, y_b=y_b, ctot=ctot, y_c=y_c, o_r=o_r,
              states=states, pa=pa, pb=pb, pc=pc, merged=merged, x_mid=x_mid, h2=h2, u0=u0, gt=gt, f=f)
    return x_out, sv


def _layer_backward(dx, sv, wl, tabs, l):
    df = _matmul(dx, wl["wdown"], "nt", f"ffn_down_dx_{l}")
    d_wdown = _matmul(sv["f"], dx, "tn", f"ffn_down_dw_{l}")
    du, dgt, d_cw, d_cb = _ffn_mid_bwd_a(sv["u0"], sv["gt"], wl["conv_w"], wl["conv_b"], df, f"ffn_mid_bwd_a_{l}")
    du0 = _ffn_mid_bwd_b(du, wl["conv_w"], f"ffn_mid_bwd_b_{l}")
    dh2 = _matmul(du0, wl["wup"], "nn", f"ffn_up_dx_{l}")
    dh2 = _matmul(dgt, wl["wgate"], "nn", f"ffn_gate_dx_{l}", add=dh2)
    d_wup = _matmul(du0, sv["h2"], "tn", f"ffn_up_dw_{l}")
    d_wgate = _matmul(dgt, sv["h2"], "tn", f"ffn_gate_dw_{l}")
    dx_mid, d_nffn = _rms_bwd(sv["x_mid"], wl["norm_ffn_w"], dh2, dx, f"rms_ffn_bwd_{l}")
    dm = _matmul(dx_mid, wl["wout"], "nt", f"out_proj_dx_{l}")
    d_wout = _matmul(sv["merged"], dx_mid, "tn", f"out_proj_dw_{l}")
    dgi, d_bg, dpa, dpb, dpc = _merge_bwd(sv["hi"], wl["b_gate"], sv["pa"], sv["pb"], sv["pc"], dm, f"merge_bwd_{l}")
    dy_a = _matmul(dpa, wl["wpa"], "nn", f"proj_a_dx_{l}")
    dy_b = _matmul(dpb, wl["wpb"], "nn", f"proj_b_dx_{l}")
    dy_c = _matmul(dpc, wl["wpc"], "nn", f"proj_c_dx_{l}")
    d_wpa = _matmul(dpa, sv["y_a"], "tn", f"proj_a_dw_{l}")
    d_wpb = _matmul(dpb, sv["y_b"], "tn", f"proj_b_dw_{l}")
    d_wpc = _matmul(dpc, sv["y_c"], "tn", f"proj_c_dw_{l}")
    d_rq, d_rk, d_rv, d_rg, d_rnw = _ret_bwd(sv["hi"], wl["ret_norm_w"], sv["o_r"], sv["states"], dy_c,
                                             tabs["ret"], tabs["decay"], f"ret_bwd_{l}")
    d_sq, d_sk, d_sv = _sb_bwd(sv["lo"], sv["ctot"], dy_b, f"sb_bwd_{l}")
    douts, dlses = _a_merge_bwd(sv["outs"], sv["lses"], dy_a, f"a_merge_bwd_{l}")
    dqs, dks, dvs, dbs = [], [], [], []
    for g in range(3):
        dq, dk, dv, db = _a_attn_bwd(sv["lo"], tabs["bias"], *sv["a_views"][g], douts[g], dlses[g],
                                     g, f"a_bwd_{g}_{l}")
        dqs.append(dq)
        dks.append(dk)
        dvs.append(dv)
        dbs.append(db)
    heads = lambda t: [t[i].astype(BF16) for i in range(SB_HEADS)]
    a_cols = [t[g] for g in range(3) for t in (dqs, dks, dvs)]
    dlo = jnp.concatenate(a_cols + [d_sq] + heads(d_sk) + heads(d_sv), axis=1)
    dhi = jnp.concatenate([d_rq, d_rk, d_rv, d_rg, dgi], axis=1)
    dh = _matmul(dlo, wl["win_lo"], "nn", f"in_proj_lo_dx_{l}")
    dh = _matmul(dhi, wl["win_hi"], "nn", f"in_proj_hi_dx_{l}", add=dh)
    d_win = jnp.concatenate([_group_major(_matmul(dlo, sv["h"], "tn", f"in_proj_lo_dw_{l}")),
                             _matmul(dhi, sv["h"], "tn", f"in_proj_hi_dw_{l}")], axis=0)
    dx_in, d_nmix = _rms_bwd(sv["x_in"], wl["norm_mix_w"], dh, dx_mid, f"rms_mix_bwd_{l}")
    chunks = lambda t: t.reshape(N_DEV, -1, D_MODEL)
    big = jnp.concatenate([chunks(d_win), chunks(d_wup), chunks(d_wgate), chunks(d_wout), chunks(d_wdown),
                           chunks(d_wpa), chunks(d_wpb), chunks(d_wpc)], axis=1)
    small = dict(norm_mix_w=d_nmix[0], ret_norm_w=d_rnw[0], norm_ffn_w=d_nffn[0], conv_b=d_cb[0],
                 b_gate=d_bg, conv_w=d_cw, dbias=jnp.concatenate(dbs, axis=0))
    return dx_in, big, small


def kernel(x, rel_bias, norm_mix_w, w_in, b_gate, ret_norm_w, w_proj_a, w_proj_b, w_proj_c, w_out, norm_ffn_w, w_up, w_gate, conv_w, conv_b, w_down, final_norm_w, loss_target, m_rel_bias, m_norm_mix_w, m_w_in, m_b_gate, m_ret_norm_w, m_w_proj_a, m_w_proj_b, m_w_proj_c, m_w_out, m_norm_ffn_w, m_w_up, m_w_gate, m_conv_w, m_conv_b, m_w_down, m_final_norm_w, v_rel_bias, v_norm_mix_w, v_w_in, v_b_gate, v_ret_norm_w, v_w_proj_a, v_w_proj_b, v_w_proj_c, v_w_out, v_norm_ffn_w, v_w_up, v_w_gate, v_conv_w, v_conv_b, v_w_down, v_final_norm_w):
    T = x.shape[1]
    me = 4 * lax.axis_index("x") + 2 * lax.axis_index("y") + lax.axis_index("c")

    rows = []
    for l in range(DEPTH):
        rows += [w_in[l].T, w_up[l].T, w_gate[l].T, w_out[l], w_down[l],
                 w_proj_a[l].T.reshape(-1, D_MODEL), w_proj_b[l].T.reshape(-1, D_MODEL),
                 w_proj_c[l].T.reshape(-1, D_MODEL)]
    wall = _exchange(jnp.concatenate(rows, axis=0).astype(BF16), "gather_weights", scatter=False)
    n_bg, n_cw = b_gate.size, conv_w.size
    tiny = jnp.concatenate([b_gate.reshape(-1), conv_w.reshape(-1), jnp.zeros((8 * D_MODEL - n_bg - n_cw,), F32)])
    tall = _exchange(tiny.reshape(8, D_MODEL), "gather_small_weights", scatter=False).reshape(N_DEV, -1)
    spread = lambda t, w: t.reshape(N_DEV, DEPTH, 3, w).transpose(1, 2, 0, 3).reshape(DEPTH, 3, N_DEV * w)
    b_gate_full = spread(tall[:, :n_bg], b_gate.shape[-1])
    conv_w_full = spread(tall[:, n_bg:n_bg + n_cw], conv_w.shape[-1])

    def layer_weights(l):
        base = l * PACK_LAYER
        offs = np.cumsum((0,) + PACK_ROWS)
        seg = lambda j: wall[:, base + offs[j]:base + offs[j + 1], :]
        win = seg(0).reshape(-1, D_MODEL)
        unpack = lambda j, k: seg(j).reshape(N_DEV, D_MODEL // N_DEV, k).reshape(D_MODEL, k)
        return dict(win_lo=_group_major(win[:LO_WIDTH]), win_hi=win[LO_WIDTH:], wup=seg(1).reshape(-1, D_MODEL),
                    wgate=seg(2).reshape(-1, D_MODEL), wout=seg(3).reshape(-1, D_MODEL),
                    wdown=seg(4).reshape(-1, D_MODEL), wpa=unpack(5, 256), wpb=unpack(6, 512), wpc=unpack(7, 512),
                    norm_mix_w=norm_mix_w[l], norm_ffn_w=norm_ffn_w[l], ret_norm_w=ret_norm_w[l],
                    b_gate=b_gate_full[l], conv_w=conv_w_full[l], conv_b=conv_b[l])

    buckets = _a_buckets()
    ret_tabs, decay = _ret_tables(T)
    tabs = dict(bias=_a_bias_tables(rel_bias, buckets), ret=ret_tabs, decay=decay)

    xs = x[0]
    saved, wls = [], []
    for l in range(DEPTH):
        wls.append(layer_weights(l))
        xs, sv = _layer_forward(xs, wls[l], tabs, l)
        saved.append(sv)
    loss_tile, dx, d_final = _loss_head(xs, final_norm_w, loss_target[0], "loss_head")
    loss = lax.psum(loss_tile[0, 0], ("x", "y", "c"))

    bigs, smalls = [None] * DEPTH, [None] * DEPTH
    for l in reversed(range(DEPTH)):
        dx, bigs[l], smalls[l] = _layer_backward(dx, saved[l], wls[l], tabs, l)
    grad_x = dx[None]

    parts = _exchange(jnp.concatenate(bigs, axis=1), "scatter_grads", scatter=True)
    mine = _sum_devices(parts, "sum_grads")
    dbias = smalls[0]["dbias"] + smalls[1]["dbias"] + smalls[2]["dbias"] + smalls[3]["dbias"]
    small_vals = dict(rel_bias=_a_bias_grad(dbias, buckets), final_norm_w=d_final[0])
    for name in ("norm_mix_w", "ret_norm_w", "norm_ffn_w", "conv_b", "b_gate", "conv_w"):
        small_vals[name] = jnp.stack([smalls[l][name] for l in range(DEPTH)])
    flat = jnp.concatenate([small_vals[n].reshape(-1) for n, _ in SMALL_SIZES])
    flat = jnp.concatenate([flat, jnp.zeros((SMALL_ROWS * D_MODEL - flat.shape[0],), F32)])
    sparts = _exchange(flat.reshape(SMALL_ROWS, D_MODEL), "gather_small_grads", scatter=False)
    ssum = _sum_devices(sparts, "sum_small_grads").reshape(-1)

    grads = {}
    off = 0
    for name, size in SMALL_SIZES:
        grads[name] = ssum[off:off + size]
        off += size
    grads["rel_bias"] = grads["rel_bias"].reshape(REL_BUCKETS, 12)
    for name in ("norm_mix_w", "norm_ffn_w"):
        grads[name] = grads[name].reshape(DEPTH, D_MODEL)
    grads["ret_norm_w"] = grads["ret_norm_w"].reshape(DEPTH, -1)
    grads["conv_b"] = grads["conv_b"].reshape(DEPTH, D_FF)
    bw, cwid = b_gate.shape[-1], conv_w.shape[-1]
    grads["b_gate"] = lax.dynamic_slice_in_dim(grads["b_gate"].reshape(DEPTH, 3, -1), me * bw, bw, axis=2)
    grads["conv_w"] = lax.dynamic_slice_in_dim(grads["conv_w"].reshape(DEPTH, 3, -1), me * cwid, cwid, axis=2)

    offs = np.cumsum((0,) + PACK_ROWS)
    per_layer = mine.reshape(DEPTH, PACK_LAYER, D_MODEL)
    seg = lambda j: per_layer[:, offs[j]:offs[j + 1], :]
    back = lambda j, k: seg(j).reshape(DEPTH, D_MODEL // N_DEV, k).transpose(0, 2, 1)
    grads["w_in"] = seg(0).transpose(0, 2, 1)
    grads["w_up"] = seg(1).transpose(0, 2, 1)
    grads["w_gate"] = seg(2).transpose(0, 2, 1)
    grads["w_out"] = seg(3)
    grads["w_down"] = seg(4)
    grads["w_proj_a"] = back(5, 256)
    grads["w_proj_b"] = back(6, 512)
    grads["w_proj_c"] = back(7, 512)

    order = ["rel_bias", "norm_mix_w", "w_in", "b_gate", "ret_norm_w", "w_proj_a", "w_proj_b", "w_proj_c",
             "w_out", "norm_ffn_w", "w_up", "w_gate", "conv_w", "conv_b", "w_down", "final_norm_w"]
    ws = dict(rel_bias=rel_bias, norm_mix_w=norm_mix_w, w_in=w_in, b_gate=b_gate, ret_norm_w=ret_norm_w,
              w_proj_a=w_proj_a, w_proj_b=w_proj_b, w_proj_c=w_proj_c, w_out=w_out, norm_ffn_w=norm_ffn_w,
              w_up=w_up, w_gate=w_gate, conv_w=conv_w, conv_b=conv_b, w_down=w_down, final_norm_w=final_norm_w)
    ms = dict(rel_bias=m_rel_bias, norm_mix_w=m_norm_mix_w, w_in=m_w_in, b_gate=m_b_gate, ret_norm_w=m_ret_norm_w,
              w_proj_a=m_w_proj_a, w_proj_b=m_w_proj_b, w_proj_c=m_w_proj_c, w_out=m_w_out,
              norm_ffn_w=m_norm_ffn_w, w_up=m_w_up, w_gate=m_w_gate, conv_w=m_conv_w, conv_b=m_conv_b,
              w_down=m_w_down, final_norm_w=m_final_norm_w)
    vs = dict(rel_bias=v_rel_bias, norm_mix_w=v_norm_mix_w, w_in=v_w_in, b_gate=v_b_gate, ret_norm_w=v_ret_norm_w,
              w_proj_a=v_w_proj_a, w_proj_b=v_w_proj_b, w_proj_c=v_w_proj_c, w_out=v_w_out,
              norm_ffn_w=v_norm_ffn_w, w_up=v_w_up, w_gate=v_w_gate, conv_w=v_conv_w, conv_b=v_conv_b,
              w_down=v_w_down, final_norm_w=v_final_norm_w)
    deltas, new_m, new_v = [], [], []
    for name in order:
        g = grads[name].reshape(ws[name].shape)
        grads[name] = g
        d, nm, nv = _adamw(ws[name], g, ms[name], vs[name], f"adamw_{name}")
        deltas.append(d)
        new_m.append(nm)
        new_v.append(nv)
    return (loss, grad_x, *[grads[n] for n in order], *deltas, *new_m, *new_v)
```

```python
import functools

import numpy as np
import jax
import jax.numpy as jnp
from jax import lax
from jax.experimental import pallas as pl
from jax.experimental.pallas import tpu as pltpu

F32 = jnp.float32
BF16 = jnp.bfloat16

D_MODEL = 1024
DEPTH = 4
N_DEV = 8
BLOCK = 128
DIL_GROUPS = ((128, 1), (512, 4), (2048, 16))
A_HEADS_PER_GROUP = 4
HEAD_DIM = 64
A_GROUP_WIDTH = A_HEADS_PER_GROUP * HEAD_DIM
SB_HEADS = 4
SB_HEAD_DIM = 128
RET_HEADS = 4
RET_DK = 64
RET_DV = 128
ROPE_BASE = 10000.0
REL_BUCKETS = 32
REL_MAX_DIST = 2048
D_FF = 2816
EPS = 1e-6
LO_WIDTH = 3840
HI_WIDTH = 4608
A_SCALE = HEAD_DIM ** -0.5
SB_SCALE = SB_HEAD_DIM ** -0.5
RET_SCALE = RET_DK ** -0.5
NEG = -1e30

ADAM_LR = 0.001
ADAM_B1 = 0.9
ADAM_B2 = 0.999
ADAM_EPS = 1e-08
ADAM_WD = 0.01
ADAM_STEP = 10

VMEM_LIMIT_V7X = 56 * 1024 * 1024

_NT = (((1,), (1,)), ((), ()))
_NN = (((1,), (0,)), ((), ()))
_TN = (((0,), (0,)), ((), ()))


def _dot(a, b, dims):
    return lax.dot_general(a, b, dims, preferred_element_type=F32)


def _tile(n, cands):
    for c in cands:
        if n % c == 0:
            return c
    return n


def _params(*sem):
    return pltpu.CompilerParams(dimension_semantics=sem, vmem_limit_bytes=VMEM_LIMIT_V7X)


def _matmul(a, b, mode, name, out_dtype=None, add=None):
    if out_dtype is None:
        out_dtype = BF16 if mode == "tn" else F32
    if mode == "tn":
        (K, M), N = a.shape, b.shape[1]
    elif mode == "nn":
        (M, K), N = a.shape, b.shape[1]
    else:
        (M, K), N = a.shape, b.shape[0]
    wide = (1536, 1408, 1280, 1152, 1024, 768, 512, 384, 256, 128)
    if mode == "tn":
        tm, tn, tk = _tile(M, wide), _tile(N, wide[4:]), _tile(K, (1024, 512, 256, 128))
    else:
        tm, tn, tk = _tile(M, (512, 256, 128)), _tile(N, wide), K
    nk = K // tk
    if mode == "tn":
        grid, ij = (M // tm, N // tn, nk), (lambda g0, g1: (g0, g1))
    else:
        grid, ij = (N // tn, M // tm, nk), (lambda g0, g1: (g1, g0))
    at = lambda f: (lambda g0, g1, k: f(*ij(g0, g1), k))
    if mode == "tn":
        a_spec = pl.BlockSpec((tk, tm), at(lambda i, j, k: (k, i)))
    else:
        a_spec = pl.BlockSpec((tm, tk), at(lambda i, j, k: (i, k)))
    if mode == "nt":
        b_spec = pl.BlockSpec((tn, tk), at(lambda i, j, k: (j, k)))
    else:
        b_spec = pl.BlockSpec((tk, tn), at(lambda i, j, k: (k, j)))
    dims = {"nt": _NT, "nn": _NN, "tn": _TN}[mode]
    o_spec = pl.BlockSpec((tm, tn), at(lambda i, j, k: (i, j)))
    has_add = add is not None

    def finish(refs, r):
        if has_add:
            r = r + refs[2][...]
        refs[-1 if nk == 1 else -2][...] = r.astype(out_dtype)

    def body(*refs):
        part = _dot(refs[0][...].astype(BF16), refs[1][...].astype(BF16), dims)
        if nk == 1:
            finish(refs, part)
            return
        acc = refs[-1]
        k = pl.program_id(2)

        @pl.when(k == 0)
        def _():
            acc[...] = part

        @pl.when(k > 0)
        def _():
            acc[...] += part

        @pl.when(k == nk - 1)
        def _():
            finish(refs, acc[...])

    ins = [a, b] + ([add] if has_add else [])
    specs = [a_spec, b_spec] + ([o_spec] if has_add else [])
    return pl.pallas_call(
        body, name=name, grid=grid,
        in_specs=specs, out_specs=o_spec,
        out_shape=jax.ShapeDtypeStruct((M, N), out_dtype),
        scratch_shapes=[] if nk == 1 else [pltpu.VMEM((tm, tn), F32)],
        compiler_params=_params("parallel", "parallel", "arbitrary"),
    )(*ins)


def _rms(x, w):
    return x * lax.rsqrt(jnp.mean(x * x, axis=-1, keepdims=True) + EPS) * w


def _rms_fwd(x, w, name):
    T = x.shape[0]
    tm = _tile(T, (512, 256, 128))

    def body(x_ref, w_ref, o_ref):
        o_ref[...] = _rms(x_ref[...], w_ref[...]).astype(BF16)

    return pl.pallas_call(
        body, name=name, grid=(T // tm,),
        in_specs=[pl.BlockSpec((tm, D_MODEL), lambda i: (i, 0)), pl.BlockSpec((1, D_MODEL), lambda i: (0, 0))],
        out_specs=pl.BlockSpec((tm, D_MODEL), lambda i: (i, 0)),
        out_shape=jax.ShapeDtypeStruct((T, D_MODEL), BF16),
        compiler_params=_params("parallel"),
    )(x, w.reshape(1, D_MODEL))


def _rms_bwd(x, w, dh, res, name):
    T = x.shape[0]
    tm = _tile(T, (512, 256, 128))

    def body(x_ref, w_ref, dh_ref, res_ref, dx_ref, dw_ref):
        _, vjp = jax.vjp(_rms, x_ref[...], w_ref[...])
        dx, dw = vjp(dh_ref[...])
        dx_ref[...] = dx + res_ref[...]

        @pl.when(pl.program_id(0) == 0)
        def _():
            dw_ref[...] = jnp.zeros_like(dw_ref)

        dw_ref[...] += dw

    row = pl.BlockSpec((tm, D_MODEL), lambda i: (i, 0))
    vec = pl.BlockSpec((1, D_MODEL), lambda i: (0, 0))
    return pl.pallas_call(
        body, name=name, grid=(T // tm,),
        in_specs=[row, vec, row, row], out_specs=[row, vec],
        out_shape=[jax.ShapeDtypeStruct((T, D_MODEL), F32), jax.ShapeDtypeStruct((1, D_MODEL), F32)],
        compiler_params=_params("arbitrary"),
    )(x, w.reshape(1, D_MODEL), dh, res)


def _loss_head(x, w, tgt, name):
    T = x.shape[0]
    tm = _tile(T, (512, 256, 128))

    def body(x_ref, w_ref, t_ref, l_ref, dx_ref, dw_ref):
        y, vjp = jax.vjp(_rms, x_ref[...], w_ref[...])
        e = y - t_ref[...]
        dx, dw = vjp(e * (1.0 / D_MODEL))
        dx_ref[...] = dx

        @pl.when(pl.program_id(0) == 0)
        def _():
            dw_ref[...] = jnp.zeros_like(dw_ref)
            l_ref[...] = jnp.zeros_like(l_ref)

        dw_ref[...] += dw
        l_ref[...] += 0.5 * jnp.sum(jnp.mean(e * e, axis=-1, keepdims=True))

    row = pl.BlockSpec((tm, D_MODEL), lambda i: (i, 0))
    vec = pl.BlockSpec((1, D_MODEL), lambda i: (0, 0))
    return pl.pallas_call(
        body, name=name, grid=(T // tm,),
        in_specs=[row, vec, row],
        out_specs=[pl.BlockSpec((8, 128), lambda i: (0, 0)), row, vec],
        out_shape=[jax.ShapeDtypeStruct((8, 128), F32), jax.ShapeDtypeStruct((T, D_MODEL), F32),
                   jax.ShapeDtypeStruct((1, D_MODEL), F32)],
        compiler_params=_params("arbitrary"),
    )(x, w.reshape(1, D_MODEL), tgt)


def _t5_bucket(dist):
    max_exact = REL_BUCKETS // 2
    n = np.maximum(dist, 0)
    large = max_exact + (np.log(np.maximum(n, 1) / max_exact) / np.log(REL_MAX_DIST / max_exact)
                         * (REL_BUCKETS - max_exact)).astype(np.int64)
    large = np.minimum(large, REL_BUCKETS - 1)
    return np.where(n < max_exact, n, large).astype(np.int32)


def _a_buckets():
    steps = np.arange(BLOCK)[:, None] + BLOCK - np.arange(2 * BLOCK)[None, :]
    out = []
    for window, dilation in DIL_GROUPS:
        in_band = (steps >= 0) & (steps <= window // dilation)
        out.append(np.where(in_band, _t5_bucket(steps * dilation), -1))
    return jnp.asarray(np.stack(out).astype(np.int32))


def _a_bias_tables(rel_bias, buckets):
    def body(rb_ref, bk_ref, o_ref):
        hh = pl.program_id(0)
        bk = bk_ref[0]
        acc = jnp.full((BLOCK, 2 * BLOCK), NEG, F32)
        for b in range(REL_BUCKETS):
            acc = jnp.where(bk == b, rb_ref[b, hh], acc)
        o_ref[0] = acc

    return pl.pallas_call(
        body, name="a_bias_tables", grid=(12,),
        in_specs=[pl.BlockSpec(memory_space=pltpu.SMEM),
                  pl.BlockSpec((1, BLOCK, 2 * BLOCK), lambda h: (h // 4, 0, 0))],
        out_specs=pl.BlockSpec((1, BLOCK, 2 * BLOCK), lambda h: (h, 0, 0)),
        out_shape=jax.ShapeDtypeStruct((12, BLOCK, 2 * BLOCK), F32),
        compiler_params=_params("parallel"),
    )(rel_bias, buckets)


def _a_bias_grad(dbias, buckets):
    def body(db_ref, bk_ref, o_ref):
        bk = bk_ref[0]
        db = db_ref[0]
        lane = lax.broadcasted_iota(jnp.int32, (8, 128), 1)
        acc = jnp.zeros((8, 128), F32)
        for b in range(REL_BUCKETS):
            acc = jnp.where(lane == b, jnp.sum(jnp.where(bk == b, db, 0.0)), acc)
        o_ref[0] = acc

    out = pl.pallas_call(
        body, name="a_bias_grad", grid=(12,),
        in_specs=[pl.BlockSpec((1, BLOCK, 2 * BLOCK), lambda h: (h, 0, 0)),
                  pl.BlockSpec((1, BLOCK, 2 * BLOCK), lambda h: (h // 4, 0, 0))],
        out_specs=pl.BlockSpec((1, 8, 128), lambda h: (h, 0, 0)),
        out_shape=jax.ShapeDtypeStruct((12, 8, 128), F32),
        compiler_params=_params("parallel"),
    )(dbias, buckets)
    return out[:, 0, :REL_BUCKETS].T


def _a_logits(q_ref, kp_ref, kc_ref, b_ref, h, ok):
    sl = slice(h * HEAD_DIM, (h + 1) * HEAD_DIM)
    qh = q_ref[:, sl]
    s = jnp.concatenate([_dot(qh, kp_ref[:, sl], _NT), _dot(qh, kc_ref[:, sl], _NT)], axis=1)
    s = s * A_SCALE + b_ref[h]
    return jnp.where(ok, s, NEG)


def _group_major(w):
    n = 3 * 3 * A_GROUP_WIDTH
    head = w[:n].reshape(3, 3, A_GROUP_WIDTH, -1).transpose(1, 0, 2, 3).reshape(n, -1)
    return jnp.concatenate([head, w[n:]], axis=0)


def _a_view(proj_lo, g):
    if g == 0:
        return proj_lo, proj_lo.shape[1] // A_GROUP_WIDTH
    T, d, w = proj_lo.shape[0], DIL_GROUPS[g][1], 3 * A_GROUP_WIDTH
    return proj_lo[:, g * w:(g + 1) * w].reshape(T // d, d * w), 3


def _a_attn_fwd(proj_lo, bias_all, g, name):
    T = proj_lo.shape[0]
    d = DIL_GROUPS[g][1]
    L = T // d
    nb = L // BLOCK
    pv, nlo = _a_view(proj_lo, g)

    def body(q_ref, kc_ref, kp_ref, vc_ref, vp_ref, b_ref, o_ref, l_ref):
        n = pl.program_id(1)
        col = lax.broadcasted_iota(jnp.int32, (BLOCK, 2 * BLOCK), 1)
        ok = col >= jnp.where(n > 0, 0, BLOCK)
        heads = range(A_HEADS_PER_GROUP)
        sls = [slice(h * HEAD_DIM, (h + 1) * HEAD_DIM) for h in heads]
        s = [_a_logits(q_ref, kp_ref, kc_ref, b_ref, h, ok) for h in heads]
        m = [jnp.max(s[h], axis=-1, keepdims=True) for h in heads]
        p = [jnp.exp(s[h] - m[h]) for h in heads]
        l = [jnp.sum(p[h], axis=-1, keepdims=True) for h in heads]
        pb = [p[h].astype(BF16) for h in heads]
        o = [_dot(pb[h][:, :BLOCK], vp_ref[:, sls[h]], _NN) + _dot(pb[h][:, BLOCK:], vc_ref[:, sls[h]], _NN)
             for h in heads]
        for h in heads:
            o_ref[:, sls[h]] = o[h] / l[h]
            l_ref[:, sls[h]] = jnp.broadcast_to(m[h] + jnp.log(l[h]), (BLOCK, HEAD_DIM))

    blk = (BLOCK, A_GROUP_WIDTH)
    out_spec = pl.BlockSpec(blk, lambda r, n: (n, r))
    out, lse = pl.pallas_call(
        body, name=name, grid=(d, nb),
        in_specs=[pl.BlockSpec(blk, lambda r, n: (n, r * nlo)),
                  pl.BlockSpec(blk, lambda r, n: (n, r * nlo + 1)),
                  pl.BlockSpec(blk, lambda r, n: (jnp.maximum(n - 1, 0), r * nlo + 1)),
                  pl.BlockSpec(blk, lambda r, n: (n, r * nlo + 2)),
                  pl.BlockSpec(blk, lambda r, n: (jnp.maximum(n - 1, 0), r * nlo + 2)),
                  pl.BlockSpec((4, BLOCK, 2 * BLOCK), lambda r, n: (g, 0, 0))],
        out_specs=[out_spec, out_spec],
        out_shape=[jax.ShapeDtypeStruct((L, d * A_GROUP_WIDTH), F32)] * 2,
        compiler_params=_params("parallel", "arbitrary"),
    )(pv, pv, pv, pv, pv, bias_all)
    return out.reshape(T, A_GROUP_WIDTH), lse.reshape(T, A_GROUP_WIDTH), (out, lse)


def _a_attn_bwd(proj_lo, bias_all, out, lse, dout, dlse, g, name):
    T = proj_lo.shape[0]
    d = DIL_GROUPS[g][1]
    L = T // d
    nb = L // BLOCK
    pv, nlo = _a_view(proj_lo, g)
    view = lambda t: t.reshape(L, d * A_GROUP_WIDTH)

    def body(q_ref, kc_ref, kp_ref, vc_ref, vp_ref, b_ref, o_ref, l_ref, do_ref, dl_ref,
             dq_ref, dk_ref, dv_ref, db_ref, ck, cv):
        r = pl.program_id(0)
        n = pl.program_id(1)

        @pl.when((r == 0) & (n == 0))
        def _():
            db_ref[...] = jnp.zeros_like(db_ref)

        @pl.when(n == 0)
        def _():
            ck[...] = jnp.zeros_like(ck)
            cv[...] = jnp.zeros_like(cv)

        @pl.when(n < nb)
        def _():
            col = lax.broadcasted_iota(jnp.int32, (BLOCK, 2 * BLOCK), 1)
            ok = col >= jnp.where(n > 0, 0, BLOCK)
            heads = range(A_HEADS_PER_GROUP)
            sls = [slice(h * HEAD_DIM, (h + 1) * HEAD_DIM) for h in heads]
            s = [_a_logits(q_ref, kp_ref, kc_ref, b_ref, h, ok) for h in heads]
            p = [jnp.exp(s[h] - l_ref[:, h * HEAD_DIM:h * HEAD_DIM + 1]) for h in heads]
            do = [do_ref[:, sls[h]] for h in heads]
            dob = [do[h].astype(BF16) for h in heads]
            shift = [jnp.sum(dl_ref[:, sls[h]] - do[h] * o_ref[:, sls[h]], axis=-1, keepdims=True) for h in heads]
            dp = [jnp.concatenate([_dot(dob[h], vp_ref[:, sls[h]], _NT), _dot(dob[h], vc_ref[:, sls[h]], _NT)],
                                  axis=1) for h in heads]
            ds = [p[h] * (dp[h] + shift[h]) for h in heads]
            for h in heads:
                db_ref[h] += ds[h]
            dsb = [(ds[h] * A_SCALE).astype(BF16) for h in heads]
            pb = [p[h].astype(BF16) for h in heads]
            qh = [q_ref[:, sls[h]] for h in heads]
            dq = [_dot(dsb[h][:, :BLOCK], kp_ref[:, sls[h]], _NN) + _dot(dsb[h][:, BLOCK:], kc_ref[:, sls[h]], _NN)
                  for h in heads]
            dkp = [_dot(dsb[h][:, :BLOCK], qh[h], _TN) for h in heads]
            dvp = [_dot(pb[h][:, :BLOCK], dob[h], _TN) for h in heads]
            dkc = [_dot(dsb[h][:, BLOCK:], qh[h], _TN) for h in heads]
            dvc = [_dot(pb[h][:, BLOCK:], dob[h], _TN) for h in heads]
            for h in heads:
                dq_ref[:, sls[h]] = dq[h].astype(BF16)
                dk_ref[:, sls[h]] = (ck[:, sls[h]] + dkp[h]).astype(BF16)
                dv_ref[:, sls[h]] = (cv[:, sls[h]] + dvp[h]).astype(BF16)
                ck[:, sls[h]] = dkc[h]
                cv[:, sls[h]] = dvc[h]

        @pl.when(n == nb)
        def _():
            dk_ref[...] = ck[...].astype(BF16)
            dv_ref[...] = cv[...].astype(BF16)

    blk = (BLOCK, A_GROUP_WIDTH)
    nq = lambda n: jnp.minimum(n, nb - 1)
    prv = lambda n: jnp.maximum(jnp.minimum(n, nb - 1) - 1, 0)
    cur_o = pl.BlockSpec(blk, lambda r, n: (nq(n), r))
    lag_o = pl.BlockSpec(blk, lambda r, n: (jnp.maximum(n - 1, 0), r))
    dq, dk, dv, db = pl.pallas_call(
        body, name=name, grid=(d, nb + 1),
        in_specs=[pl.BlockSpec(blk, lambda r, n: (nq(n), r * nlo)),
                  pl.BlockSpec(blk, lambda r, n: (nq(n), r * nlo + 1)),
                  pl.BlockSpec(blk, lambda r, n: (prv(n), r * nlo + 1)),
                  pl.BlockSpec(blk, lambda r, n: (nq(n), r * nlo + 2)),
                  pl.BlockSpec(blk, lambda r, n: (prv(n), r * nlo + 2)),
                  pl.BlockSpec((4, BLOCK, 2 * BLOCK), lambda r, n: (g, 0, 0)),
                  cur_o, cur_o, cur_o, cur_o],
        out_specs=[cur_o, lag_o, lag_o, pl.BlockSpec((4, BLOCK, 2 * BLOCK), lambda r, n: (0, 0, 0))],
        out_shape=[jax.ShapeDtypeStruct((L, d * A_GROUP_WIDTH), BF16)] * 3
        + [jax.ShapeDtypeStruct((4, BLOCK, 2 * BLOCK), F32)],
        scratch_shapes=[pltpu.VMEM(blk, F32), pltpu.VMEM(blk, F32)],
        compiler_params=_params("arbitrary", "arbitrary"),
    )(pv, pv, pv, pv, pv, bias_all, out, lse, view(dout), view(dlse))
    return dq.reshape(T, -1), dk.reshape(T, -1), dv.reshape(T, -1), db


def _a_merge(outs, lses):
    m = jnp.maximum(jnp.maximum(lses[0], lses[1]), lses[2])
    e = [jnp.exp(l - m) for l in lses]
    inv = 1.0 / (e[0] + e[1] + e[2])
    return (e[0] * outs[0] + e[1] * outs[1] + e[2] * outs[2]) * inv


def _a_merge_fwd(outs, lses, name):
    T = outs[0].shape[0]
    tm = _tile(T, (512, 256, 128))

    def body(o0, o1, o2, l0, l1, l2, y_ref):
        y_ref[...] = _a_merge([o0[...], o1[...], o2[...]], [l0[...], l1[...], l2[...]]).astype(BF16)

    row = pl.BlockSpec((tm, A_GROUP_WIDTH), lambda i: (i, 0))
    return pl.pallas_call(
        body, name=name, grid=(T // tm,), in_specs=[row] * 6, out_specs=row,
        out_shape=jax.ShapeDtypeStruct((T, A_GROUP_WIDTH), BF16),
        compiler_params=_params("parallel"),
    )(*outs, *lses)


def _a_merge_bwd(outs, lses, dy, name):
    T = outs[0].shape[0]
    tm = _tile(T, (512, 256, 128))

    def body(o0, o1, o2, l0, l1, l2, dy_ref, *outs_ref):
        _, vjp = jax.vjp(_a_merge, [o0[...], o1[...], o2[...]], [l0[...], l1[...], l2[...]])
        do, dl = vjp(dy_ref[...])
        for ref, val in zip(outs_ref, list(do) + list(dl)):
            ref[...] = val

    row = pl.BlockSpec((tm, A_GROUP_WIDTH), lambda i: (i, 0))
    res = pl.pallas_call(
        body, name=name, grid=(T // tm,), in_specs=[row] * 7, out_specs=[row] * 6,
        out_shape=[jax.ShapeDtypeStruct((T, A_GROUP_WIDTH), F32)] * 6,
        compiler_params=_params("parallel"),
    )(*outs, *lses, dy)
    return res[:3], res[3:]


SB_TQ = 512
SB_KS = 256
SB_UNROLL = 4
SB_Q_COL, SB_K_COL, SB_V_COL = 18, 22, 26


def _sb_tri():
    return jnp.asarray(np.arange(SB_KS)[:, None] > np.arange(SB_KS)[None, :], BF16)


def _logit_parts(m):
    z = m * SB_SCALE
    e = jnp.exp2(jnp.abs(m) * (-SB_SCALE * 1.4426950408889634))
    return z, jnp.maximum(z, 0.0) + jnp.log(1.0 + e)


def _sb_sizes(T):
    nsub = SB_TQ // SB_KS
    assert T % SB_TQ == 0 and SB_TQ % SB_KS == 0 and SB_UNROLL in (nsub, 2 * nsub)
    return SB_TQ, SB_KS, nsub, SB_UNROLL


def _wide(c, width):
    return jnp.concatenate([c] * (width // BLOCK), axis=1)


def _sb_fwd(proj_lo, name):
    T = proj_lo.shape[0]
    TQ, KS, nsub, unroll = _sb_sizes(T)
    nq = T // TQ

    def body(q_ref, k_ref, v_ref, tri_ref, o_ref, c_ref, acc, cs):
        i = pl.program_id(1)
        q = q_ref[...]
        tri = tri_ref[...]
        row = i * TQ + lax.broadcasted_iota(jnp.int32, (TQ, KS), 0)
        col = lax.broadcasted_iota(jnp.int32, (TQ, KS), 1)

        def sub(n, masked, c, o):
            off = pl.multiple_of(n * KS, KS)
            z, nl = _logit_parts(_dot(q, k_ref[pl.ds(off, KS), :], _NT))
            if masked:
                valid = row > n * KS + col
                nl = jnp.where(valid, nl, 0.0)
            w = jnp.exp(z - nl - _dot(nl.astype(BF16), tri, _NN) - _wide(c, KS))
            if masked:
                w = jnp.where(valid, w, 0.0)
            o = o + _dot(w.astype(BF16), v_ref[pl.ds(off, KS), :], _NN)
            return c + jnp.sum(nl, axis=1, keepdims=True), o

        c = jnp.zeros((TQ, BLOCK), F32)
        o = jnp.zeros((TQ, BLOCK), F32)
        for j in reversed(range(nsub)):
            c, o = sub(i * nsub + j, True, c, o)
        cs[...] = c
        acc[...] = o

        def group(top, count):
            c = cs[...]
            o = jnp.zeros((TQ, BLOCK), F32)
            for u in range(count):
                c, o = sub(top - u, False, c, o)
            cs[...] = c
            acc[...] += o

        below = i * nsub
        odd = below % unroll

        @pl.when(odd != 0)
        def _():
            group(below - 1, nsub)

        def step(m, carry):
            group(below - odd - 1 - m * unroll, unroll)
            return carry

        lax.fori_loop(0, below // unroll, step, 0)
        o_ref[...] = acc[...]
        c_ref[...] = cs[...]

    qo = pl.BlockSpec((TQ, BLOCK), lambda h, i: (i, h))
    return pl.pallas_call(
        body, name=name, grid=(SB_HEADS, nq),
        in_specs=[pl.BlockSpec((TQ, BLOCK), lambda h, i: (i, SB_Q_COL + h)),
                  pl.BlockSpec((T, BLOCK), lambda h, i: (0, SB_K_COL + h)),
                  pl.BlockSpec((T, BLOCK), lambda h, i: (0, SB_V_COL + h)),
                  pl.BlockSpec((SB_KS, SB_KS), lambda h, i: (0, 0))],
        out_specs=[qo, qo],
        out_shape=[jax.ShapeDtypeStruct((T, SB_HEADS * BLOCK), F32)] * 2,
        scratch_shapes=[pltpu.VMEM((TQ, BLOCK), F32), pltpu.VMEM((TQ, BLOCK), F32)],
        compiler_params=_params("parallel", "arbitrary"),
    )(proj_lo, proj_lo, proj_lo, _sb_tri())


def _sb_bwd(proj_lo, ctot, dy, name):
    T = proj_lo.shape[0]
    TQ, KS, nsub, unroll = _sb_sizes(T)
    nq = T // TQ

    def body(q_ref, k_ref, v_ref, c_ref, do_ref, tri_ref, dq_ref, dk_hbm, dv_hbm,
             dka, dva, dqa, cps, pgs, sem):
        h = pl.program_id(0)
        i = pl.program_id(1)

        @pl.when(i == 0)
        def _():
            dka[...] = jnp.zeros_like(dka)
            dva[...] = jnp.zeros_like(dva)

        q = q_ref[...]
        tri = tri_ref[...]
        dob = do_ref[...].astype(BF16)
        ctot = c_ref[...]
        row = i * TQ + lax.broadcasted_iota(jnp.int32, (TQ, KS), 0)
        col = lax.broadcasted_iota(jnp.int32, (TQ, KS), 1)

        def sub(n, masked, cp, pg, dq):
            off = pl.multiple_of(n * KS, KS)
            kb = k_ref[pl.ds(off, KS), :]
            z, nl = _logit_parts(_dot(q, kb, _NT))
            if masked:
                valid = row > n * KS + col
                nl = jnp.where(valid, nl, 0.0)
            cp = cp + jnp.sum(nl, axis=1, keepdims=True)
            lb = z - nl
            w = jnp.exp(lb - _dot(nl.astype(BF16), tri, _NN) - _wide(ctot - cp, KS))
            if masked:
                w = jnp.where(valid, w, 0.0)
            g = w * _dot(dob, v_ref[pl.ds(off, KS), :], _NT)
            dz = g - jnp.exp(lb) * (g + _dot(g.astype(BF16), tri, _NT) + _wide(pg, KS))
            if masked:
                dz = jnp.where(valid, dz, 0.0)
            pg = pg + jnp.sum(g, axis=1, keepdims=True)
            dzb = dz.astype(BF16)
            dka[pl.ds(off, KS), :] += _dot(dzb, q, _TN)
            dva[pl.ds(off, KS), :] += _dot(w.astype(BF16), dob, _TN)
            return cp, pg, dq + _dot(dzb, kb, _NN)

        zero = jnp.zeros((TQ, BLOCK), F32)
        cps[...] = zero
        pgs[...] = zero
        dqa[...] = zero

        def group(first, count):
            cp, pg, dq = cps[...], pgs[...], zero
            for u in range(count):
                cp, pg, dq = sub(first + u, False, cp, pg, dq)
            cps[...] = cp
            pgs[...] = pg
            dqa[...] += dq

        below = i * nsub
        odd = below % unroll

        def step(m, carry):
            group(m * unroll, unroll)
            return carry

        lax.fori_loop(0, below // unroll, step, 0)

        @pl.when(odd != 0)
        def _():
            group(below - nsub, nsub)

        cp, pg, dq = cps[...], pgs[...], dqa[...]
        for j in range(nsub):
            cp, pg, dq = sub(i * nsub + j, True, cp, pg, dq)
        dq_ref[...] = (dq * SB_SCALE).astype(BF16)

        @pl.when(i == nq - 1)
        def _():
            def scale_rows(r, carry):
                rows = pl.ds(pl.multiple_of(r * TQ, TQ), TQ)
                dka[rows, :] = dka[rows, :] * SB_SCALE
                return carry

            lax.fori_loop(0, nq, scale_rows, 0)
            ck = pltpu.make_async_copy(dka, dk_hbm.at[h], sem.at[0])
            cv = pltpu.make_async_copy(dva, dv_hbm.at[h], sem.at[1])
            ck.start()
            cv.start()
            ck.wait()
            cv.wait()

    qo = pl.BlockSpec((TQ, BLOCK), lambda h, i: (i, h))
    return pl.pallas_call(
        body, name=name, grid=(SB_HEADS, nq),
        in_specs=[pl.BlockSpec((TQ, BLOCK), lambda h, i: (i, SB_Q_COL + h)),
                  pl.BlockSpec((T, BLOCK), lambda h, i: (0, SB_K_COL + h)),
                  pl.BlockSpec((T, BLOCK), lambda h, i: (0, SB_V_COL + h)),
                  qo, qo,
                  pl.BlockSpec((SB_KS, SB_KS), lambda h, i: (0, 0))],
        out_specs=[qo, pl.BlockSpec(memory_space=pl.ANY), pl.BlockSpec(memory_space=pl.ANY)],
        out_shape=[jax.ShapeDtypeStruct((T, SB_HEADS * BLOCK), BF16),
                   jax.ShapeDtypeStruct((SB_HEADS, T, BLOCK), F32),
                   jax.ShapeDtypeStruct((SB_HEADS, T, BLOCK), F32)],
        scratch_shapes=[pltpu.VMEM((T, BLOCK), F32), pltpu.VMEM((T, BLOCK), F32),
                        pltpu.VMEM((TQ, BLOCK), F32), pltpu.VMEM((TQ, BLOCK), F32),
                        pltpu.VMEM((TQ, BLOCK), F32), pltpu.SemaphoreType.DMA((2,))],
        compiler_params=_params("arbitrary", "arbitrary"),
    )(proj_lo, proj_lo, proj_lo, ctot, dy, _sb_tri())


def _ret_tables(T):
    half = RET_DK // 2
    inv = ROPE_BASE ** (-jnp.arange(half, dtype=F32) / half)
    ang = jnp.arange(T, dtype=jnp.int32).astype(F32)[:, None] * inv[None, :]
    cos, sin = jnp.cos(ang), jnp.sin(ang)
    cos_t = jnp.tile(cos, (1, 2 * RET_HEADS))
    sin_t = jnp.tile(jnp.concatenate([-sin, sin], axis=1), (1, RET_HEADS))
    lg = np.log1p(-np.exp2(-5.0 - np.arange(RET_HEADS)))
    n = np.arange(BLOCK)
    diff = n[:, None] - n[None, :]
    dmat = np.where(diff >= 0, np.exp(np.minimum(diff, BLOCK)[None] * lg[:, None, None]), 0.0)
    zeta = np.repeat(np.exp((BLOCK - 1 - n)[:, None] * lg[None, :]), RET_DK, axis=1)
    xi = np.repeat(np.exp((n + 1)[:, None] * lg[None, :]), RET_DK, axis=1)
    chunk_decay = [float(v) for v in np.exp(BLOCK * lg)]
    return (cos_t, sin_t, jnp.asarray(dmat, F32), jnp.asarray(xi, F32), jnp.asarray(zeta, F32)), chunk_decay


def _swap_halves(x):
    lane = lax.broadcasted_iota(jnp.int32, x.shape, 1)
    lower = (lane % RET_DK) < (RET_DK // 2)
    w = x.shape[1]
    return jnp.where(lower, pltpu.roll(x, w - RET_DK // 2, 1), pltpu.roll(x, RET_DK // 2, 1))


def _gn_gate(o, rg, w):
    mu = jnp.mean(o, axis=-1, keepdims=True)
    xc = o - mu
    var = jnp.mean(xc * xc, axis=-1, keepdims=True)
    return (rg * jax.nn.sigmoid(rg)) * (xc * lax.rsqrt(var + EPS) * w)


def _ret_fwd(proj_hi, ret_norm_w, tables, chunk_decay, name):
    T = proj_hi.shape[0]
    nc = T // BLOCK
    cos_t, sin_t, dmat, xi, zeta = tables

    def body(rq_ref, rk_ref, rv_ref, rg_ref, w_ref, cos_ref, sin_ref, d_ref, xi_ref, ze_ref,
             y_ref, o_ref, st_ref, rs):
        @pl.when(pl.program_id(0) == 0)
        def _():
            rs[...] = jnp.zeros_like(rs)

        cos, sin = cos_ref[...], sin_ref[...]
        rq, rk = rq_ref[...], rk_ref[...]
        q = rq * cos + _swap_halves(rq) * sin
        k = (rk * cos + _swap_halves(rk) * sin) * RET_SCALE
        qb, kb = q.astype(BF16), k.astype(BF16)
        qx, kz = (q * xi_ref[...]).astype(BF16), (k * ze_ref[...]).astype(BF16)
        heads = range(RET_HEADS)
        sl = [slice(h * RET_DK, (h + 1) * RET_DK) for h in heads]
        sv = [slice(h * RET_DV, (h + 1) * RET_DV) for h in heads]
        vb = [rv_ref[:, sv[h]].astype(BF16) for h in heads]
        r = [rs[h] for h in heads]
        intra = [(_dot(qb[:, sl[h]], kb[:, sl[h]], _NT) * d_ref[h]).astype(BF16) for h in heads]
        o = [_dot(intra[h], vb[h], _NN) + _dot(qx[:, sl[h]], r[h].astype(BF16), _NN) for h in heads]
        new_r = [r[h] * chunk_decay[h] + _dot(kz[:, sl[h]], vb[h], _TN) for h in heads]
        for h in heads:
            st_ref[0, h] = r[h]
            rs[h] = new_r[h]
            o_ref[:, sv[h]] = o[h]
            y_ref[:, sv[h]] = _gn_gate(o[h], rg_ref[:, sv[h]], w_ref[:, sv[h]]).astype(BF16)

    qk = (BLOCK, RET_HEADS * RET_DK)
    vv = (BLOCK, RET_HEADS * RET_DV)
    const = lambda shape: pl.BlockSpec(shape, lambda n: (0,) * len(shape))
    return pl.pallas_call(
        body, name=name, grid=(nc,),
        in_specs=[pl.BlockSpec(qk, lambda n: (n, 0)), pl.BlockSpec(qk, lambda n: (n, 1)),
                  pl.BlockSpec(vv, lambda n: (n, 1)), pl.BlockSpec(vv, lambda n: (n, 2)),
                  const((1, RET_HEADS * RET_DV)),
                  pl.BlockSpec(qk, lambda n: (n, 0)), pl.BlockSpec(qk, lambda n: (n, 0)),
                  const((RET_HEADS, BLOCK, BLOCK)), const(qk), const(qk)],
        out_specs=[pl.BlockSpec(vv, lambda n: (n, 0)), pl.BlockSpec(vv, lambda n: (n, 0)),
                   pl.BlockSpec((1, RET_HEADS, RET_DK, RET_DV), lambda n: (n, 0, 0, 0))],
        out_shape=[jax.ShapeDtypeStruct((T, RET_HEADS * RET_DV), BF16),
                   jax.ShapeDtypeStruct((T, RET_HEADS * RET_DV), F32),
                   jax.ShapeDtypeStruct((nc, RET_HEADS, RET_DK, RET_DV), F32)],
        scratch_shapes=[pltpu.VMEM((RET_HEADS, RET_DK, RET_DV), F32)],
        compiler_params=_params("arbitrary"),
    )(proj_hi, proj_hi, proj_hi, proj_hi, ret_norm_w.reshape(1, -1), cos_t, sin_t, dmat, xi, zeta)


def _ret_bwd(proj_hi, ret_norm_w, o_r, states, dy, tables, chunk_decay, name):
    T = proj_hi.shape[0]
    nc = T // BLOCK
    cos_t, sin_t, dmat, xi, zeta = tables

    def body(rq_ref, rk_ref, rv_ref, rg_ref, w_ref, cos_ref, sin_ref, d_ref, xi_ref, ze_ref,
             o_ref, st_ref, dy_ref, dq_ref, dk_ref, dv_ref, dg_ref, dw_ref, drs, dqs, dks):
        @pl.when(pl.program_id(0) == 0)
        def _():
            drs[...] = jnp.zeros_like(drs)
            dw_ref[...] = jnp.zeros_like(dw_ref)

        cos, sin = cos_ref[...], sin_ref[...]
        rq, rk = rq_ref[...], rk_ref[...]
        q = rq * cos + _swap_halves(rq) * sin
        k = (rk * cos + _swap_halves(rk) * sin) * RET_SCALE
        qb, kb = q.astype(BF16), k.astype(BF16)
        qx, kz = (q * xi_ref[...]).astype(BF16), (k * ze_ref[...]).astype(BF16)
        heads = range(RET_HEADS)
        sl = [slice(h * RET_DK, (h + 1) * RET_DK) for h in heads]
        sv = [slice(h * RET_DV, (h + 1) * RET_DV) for h in heads]
        grads = [jax.vjp(_gn_gate, o_ref[:, sv[h]], rg_ref[:, sv[h]], w_ref[:, sv[h]])[1](dy_ref[:, sv[h]])
                 for h in heads]
        dob = [grads[h][0].astype(BF16) for h in heads]
        vb = [rv_ref[:, sv[h]].astype(BF16) for h in heads]
        rb = [st_ref[0, h].astype(BF16) for h in heads]
        dr = [drs[h] for h in heads]
        drb = [dr[h].astype(BF16) for h in heads]
        a = [(_dot(dob[h], vb[h], _NT) * d_ref[h]).astype(BF16) for h in heads]
        p = [(_dot(qb[:, sl[h]], kb[:, sl[h]], _NT) * d_ref[h]).astype(BF16) for h in heads]
        dqh = [_dot(a[h], kb[:, sl[h]], _NN) + _dot(dob[h], rb[h], _NT) * xi_ref[:, sl[h]] for h in heads]
        dkh = [_dot(a[h], qb[:, sl[h]], _TN) + _dot(vb[h], drb[h], _NT) * ze_ref[:, sl[h]] for h in heads]
        dvh = [_dot(p[h], dob[h], _TN) + _dot(kz[:, sl[h]], drb[h], _NN) for h in heads]
        new_dr = [dr[h] * chunk_decay[h] + _dot(qx[:, sl[h]], dob[h], _TN) for h in heads]
        for h in heads:
            dg_ref[:, sv[h]] = grads[h][1].astype(BF16)
            dw_ref[:, sv[h]] += grads[h][2]
            dqs[:, sl[h]] = dqh[h]
            dks[:, sl[h]] = dkh[h]
            dv_ref[:, sv[h]] = dvh[h].astype(BF16)
            drs[h] = new_dr[h]
        dq = dqs[...]
        dk = dks[...] * RET_SCALE
        dq_ref[...] = (dq * cos + _swap_halves(dq * sin)).astype(BF16)
        dk_ref[...] = (dk * cos + _swap_halves(dk * sin)).astype(BF16)

    qk = (BLOCK, RET_HEADS * RET_DK)
    vv = (BLOCK, RET_HEADS * RET_DV)
    rev = lambda n: nc - 1 - n
    const = lambda shape: pl.BlockSpec(shape, lambda n: (0,) * len(shape))
    return pl.pallas_call(
        body, name=name, grid=(nc,),
        in_specs=[pl.BlockSpec(qk, lambda n: (rev(n), 0)), pl.BlockSpec(qk, lambda n: (rev(n), 1)),
                  pl.BlockSpec(vv, lambda n: (rev(n), 1)), pl.BlockSpec(vv, lambda n: (rev(n), 2)),
                  const((1, RET_HEADS * RET_DV)),
                  pl.BlockSpec(qk, lambda n: (rev(n), 0)), pl.BlockSpec(qk, lambda n: (rev(n), 0)),
                  const((RET_HEADS, BLOCK, BLOCK)), const(qk), const(qk),
                  pl.BlockSpec(vv, lambda n: (rev(n), 0)),
                  pl.BlockSpec((1, RET_HEADS, RET_DK, RET_DV), lambda n: (rev(n), 0, 0, 0)),
                  pl.BlockSpec(vv, lambda n: (rev(n), 0))],
        out_specs=[pl.BlockSpec(qk, lambda n: (rev(n), 0)), pl.BlockSpec(qk, lambda n: (rev(n), 0)),
                   pl.BlockSpec(vv, lambda n: (rev(n), 0)), pl.BlockSpec(vv, lambda n: (rev(n), 0)),
                   const((1, RET_HEADS * RET_DV))],
        out_shape=[jax.ShapeDtypeStruct((T, RET_HEADS * RET_DK), BF16)] * 2
        + [jax.ShapeDtypeStruct((T, RET_HEADS * RET_DV), BF16)] * 2
        + [jax.ShapeDtypeStruct((1, RET_HEADS * RET_DV), F32)],
        scratch_shapes=[pltpu.VMEM((RET_HEADS, RET_DK, RET_DV), F32), pltpu.VMEM(qk, F32), pltpu.VMEM(qk, F32)],
        compiler_params=_params("arbitrary"),
    )(proj_hi, proj_hi, proj_hi, proj_hi, ret_norm_w.reshape(1, -1), cos_t, sin_t, dmat, xi, zeta,
      o_r, states, dy)


GATE_BLK = 512
GATE_FIRST_BLK = 3


def _gated(g0, g1, g2, b0, b1, b2, pa, pb, pc):
    return jax.nn.sigmoid(g0 + b0) * pa + jax.nn.sigmoid(g1 + b1) * pb + jax.nn.sigmoid(g2 + b2) * pc


def _gate_specs(tm):
    return [pl.BlockSpec((tm, GATE_BLK), functools.partial(lambda i, c: (i, c), c=GATE_FIRST_BLK + j))
            for j in range(6)]


def _gate_args(g, bg_ref):
    gi = [jnp.concatenate([g[2 * j][...], g[2 * j + 1][...]], axis=1) for j in range(3)]
    return gi + [bg_ref[j:j + 1, :] for j in range(3)]


def _merge_fwd(proj_hi, b_gate, pa, pb, pc, name):
    T = proj_hi.shape[0]
    tm = _tile(T, (256, 128))

    def body(g0, g1, g2, g3, g4, g5, bg_ref, pa_ref, pb_ref, pc_ref, o_ref):
        args = _gate_args((g0, g1, g2, g3, g4, g5), bg_ref)
        o_ref[...] = _gated(*args, pa_ref[...], pb_ref[...], pc_ref[...]).astype(BF16)

    row = pl.BlockSpec((tm, D_MODEL), lambda i: (i, 0))
    return pl.pallas_call(
        body, name=name, grid=(T // tm,),
        in_specs=_gate_specs(tm) + [pl.BlockSpec((3, D_MODEL), lambda i: (0, 0)), row, row, row],
        out_specs=row, out_shape=jax.ShapeDtypeStruct((T, D_MODEL), BF16),
        compiler_params=_params("parallel"),
    )(*([proj_hi] * 6), b_gate, pa, pb, pc)


def _merge_bwd(proj_hi, b_gate, pa, pb, pc, dm, name):
    T = proj_hi.shape[0]
    tm = _tile(T, (256, 128))

    def body(g0, g1, g2, g3, g4, g5, bg_ref, pa_ref, pb_ref, pc_ref, dm_ref,
             dgi_ref, dbg_ref, dpa_ref, dpb_ref, dpc_ref):
        args = _gate_args((g0, g1, g2, g3, g4, g5), bg_ref)
        _, vjp = jax.vjp(_gated, *args, pa_ref[...], pb_ref[...], pc_ref[...])
        d = vjp(dm_ref[...])
        for j in range(3):
            dgi_ref[:, j * D_MODEL:(j + 1) * D_MODEL] = d[j].astype(BF16)
        dpa_ref[...] = d[6].astype(BF16)
        dpb_ref[...] = d[7].astype(BF16)
        dpc_ref[...] = d[8].astype(BF16)

        @pl.when(pl.program_id(0) == 0)
        def _():
            dbg_ref[...] = jnp.zeros_like(dbg_ref)

        for j in range(3):
            dbg_ref[j:j + 1, :] += d[3 + j]

    row = pl.BlockSpec((tm, D_MODEL), lambda i: (i, 0))
    vec = pl.BlockSpec((3, D_MODEL), lambda i: (0, 0))
    return pl.pallas_call(
        body, name=name, grid=(T // tm,),
        in_specs=_gate_specs(tm) + [vec, row, row, row, row],
        out_specs=[pl.BlockSpec((tm, 3 * D_MODEL), lambda i: (i, 0)), vec, row, row, row],
        out_shape=[jax.ShapeDtypeStruct((T, 3 * D_MODEL), BF16), jax.ShapeDtypeStruct((3, D_MODEL), F32)]
        + [jax.ShapeDtypeStruct((T, D_MODEL), BF16)] * 3,
        compiler_params=_params("arbitrary"),
    )(*([proj_hi] * 6), b_gate, pa, pb, pc, dm)


FFN_TM = 256


def _gelu(u):
    return 0.5 * u * (1.0 + jnp.tanh(0.7978845608028654 * (u + 0.044715 * (u * u * u))))


def _conv_taps(u0, prev8, first):
    prev8 = jnp.where(first, 0.0, prev8)
    row = lax.broadcasted_iota(jnp.int32, u0.shape, 0)
    s1 = jnp.where(row == 0, prev8[7:8], pltpu.roll(u0, 1, 0))
    s2 = jnp.where(row == 0, prev8[6:7], jnp.where(row == 1, prev8[7:8], pltpu.roll(u0, 2, 0)))
    return s1, s2


def _ffn_specs(T, tm):
    row = pl.BlockSpec((tm, D_FF), lambda i: (i, 0))
    prev = pl.BlockSpec((8, D_FF), lambda i: (jnp.maximum(i * (tm // 8) - 1, 0), 0))
    nxt = pl.BlockSpec((8, D_FF), lambda i: (jnp.minimum((i + 1) * (tm // 8), T // 8 - 1), 0))
    return row, prev, nxt


def _ffn_mid_fwd(u0, gt, cw, cb, name):
    T = u0.shape[0]
    tm = _tile(T, (FFN_TM, 128))
    row, prev, _ = _ffn_specs(T, tm)

    def body(u_ref, p_ref, g_ref, cw_ref, cb_ref, f_ref):
        u0 = u_ref[...]
        s1, s2 = _conv_taps(u0, p_ref[...], pl.program_id(0) == 0)
        cw = cw_ref[...]
        u = cw[0:1] * s2 + cw[1:2] * s1 + cw[2:3] * u0 + cb_ref[...]
        f_ref[...] = (_gelu(u) * g_ref[...]).astype(BF16)

    return pl.pallas_call(
        body, name=name, grid=(T // tm,),
        in_specs=[row, prev, row, pl.BlockSpec((3, D_FF), lambda i: (0, 0)), pl.BlockSpec((1, D_FF), lambda i: (0, 0))],
        out_specs=row, out_shape=jax.ShapeDtypeStruct((T, D_FF), BF16),
        compiler_params=_params("parallel"),
    )(u0, u0, gt, cw, cb.reshape(1, D_FF))


def _ffn_mid_bwd_a(u0, gt, cw, cb, df, name):
    T = u0.shape[0]
    tm = _tile(T, (FFN_TM, 128))
    row, prev, _ = _ffn_specs(T, tm)

    def body(u_ref, p_ref, g_ref, cw_ref, cb_ref, df_ref, du_ref, dg_ref, dcw_ref, dcb_ref):
        u0 = u_ref[...]
        s1, s2 = _conv_taps(u0, p_ref[...], pl.program_id(0) == 0)
        cw = cw_ref[...]
        u = cw[0:1] * s2 + cw[1:2] * s1 + cw[2:3] * u0 + cb_ref[...]
        a, vjp = jax.vjp(_gelu, u)
        df = df_ref[...]
        dg_ref[...] = (df * a).astype(BF16)
        du = vjp(df * g_ref[...])[0]
        du_ref[...] = du

        @pl.when(pl.program_id(0) == 0)
        def _():
            dcw_ref[...] = jnp.zeros_like(dcw_ref)
            dcb_ref[...] = jnp.zeros_like(dcb_ref)

        dcw_ref[0:1, :] += jnp.sum(du * s2, axis=0, keepdims=True)
        dcw_ref[1:2, :] += jnp.sum(du * s1, axis=0, keepdims=True)
        dcw_ref[2:3, :] += jnp.sum(du * u0, axis=0, keepdims=True)
        dcb_ref[...] += jnp.sum(du, axis=0, keepdims=True)

    c3 = pl.BlockSpec((3, D_FF), lambda i: (0, 0))
    c1 = pl.BlockSpec((1, D_FF), lambda i: (0, 0))
    return pl.pallas_call(
        body, name=name, grid=(T // tm,),
        in_specs=[row, prev, row, c3, c1, row], out_specs=[row, row, c3, c1],
        out_shape=[jax.ShapeDtypeStruct((T, D_FF), F32), jax.ShapeDtypeStruct((T, D_FF), BF16),
                   jax.ShapeDtypeStruct((3, D_FF), F32), jax.ShapeDtypeStruct((1, D_FF), F32)],
        compiler_params=_params("arbitrary"),
    )(u0, u0, gt, cw, cb.reshape(1, D_FF), df)


def _ffn_mid_bwd_b(du, cw, name):
    T = du.shape[0]
    tm = _tile(T, (FFN_TM, 128))
    row, _, nxt = _ffn_specs(T, tm)

    def body(du_ref, n_ref, cw_ref, o_ref):
        du = du_ref[...]
        nx = jnp.where(pl.program_id(0) == T // tm - 1, 0.0, n_ref[...])
        r = lax.broadcasted_iota(jnp.int32, du.shape, 0)
        u1 = jnp.where(r == tm - 1, nx[0:1], pltpu.roll(du, tm - 1, 0))
        u2 = jnp.where(r == tm - 1, nx[1:2], jnp.where(r == tm - 2, nx[0:1], pltpu.roll(du, tm - 2, 0)))
        cw = cw_ref[...]
        o_ref[...] = (cw[2:3] * du + cw[1:2] * u1 + cw[0:1] * u2).astype(BF16)

    return pl.pallas_call(
        body, name=name, grid=(T // tm,),
        in_specs=[row, nxt, pl.BlockSpec((3, D_FF), lambda i: (0, 0))],
        out_specs=row, out_shape=jax.ShapeDtypeStruct((T, D_FF), BF16),
        compiler_params=_params("parallel"),
    )(du, du, cw)


def _mesh_peers():
    x, y, c = lax.axis_index("x"), lax.axis_index("y"), lax.axis_index("c")
    peers = []
    for k in range(1, N_DEV):
        px = 1 - x if k & 4 else x
        py = 1 - y if k & 2 else y
        pc = 1 - c if k & 1 else c
        peers.append(((px, py, pc), 4 * px + 2 * py + pc))
    return 4 * x + 2 * y + c, peers


def _exchange(src, name, scatter):
    shape = src.shape[-2:]

    def body(s_ref, o_ref, send_sems, recv_sems, local_sem):
        me, peers = _mesh_peers()
        mine = pltpu.make_async_copy(s_ref.at[me] if scatter else s_ref, o_ref.at[me], local_sem)
        mine.start()
        sends, recvs = [], []
        for k, (dev, idx) in enumerate(peers):
            sends.append(pltpu.make_async_remote_copy(
                src_ref=s_ref.at[idx] if scatter else s_ref, dst_ref=o_ref.at[me],
                send_sem=send_sems.at[k], recv_sem=recv_sems.at[k],
                device_id=dev, device_id_type=pl.DeviceIdType.MESH))
            recvs.append(pltpu.make_async_remote_copy(
                src_ref=s_ref.at[idx] if scatter else s_ref, dst_ref=o_ref.at[idx],
                send_sem=send_sems.at[k], recv_sem=recv_sems.at[k],
                device_id=dev, device_id_type=pl.DeviceIdType.MESH))
        for cp in sends:
            cp.start()
        for cp in recvs:
            cp.wait_recv()
        for cp in sends:
            cp.wait_send()
        mine.wait()

    return pl.pallas_call(
        body, name=name,
        in_specs=[pl.BlockSpec(memory_space=pl.ANY)], out_specs=pl.BlockSpec(memory_space=pl.ANY),
        out_shape=jax.ShapeDtypeStruct((N_DEV,) + shape, src.dtype),
        scratch_shapes=[pltpu.SemaphoreType.DMA((N_DEV - 1,)), pltpu.SemaphoreType.DMA((N_DEV - 1,)),
                        pltpu.SemaphoreType.DMA],
    )(src)


_HBM = pl.BlockSpec(memory_space=pltpu.HBM)
_SEM = pl.BlockSpec(memory_space=pltpu.SEMAPHORE)
_DATAFLOW = pltpu.SideEffectType.DATAFLOW_SIDE_EFFECTING


def _split_copies(s_ref, land_ref, send_sems, recv_sems, scatter, landing_row):
    me, peers = _mesh_peers()
    return [pltpu.make_async_remote_copy(
        src_ref=s_ref.at[idx] if scatter else s_ref, dst_ref=land_ref.at[landing_row(me, idx)],
        send_sem=send_sems.at[k], recv_sem=recv_sems.at[k],
        device_id=dev, device_id_type=pl.DeviceIdType.MESH) for k, (dev, idx) in enumerate(peers)]


def _exchange_start(src, name, scatter):
    land = lax.empty((N_DEV,) + src.shape[-2:], src.dtype)

    def body(s_ref, land_ref, send_sems, recv_sems, s_thru, land_thru, token):
        for cp in _split_copies(s_ref, land_ref, send_sems, recv_sems, scatter, lambda me, idx: me):
            cp.start()
        token[...] = jnp.zeros_like(token)

    return pl.pallas_call(
        body, name=name,
        out_shape=(pltpu.SemaphoreType.DMA((N_DEV - 1,)), pltpu.SemaphoreType.DMA((N_DEV - 1,)),
                   pltpu.HBM(src.shape, src.dtype), pltpu.HBM(land.shape, land.dtype),
                   jax.ShapeDtypeStruct((8, 128), F32)),
        in_specs=(_HBM, _HBM), out_specs=(_SEM, _SEM, _HBM, _HBM, pl.BlockSpec(memory_space=pltpu.VMEM)),
        input_output_aliases={0: 2, 1: 3},
        compiler_params=pltpu.CompilerParams(has_side_effects=_DATAFLOW),
    )(pltpu.with_memory_space_constraint(src, pltpu.HBM), pltpu.with_memory_space_constraint(land, pltpu.HBM))


def _exchange_wait(started, after, name, scatter):
    send_sems, recv_sems, s_thru, land_thru, _ = started

    def body(s_ref, land_ref, send_sems, recv_sems, after_ref, s_out, land_out):
        for cp in _split_copies(s_ref, land_ref, send_sems, recv_sems, scatter, lambda me, idx: idx):
            cp.wait_send()
            cp.wait_recv()

    src, got = pl.pallas_call(
        body, name=name,
        out_shape=(pltpu.HBM(s_thru.shape, s_thru.dtype), pltpu.HBM(land_thru.shape, land_thru.dtype)),
        in_specs=(_HBM, _HBM, _SEM, _SEM, pl.BlockSpec(memory_space=pl.ANY)), out_specs=(_HBM, _HBM),
        input_output_aliases={0: 0, 1: 1},
        compiler_params=pltpu.CompilerParams(has_side_effects=_DATAFLOW),
    )(s_thru, land_thru, send_sems, recv_sems, after)
    me = 4 * lax.axis_index("x") + 2 * lax.axis_index("y") + lax.axis_index("c")
    own = lax.dynamic_index_in_dim(src, me, 0, keepdims=True) if scatter else src[None]
    return lax.dynamic_update_slice(got, own, (me, 0, 0))


def _sum_devices(parts, name):
    _, R, C = parts.shape
    tr = _tile(R, (512, 480, 256, 240, 128, 64, 32, 16, 8))

    def body(p_ref, o_ref):
        acc = p_ref[0].astype(F32)
        for j in range(1, N_DEV):
            acc = acc + p_ref[j].astype(F32)
        o_ref[...] = acc

    return pl.pallas_call(
        body, name=name, grid=(R // tr,),
        in_specs=[pl.BlockSpec((N_DEV, tr, C), lambda i: (0, i, 0))],
        out_specs=pl.BlockSpec((tr, C), lambda i: (i, 0)),
        out_shape=jax.ShapeDtypeStruct((R, C), F32),
        compiler_params=_params("parallel"),
    )(parts)


def _adamw(w, g, m, v, name):
    shape = w.shape
    C = shape[-1]
    R = int(np.prod(shape[:-1])) if len(shape) > 1 else 1
    tr = _tile(R, (256, 128, 64, 32, 16, 8))

    def body(w_ref, g_ref, m_ref, v_ref, d_ref, nm_ref, nv_ref):
        g = g_ref[...]
        m = ADAM_B1 * m_ref[...] + (1.0 - ADAM_B1) * g
        v = ADAM_B2 * v_ref[...] + (1.0 - ADAM_B2) * (g * g)
        m_hat = m / (1.0 - ADAM_B1 ** ADAM_STEP)
        v_hat = v / (1.0 - ADAM_B2 ** ADAM_STEP)
        d_ref[...] = -ADAM_LR * (m_hat / (jnp.sqrt(v_hat) + ADAM_EPS) + ADAM_WD * w_ref[...])
        nm_ref[...] = m
        nv_ref[...] = v

    blk = pl.BlockSpec((tr, C), lambda i: (i, 0))
    outs = pl.pallas_call(
        body, name=name, grid=(R // tr,), in_specs=[blk] * 4, out_specs=[blk] * 3,
        out_shape=[jax.ShapeDtypeStruct((R, C), F32)] * 3,
        compiler_params=_params("parallel"),
    )(*[t.reshape(R, C) for t in (w, g, m, v)])
    return [o.reshape(shape) for o in outs]


PACK_ROWS = (1056, 352, 352, 128, 352, 32, 64, 64)
PACK_LAYER = sum(PACK_ROWS)
SMALL_SIZES = (("rel_bias", 384), ("norm_mix_w", 4096), ("ret_norm_w", 2048), ("norm_ffn_w", 4096),
               ("conv_b", 11264), ("final_norm_w", 1024), ("b_gate", 12288), ("conv_w", 33792))
SMALL_ROWS = 72


def _layer_forward(x, wl, tabs, l):
    sv = {"x_in": x}
    h = _rms_fwd(x, wl["norm_mix_w"], f"rms_mix_fwd_{l}")
    lo = _matmul(h, wl["win_lo"], "nt", f"in_proj_lo_{l}", out_dtype=BF16)
    hi = _matmul(h, wl["win_hi"], "nt", f"in_proj_hi_{l}")
    outs, lses, a_views = [], [], []
    for g in range(3):
        o, s, views = _a_attn_fwd(lo, tabs["bias"], g, f"a_fwd_{g}_{l}")
        outs.append(o)
        lses.append(s)
        a_views.append(views)
    y_a = _a_merge_fwd(outs, lses, f"a_merge_fwd_{l}")
    y_b, ctot = _sb_fwd(lo, f"sb_fwd_{l}")
    y_c, o_r, states = _ret_fwd(hi, wl["ret_norm_w"], tabs["ret"], tabs["decay"], f"ret_fwd_{l}")
    pa = _matmul(y_a, wl["wpa"], "nt", f"proj_a_{l}")
    pb = _matmul(y_b, wl["wpb"], "nt", f"proj_b_{l}")
    pc = _matmul(y_c, wl["wpc"], "nt", f"proj_c_{l}")
    merged = _merge_fwd(hi, wl["b_gate"], pa, pb, pc, f"merge_fwd_{l}")
    x_mid = _matmul(merged, wl["wout"], "nn", f"out_proj_{l}", add=x)
    h2 = _rms_fwd(x_mid, wl["norm_ffn_w"], f"rms_ffn_fwd_{l}")
    u0 = _matmul(h2, wl["wup"], "nt", f"ffn_up_{l}")
    gt = _matmul(h2, wl["wgate"], "nt", f"ffn_gate_{l}")
    f = _ffn_mid_fwd(u0, gt, wl["conv_w"], wl["conv_b"], f"ffn_mid_fwd_{l}")
    x_out = _matmul(f, wl["wdown"], "nn", f"ffn_down_{l}", add=x_mid)
    sv.update(h=h, lo=lo, hi=hi, outs=outs, lses=lses, a_views=a_views, y_a=y_a, y_b=y_b, ctot=ctot, y_c=y_c, o_r=o_r,
              states=states, pa=pa, pb=pb, pc=pc, merged=merged, x_mid=x_mid, h2=h2, u0=u0, gt=gt, f=f)
    return x_out, sv


def _layer_backward(dx, sv, wl, tabs, l):
    df = _matmul(dx, wl["wdown"], "nt", f"ffn_down_dx_{l}")
    d_wdown = _matmul(sv["f"], dx, "tn", f"ffn_down_dw_{l}")
    du, dgt, d_cw, d_cb = _ffn_mid_bwd_a(sv["u0"], sv["gt"], wl["conv_w"], wl["conv_b"], df, f"ffn_mid_bwd_a_{l}")
    du0 = _ffn_mid_bwd_b(du, wl["conv_w"], f"ffn_mid_bwd_b_{l}")
    dh2 = _matmul(du0, wl["wup"], "nn", f"ffn_up_dx_{l}")
    dh2 = _matmul(dgt, wl["wgate"], "nn", f"ffn_gate_dx_{l}", add=dh2)
    d_wup = _matmul(du0, sv["h2"], "tn", f"ffn_up_dw_{l}")
    d_wgate = _matmul(dgt, sv["h2"], "tn", f"ffn_gate_dw_{l}")
    dx_mid, d_nffn = _rms_bwd(sv["x_mid"], wl["norm_ffn_w"], dh2, dx, f"rms_ffn_bwd_{l}")
    dm = _matmul(dx_mid, wl["wout"], "nt", f"out_proj_dx_{l}")
    d_wout = _matmul(sv["merged"], dx_mid, "tn", f"out_proj_dw_{l}")
    dgi, d_bg, dpa, dpb, dpc = _merge_bwd(sv["hi"], wl["b_gate"], sv["pa"], sv["pb"], sv["pc"], dm, f"merge_bwd_{l}")
    dy_a = _matmul(dpa, wl["wpa"], "nn", f"proj_a_dx_{l}")
    dy_b = _matmul(dpb, wl["wpb"], "nn", f"proj_b_dx_{l}")
    dy_c = _matmul(dpc, wl["wpc"], "nn", f"proj_c_dx_{l}")
    d_wpa = _matmul(dpa, sv["y_a"], "tn", f"proj_a_dw_{l}")
    d_wpb = _matmul(dpb, sv["y_b"], "tn", f"proj_b_dw_{l}")
    d_wpc = _matmul(dpc, sv["y_c"], "tn", f"proj_c_dw_{l}")
    d_rq, d_rk, d_rv, d_rg, d_rnw = _ret_bwd(sv["hi"], wl["ret_norm_w"], sv["o_r"], sv["states"], dy_c,
                                             tabs["ret"], tabs["decay"], f"ret_bwd_{l}")
    d_sq, d_sk, d_sv = _sb_bwd(sv["lo"], sv["ctot"], dy_b, f"sb_bwd_{l}")
    douts, dlses = _a_merge_bwd(sv["outs"], sv["lses"], dy_a, f"a_merge_bwd_{l}")
    dqs, dks, dvs, dbs = [], [], [], []
    for g in range(3):
        dq, dk, dv, db = _a_attn_bwd(sv["lo"], tabs["bias"], *sv["a_views"][g], douts[g], dlses[g],
                                     g, f"a_bwd_{g}_{l}")
        dqs.append(dq)
        dks.append(dk)
        dvs.append(dv)
        dbs.append(db)
    heads = lambda t: [t[i].astype(BF16) for i in range(SB_HEADS)]
    a_cols = [t[g] for g in range(3) for t in (dqs, dks, dvs)]
    dlo = jnp.concatenate(a_cols + [d_sq] + heads(d_sk) + heads(d_sv), axis=1)
    dhi = jnp.concatenate([d_rq, d_rk, d_rv, d_rg, dgi], axis=1)
    dh = _matmul(dlo, wl["win_lo"], "nn", f"in_proj_lo_dx_{l}")
    dh = _matmul(dhi, wl["win_hi"], "nn", f"in_proj_hi_dx_{l}", add=dh)
    d_win = jnp.concatenate([_group_major(_matmul(dlo, sv["h"], "tn", f"in_proj_lo_dw_{l}")),
                             _matmul(dhi, sv["h"], "tn", f"in_proj_hi_dw_{l}")], axis=0)
    dx_in, d_nmix = _rms_bwd(sv["x_in"], wl["norm_mix_w"], dh, dx_mid, f"rms_mix_bwd_{l}")
    chunks = lambda t: t.reshape(N_DEV, -1, D_MODEL)
    big = jnp.concatenate([chunks(d_win), chunks(d_wup), chunks(d_wgate), chunks(d_wout), chunks(d_wdown),
                           chunks(d_wpa), chunks(d_wpb), chunks(d_wpc)], axis=1)
    small = dict(norm_mix_w=d_nmix[0], ret_norm_w=d_rnw[0], norm_ffn_w=d_nffn[0], conv_b=d_cb[0],
                 b_gate=d_bg, conv_w=d_cw, dbias=jnp.concatenate(dbs, axis=0))
    return dx_in, big, small


def kernel(x, rel_bias, norm_mix_w, w_in, b_gate, ret_norm_w, w_proj_a, w_proj_b, w_proj_c, w_out, norm_ffn_w, w_up, w_gate, conv_w, conv_b, w_down, final_norm_w, loss_target, m_rel_bias, m_norm_mix_w, m_w_in, m_b_gate, m_ret_norm_w, m_w_proj_a, m_w_proj_b, m_w_proj_c, m_w_out, m_norm_ffn_w, m_w_up, m_w_gate, m_conv_w, m_conv_b, m_w_down, m_final_norm_w, v_rel_bias, v_norm_mix_w, v_w_in, v_b_gate, v_ret_norm_w, v_w_proj_a, v_w_proj_b, v_w_proj_c, v_w_out, v_norm_ffn_w, v_w_up, v_w_gate, v_conv_w, v_conv_b, v_w_down, v_final_norm_w):
    T = x.shape[1]
    me = 4 * lax.axis_index("x") + 2 * lax.axis_index("y") + lax.axis_index("c")

    gathers, token = [], None
    for l in range(DEPTH):
        rows = [w_in[l].T, w_up[l].T, w_gate[l].T, w_out[l], w_down[l],
                w_proj_a[l].T.reshape(-1, D_MODEL), w_proj_b[l].T.reshape(-1, D_MODEL),
                w_proj_c[l].T.reshape(-1, D_MODEL)]
        src = jnp.concatenate(rows, axis=0).astype(BF16)
        if token is not None:
            src = src + token[0, 0].astype(BF16)
        gathers.append(_exchange_start(src, f"gather_weights_start_{l}", scatter=False))
        token = gathers[-1][4]
    n_bg, n_cw = b_gate.size, conv_w.size
    tiny = jnp.concatenate([b_gate.reshape(-1), conv_w.reshape(-1), jnp.zeros((8 * D_MODEL - n_bg - n_cw,), F32)])
    tall = _exchange(tiny.reshape(8, D_MODEL), "gather_small_weights", scatter=False).reshape(N_DEV, -1)
    spread = lambda t, w: t.reshape(N_DEV, DEPTH, 3, w).transpose(1, 2, 0, 3).reshape(DEPTH, 3, N_DEV * w)
    b_gate_full = spread(tall[:, :n_bg], b_gate.shape[-1])
    conv_w_full = spread(tall[:, n_bg:n_bg + n_cw], conv_w.shape[-1])

    def layer_weights(l, wall):
        offs = np.cumsum((0,) + PACK_ROWS)
        seg = lambda j: wall[:, offs[j]:offs[j + 1], :]
        win = seg(0).reshape(-1, D_MODEL)
        unpack = lambda j, k: seg(j).reshape(N_DEV, D_MODEL // N_DEV, k).reshape(D_MODEL, k)
        return dict(win_lo=_group_major(win[:LO_WIDTH]), win_hi=win[LO_WIDTH:], wup=seg(1).reshape(-1, D_MODEL),
                    wgate=seg(2).reshape(-1, D_MODEL), wout=seg(3).reshape(-1, D_MODEL),
                    wdown=seg(4).reshape(-1, D_MODEL), wpa=unpack(5, 256), wpb=unpack(6, 512), wpc=unpack(7, 512),
                    norm_mix_w=norm_mix_w[l], norm_ffn_w=norm_ffn_w[l], ret_norm_w=ret_norm_w[l],
                    b_gate=b_gate_full[l], conv_w=conv_w_full[l], conv_b=conv_b[l])

    buckets = _a_buckets()
    ret_tabs, decay = _ret_tables(T)
    tabs = dict(bias=_a_bias_tables(rel_bias, buckets), ret=ret_tabs, decay=decay)

    xs = x[0]
    saved, wls = [], []
    for l in range(DEPTH):
        wall = _exchange_wait(gathers[l], token if l == 0 else xs, f"gather_weights_wait_{l}", scatter=False)
        wls.append(layer_weights(l, wall))
        xs, sv = _layer_forward(xs, wls[l], tabs, l)
        saved.append(sv)
    loss_tile, dx, d_final = _loss_head(xs, final_norm_w, loss_target[0], "loss_head")
    loss = lax.psum(loss_tile[0, 0], ("x", "y", "c"))

    scatters, smalls = [None] * DEPTH, [None] * DEPTH
    for l in reversed(range(DEPTH)):
        dx, big, smalls[l] = _layer_backward(dx, saved[l], wls[l], tabs, l)
        scatters[l] = _exchange_start(big, f"scatter_grads_start_{l}", scatter=True)
        if l > 0:
            wls[l - 1] = dict(wls[l - 1], conv_b=wls[l - 1]["conv_b"] + scatters[l][4][0, 0])
    grad_x = dx[None]

    mine = jnp.stack([_sum_devices(_exchange_wait(scatters[l], dx, f"scatter_grads_wait_{l}", scatter=True),
                                   f"sum_grads_{l}") for l in range(DEPTH)])
    dbias = smalls[0]["dbias"] + smalls[1]["dbias"] + smalls[2]["dbias"] + smalls[3]["dbias"]
    small_vals = dict(rel_bias=_a_bias_grad(dbias, buckets), final_norm_w=d_final[0])
    for name in ("norm_mix_w", "ret_norm_w", "norm_ffn_w", "conv_b", "b_gate", "conv_w"):
        small_vals[name] = jnp.stack([smalls[l][name] for l in range(DEPTH)])
    flat = jnp.concatenate([small_vals[n].reshape(-1) for n, _ in SMALL_SIZES])
    flat = jnp.concatenate([flat, jnp.zeros((SMALL_ROWS * D_MODEL - flat.shape[0],), F32)])
    sparts = _exchange(flat.reshape(SMALL_ROWS, D_MODEL), "gather_small_grads", scatter=False)
    ssum = _sum_devices(sparts, "sum_small_grads").reshape(-1)

    grads = {}
    off = 0
    for name, size in SMALL_SIZES:
        grads[name] = ssum[off:off + size]
        off += size
    grads["rel_bias"] = grads["rel_bias"].reshape(REL_BUCKETS, 12)
    for name in ("norm_mix_w", "norm_ffn_w"):
        grads[name] = grads[name].reshape(DEPTH, D_MODEL)
    grads["ret_norm_w"] = grads["ret_norm_w"].reshape(DEPTH, -1)
    grads["conv_b"] = grads["conv_b"].reshape(DEPTH, D_FF)
    bw, cwid = b_gate.shape[-1], conv_w.shape[-1]
    grads["b_gate"] = lax.dynamic_slice_in_dim(grads["b_gate"].reshape(DEPTH, 3, -1), me * bw, bw, axis=2)
    grads["conv_w"] = lax.dynamic_slice_in_dim(grads["conv_w"].reshape(DEPTH, 3, -1), me * cwid, cwid, axis=2)

    offs = np.cumsum((0,) + PACK_ROWS)
    per_layer = mine
    seg = lambda j: per_layer[:, offs[j]:offs[j + 1], :]
    back = lambda j, k: seg(j).reshape(DEPTH, D_MODEL // N_DEV, k).transpose(0, 2, 1)
    grads["w_in"] = seg(0).transpose(0, 2, 1)
    grads["w_up"] = seg(1).transpose(0, 2, 1)
    grads["w_gate"] = seg(2).transpose(0, 2, 1)
    grads["w_out"] = seg(3)
    grads["w_down"] = seg(4)
    grads["w_proj_a"] = back(5, 256)
    grads["w_proj_b"] = back(6, 512)
    grads["w_proj_c"] = back(7, 512)

    order = ["rel_bias", "norm_mix_w", "w_in", "b_gate", "ret_norm_w", "w_proj_a", "w_proj_b", "w_proj_c",
             "w_out", "norm_ffn_w", "w_up", "w_gate", "conv_w", "conv_b", "w_down", "final_norm_w"]
    ws = dict(rel_bias=rel_bias, norm_mix_w=norm_mix_w, w_in=w_in, b_gate=b_gate, ret_norm_w=ret_norm_w,
              w_proj_a=w_proj_a, w_proj_b=w_proj_b, w_proj_c=w_proj_c, w_out=w_out, norm_ffn_w=norm_ffn_w,
              w_up=w_up, w_gate=w_gate, conv_w=conv_w, conv_b=conv_b, w_down=w_down, final_norm_w=final_norm_w)
    ms = dict(rel_bias=m_rel_bias, norm_mix_w=m_norm_mix_w, w_in=m_w_in, b_gate=m_b_gate, ret_norm_w=m_ret_norm_w,
              w_proj_a=m_w_proj_a, w_proj_b=m_w_proj_b, w_proj_c=m_w_proj_c, w_out=m_w_out,
              norm_ffn_w=m_norm_ffn_w, w_up=m_w_up, w_gate=m_w_gate, conv_w=m_conv_w, conv_b=m_conv_b,
              w_down=m_w_down, final_norm_w=m_final_norm_w)
    vs = dict(rel_bias=v_rel_bias, norm_mix_w=v_norm_mix_w, w_in=v_w_in, b_gate=v_b_gate, ret_norm_w=v_ret_norm_w,
              w_proj_a=v_w_proj_a, w_proj_b=v_w_proj_b, w_proj_c=v_w_proj_c, w_out=v_w_out,
              norm_ffn_w=v_norm_ffn_w, w_up=v_w_up, w_gate=v_w_gate, conv_w=v_conv_w, conv_b=v_conv_b,
              w_down=v_w_down, final_norm_w=v_final_norm_w)
    deltas, new_m, new_v = [], [], []
    for name in order:
        g = grads[name].reshape(ws[name].shape)
        grads[name] = g
        d, nm, nv = _adamw(ws[name], g, ms[name], vs[name], f"adamw_{name}")
        deltas.append(d)
        new_m.append(nm)
        new_v.append(nv)
    return (loss, grad_x, *[grads[n] for n in order], *deltas, *new_m, *new_v)
```

```python
import functools

import numpy as np
import jax
import jax.numpy as jnp
from jax import lax
from jax.experimental import pallas as pl
from jax.experimental.pallas import tpu as pltpu

F32 = jnp.float32
BF16 = jnp.bfloat16

D_MODEL = 1024
DEPTH = 4
N_DEV = 8
BLOCK = 128
DIL_GROUPS = ((128, 1), (512, 4), (2048, 16))
A_HEADS_PER_GROUP = 4
HEAD_DIM = 64
A_GROUP_WIDTH = A_HEADS_PER_GROUP * HEAD_DIM
SB_HEADS = 4
SB_HEAD_DIM = 128
RET_HEADS = 4
RET_DK = 64
RET_DV = 128
ROPE_BASE = 10000.0
REL_BUCKETS = 32
REL_MAX_DIST = 2048
D_FF = 2816
EPS = 1e-6
LO_WIDTH = 3840
HI_WIDTH = 4608
A_SCALE = HEAD_DIM ** -0.5
SB_SCALE = SB_HEAD_DIM ** -0.5
RET_SCALE = RET_DK ** -0.5
NEG = -1e30

ADAM_LR = 0.001
ADAM_B1 = 0.9
ADAM_B2 = 0.999
ADAM_EPS = 1e-08
ADAM_WD = 0.01
ADAM_STEP = 10

VMEM_LIMIT_V7X = 56 * 1024 * 1024

_NT = (((1,), (1,)), ((), ()))
_NN = (((1,), (0,)), ((), ()))
_TN = (((0,), (0,)), ((), ()))


def _dot(a, b, dims):
    return lax.dot_general(a, b, dims, preferred_element_type=F32)


def _tile(n, cands):
    for c in cands:
        if n % c == 0:
            return c
    return n


def _params(*sem):
    return pltpu.CompilerParams(dimension_semantics=sem, vmem_limit_bytes=VMEM_LIMIT_V7X)


def _matmul(a, b, mode, name, out_dtype=None, add=None):
    if out_dtype is None:
        out_dtype = BF16 if mode == "tn" else F32
    if mode == "tn":
        (K, M), N = a.shape, b.shape[1]
    elif mode == "nn":
        (M, K), N = a.shape, b.shape[1]
    else:
        (M, K), N = a.shape, b.shape[0]
    wide = (1536, 1408, 1280, 1152, 1024, 768, 512, 384, 256, 128)
    if mode == "tn":
        tm, tn, tk = _tile(M, wide), _tile(N, wide[4:]), _tile(K, (1024, 512, 256, 128))
    else:
        tm, tn, tk = _tile(M, (512, 256, 128)), _tile(N, wide), K
    nk = K // tk
    if mode == "tn":
        grid, ij = (M // tm, N // tn, nk), (lambda g0, g1: (g0, g1))
    else:
        grid, ij = (N // tn, M // tm, nk), (lambda g0, g1: (g1, g0))
    at = lambda f: (lambda g0, g1, k: f(*ij(g0, g1), k))
    if mode == "tn":
        a_spec = pl.BlockSpec((tk, tm), at(lambda i, j, k: (k, i)))
    else:
        a_spec = pl.BlockSpec((tm, tk), at(lambda i, j, k: (i, k)))
    if mode == "nt":
        b_spec = pl.BlockSpec((tn, tk), at(lambda i, j, k: (j, k)))
    else:
        b_spec = pl.BlockSpec((tk, tn), at(lambda i, j, k: (k, j)))
    dims = {"nt": _NT, "nn": _NN, "tn": _TN}[mode]
    o_spec = pl.BlockSpec((tm, tn), at(lambda i, j, k: (i, j)))
    has_add = add is not None

    def finish(refs, r):
        if has_add:
            r = r + refs[2][...]
        refs[-1 if nk == 1 else -2][...] = r.astype(out_dtype)

    def body(*refs):
        part = _dot(refs[0][...].astype(BF16), refs[1][...].astype(BF16), dims)
        if nk == 1:
            finish(refs, part)
            return
        acc = refs[-1]
        k = pl.program_id(2)

        @pl.when(k == 0)
        def _():
            acc[...] = part

        @pl.when(k > 0)
        def _():
            acc[...] += part

        @pl.when(k == nk - 1)
        def _():
            finish(refs, acc[...])

    ins = [a, b] + ([add] if has_add else [])
    specs = [a_spec, b_spec] + ([o_spec] if has_add else [])
    return pl.pallas_call(
        body, name=name, grid=grid,
        in_specs=specs, out_specs=o_spec,
        out_shape=jax.ShapeDtypeStruct((M, N), out_dtype),
        scratch_shapes=[] if nk == 1 else [pltpu.VMEM((tm, tn), F32)],
        compiler_params=_params("parallel", "parallel", "arbitrary"),
    )(*ins)


def _rms(x, w):
    return x * lax.rsqrt(jnp.mean(x * x, axis=-1, keepdims=True) + EPS) * w


def _rms_fwd(x, w, name):
    T = x.shape[0]
    tm = _tile(T, (512, 256, 128))

    def body(x_ref, w_ref, o_ref):
        o_ref[...] = _rms(x_ref[...], w_ref[...]).astype(BF16)

    return pl.pallas_call(
        body, name=name, grid=(T // tm,),
        in_specs=[pl.BlockSpec((tm, D_MODEL), lambda i: (i, 0)), pl.BlockSpec((1, D_MODEL), lambda i: (0, 0))],
        out_specs=pl.BlockSpec((tm, D_MODEL), lambda i: (i, 0)),
        out_shape=jax.ShapeDtypeStruct((T, D_MODEL), BF16),
        compiler_params=_params("parallel"),
    )(x, w.reshape(1, D_MODEL))


def _rms_bwd(x, w, dh, res, name):
    T = x.shape[0]
    tm = _tile(T, (512, 256, 128))

    def body(x_ref, w_ref, dh_ref, res_ref, dx_ref, dw_ref):
        _, vjp = jax.vjp(_rms, x_ref[...], w_ref[...])
        dx, dw = vjp(dh_ref[...])
        dx_ref[...] = dx + res_ref[...]

        @pl.when(pl.program_id(0) == 0)
        def _():
            dw_ref[...] = jnp.zeros_like(dw_ref)

        dw_ref[...] += dw

    row = pl.BlockSpec((tm, D_MODEL), lambda i: (i, 0))
    vec = pl.BlockSpec((1, D_MODEL), lambda i: (0, 0))
    return pl.pallas_call(
        body, name=name, grid=(T // tm,),
        in_specs=[row, vec, row, row], out_specs=[row, vec],
        out_shape=[jax.ShapeDtypeStruct((T, D_MODEL), F32), jax.ShapeDtypeStruct((1, D_MODEL), F32)],
        compiler_params=_params("arbitrary"),
    )(x, w.reshape(1, D_MODEL), dh, res)


def _loss_head(x, w, tgt, name):
    T = x.shape[0]
    tm = _tile(T, (512, 256, 128))

    def body(x_ref, w_ref, t_ref, l_ref, dx_ref, dw_ref):
        y, vjp = jax.vjp(_rms, x_ref[...], w_ref[...])
        e = y - t_ref[...]
        dx, dw = vjp(e * (1.0 / D_MODEL))
        dx_ref[...] = dx

        @pl.when(pl.program_id(0) == 0)
        def _():
            dw_ref[...] = jnp.zeros_like(dw_ref)
            l_ref[...] = jnp.zeros_like(l_ref)

        dw_ref[...] += dw
        l_ref[...] += 0.5 * jnp.sum(jnp.mean(e * e, axis=-1, keepdims=True))

    row = pl.BlockSpec((tm, D_MODEL), lambda i: (i, 0))
    vec = pl.BlockSpec((1, D_MODEL), lambda i: (0, 0))
    return pl.pallas_call(
        body, name=name, grid=(T // tm,),
        in_specs=[row, vec, row],
        out_specs=[pl.BlockSpec((8, 128), lambda i: (0, 0)), row, vec],
        out_shape=[jax.ShapeDtypeStruct((8, 128), F32), jax.ShapeDtypeStruct((T, D_MODEL), F32),
                   jax.ShapeDtypeStruct((1, D_MODEL), F32)],
        compiler_params=_params("arbitrary"),
    )(x, w.reshape(1, D_MODEL), tgt)


def _t5_bucket(dist):
    max_exact = REL_BUCKETS // 2
    n = np.maximum(dist, 0)
    large = max_exact + (np.log(np.maximum(n, 1) / max_exact) / np.log(REL_MAX_DIST / max_exact)
                         * (REL_BUCKETS - max_exact)).astype(np.int64)
    large = np.minimum(large, REL_BUCKETS - 1)
    return np.where(n < max_exact, n, large).astype(np.int32)


def _a_buckets():
    steps = np.arange(BLOCK)[:, None] + BLOCK - np.arange(2 * BLOCK)[None, :]
    out = []
    for window, dilation in DIL_GROUPS:
        in_band = (steps >= 0) & (steps <= window // dilation)
        out.append(np.where(in_band, _t5_bucket(steps * dilation), -1))
    return jnp.asarray(np.stack(out).astype(np.int32))


def _a_bias_tables(rel_bias, buckets):
    def body(rb_ref, bk_ref, o_ref):
        hh = pl.program_id(0)
        bk = bk_ref[0]
        acc = jnp.full((BLOCK, 2 * BLOCK), NEG, F32)
        for b in range(REL_BUCKETS):
            acc = jnp.where(bk == b, rb_ref[b, hh], acc)
        o_ref[0] = acc

    return pl.pallas_call(
        body, name="a_bias_tables", grid=(12,),
        in_specs=[pl.BlockSpec(memory_space=pltpu.SMEM),
                  pl.BlockSpec((1, BLOCK, 2 * BLOCK), lambda h: (h // 4, 0, 0))],
        out_specs=pl.BlockSpec((1, BLOCK, 2 * BLOCK), lambda h: (h, 0, 0)),
        out_shape=jax.ShapeDtypeStruct((12, BLOCK, 2 * BLOCK), F32),
        compiler_params=_params("parallel"),
    )(rel_bias, buckets)


def _a_bias_grad(dbias, buckets):
    def body(db_ref, bk_ref, o_ref):
        bk = bk_ref[0]
        db = db_ref[0]
        lane = lax.broadcasted_iota(jnp.int32, (8, 128), 1)
        acc = jnp.zeros((8, 128), F32)
        for b in range(REL_BUCKETS):
            acc = jnp.where(lane == b, jnp.sum(jnp.where(bk == b, db, 0.0)), acc)
        o_ref[0] = acc

    out = pl.pallas_call(
        body, name="a_bias_grad", grid=(12,),
        in_specs=[pl.BlockSpec((1, BLOCK, 2 * BLOCK), lambda h: (h, 0, 0)),
                  pl.BlockSpec((1, BLOCK, 2 * BLOCK), lambda h: (h // 4, 0, 0))],
        out_specs=pl.BlockSpec((1, 8, 128), lambda h: (h, 0, 0)),
        out_shape=jax.ShapeDtypeStruct((12, 8, 128), F32),
        compiler_params=_params("parallel"),
    )(dbias, buckets)
    return out[:, 0, :REL_BUCKETS].T


def _a_logits(q_ref, kp_ref, kc_ref, b_ref, h, ok):
    sl = slice(h * HEAD_DIM, (h + 1) * HEAD_DIM)
    qh = q_ref[:, sl]
    s = jnp.concatenate([_dot(qh, kp_ref[:, sl], _NT), _dot(qh, kc_ref[:, sl], _NT)], axis=1)
    s = s * A_SCALE + b_ref[h]
    return jnp.where(ok, s, NEG)


def _group_major(w):
    n = 3 * 3 * A_GROUP_WIDTH
    head = w[:n].reshape(3, 3, A_GROUP_WIDTH, -1).transpose(1, 0, 2, 3).reshape(n, -1)
    return jnp.concatenate([head, w[n:]], axis=0)


def _a_view(proj_lo, g):
    if g == 0:
        return proj_lo, proj_lo.shape[1] // A_GROUP_WIDTH
    T, d, w = proj_lo.shape[0], DIL_GROUPS[g][1], 3 * A_GROUP_WIDTH
    return proj_lo[:, g * w:(g + 1) * w].reshape(T // d, d * w), 3


def _a_attn_fwd(proj_lo, bias_all, g, name):
    T = proj_lo.shape[0]
    d = DIL_GROUPS[g][1]
    L = T // d
    nb = L // BLOCK
    pv, nlo = _a_view(proj_lo, g)

    def body(q_ref, kc_ref, kp_ref, vc_ref, vp_ref, b_ref, o_ref, l_ref):
        n = pl.program_id(1)
        col = lax.broadcasted_iota(jnp.int32, (BLOCK, 2 * BLOCK), 1)
        ok = col >= jnp.where(n > 0, 0, BLOCK)
        heads = range(A_HEADS_PER_GROUP)
        sls = [slice(h * HEAD_DIM, (h + 1) * HEAD_DIM) for h in heads]
        s = [_a_logits(q_ref, kp_ref, kc_ref, b_ref, h, ok) for h in heads]
        m = [jnp.max(s[h], axis=-1, keepdims=True) for h in heads]
        p = [jnp.exp(s[h] - m[h]) for h in heads]
        l = [jnp.sum(p[h], axis=-1, keepdims=True) for h in heads]
        pb = [p[h].astype(BF16) for h in heads]
        o = [_dot(pb[h][:, :BLOCK], vp_ref[:, sls[h]], _NN) + _dot(pb[h][:, BLOCK:], vc_ref[:, sls[h]], _NN)
             for h in heads]
        for h in heads:
            o_ref[:, sls[h]] = o[h] / l[h]
            l_ref[:, sls[h]] = jnp.broadcast_to(m[h] + jnp.log(l[h]), (BLOCK, HEAD_DIM))

    blk = (BLOCK, A_GROUP_WIDTH)
    out_spec = pl.BlockSpec(blk, lambda r, n: (n, r))
    out, lse = pl.pallas_call(
        body, name=name, grid=(d, nb),
        in_specs=[pl.BlockSpec(blk, lambda r, n: (n, r * nlo)),
                  pl.BlockSpec(blk, lambda r, n: (n, r * nlo + 1)),
                  pl.BlockSpec(blk, lambda r, n: (jnp.maximum(n - 1, 0), r * nlo + 1)),
                  pl.BlockSpec(blk, lambda r, n: (n, r * nlo + 2)),
                  pl.BlockSpec(blk, lambda r, n: (jnp.maximum(n - 1, 0), r * nlo + 2)),
                  pl.BlockSpec((4, BLOCK, 2 * BLOCK), lambda r, n: (g, 0, 0))],
        out_specs=[out_spec, out_spec],
        out_shape=[jax.ShapeDtypeStruct((L, d * A_GROUP_WIDTH), F32)] * 2,
        compiler_params=_params("parallel", "arbitrary"),
    )(pv, pv, pv, pv, pv, bias_all)
    return out.reshape(T, A_GROUP_WIDTH), lse.reshape(T, A_GROUP_WIDTH), (out, lse)


def _a_attn_bwd(proj_lo, bias_all, out, lse, dout, dlse, g, name):
    T = proj_lo.shape[0]
    d = DIL_GROUPS[g][1]
    L = T // d
    nb = L // BLOCK
    pv, nlo = _a_view(proj_lo, g)
    view = lambda t: t.reshape(L, d * A_GROUP_WIDTH)

    def body(q_ref, kc_ref, kp_ref, vc_ref, vp_ref, b_ref, o_ref, l_ref, do_ref, dl_ref,
             dq_ref, dk_ref, dv_ref, db_ref, ck, cv):
        r = pl.program_id(0)
        n = pl.program_id(1)

        @pl.when((r == 0) & (n == 0))
        def _():
            db_ref[...] = jnp.zeros_like(db_ref)

        @pl.when(n == 0)
        def _():
            ck[...] = jnp.zeros_like(ck)
            cv[...] = jnp.zeros_like(cv)

        @pl.when(n < nb)
        def _():
            col = lax.broadcasted_iota(jnp.int32, (BLOCK, 2 * BLOCK), 1)
            ok = col >= jnp.where(n > 0, 0, BLOCK)
            heads = range(A_HEADS_PER_GROUP)
            sls = [slice(h * HEAD_DIM, (h + 1) * HEAD_DIM) for h in heads]
            s = [_a_logits(q_ref, kp_ref, kc_ref, b_ref, h, ok) for h in heads]
            p = [jnp.exp(s[h] - l_ref[:, h * HEAD_DIM:h * HEAD_DIM + 1]) for h in heads]
            do = [do_ref[:, sls[h]] for h in heads]
            dob = [do[h].astype(BF16) for h in heads]
            shift = [jnp.sum(dl_ref[:, sls[h]] - do[h] * o_ref[:, sls[h]], axis=-1, keepdims=True) for h in heads]
            dp = [jnp.concatenate([_dot(dob[h], vp_ref[:, sls[h]], _NT), _dot(dob[h], vc_ref[:, sls[h]], _NT)],
                                  axis=1) for h in heads]
            ds = [p[h] * (dp[h] + shift[h]) for h in heads]
            for h in heads:
                db_ref[h] += ds[h]
            dsb = [(ds[h] * A_SCALE).astype(BF16) for h in heads]
            pb = [p[h].astype(BF16) for h in heads]
            qh = [q_ref[:, sls[h]] for h in heads]
            dq = [_dot(dsb[h][:, :BLOCK], kp_ref[:, sls[h]], _NN) + _dot(dsb[h][:, BLOCK:], kc_ref[:, sls[h]], _NN)
                  for h in heads]
            dkp = [_dot(dsb[h][:, :BLOCK], qh[h], _TN) for h in heads]
            dvp = [_dot(pb[h][:, :BLOCK], dob[h], _TN) for h in heads]
            dkc = [_dot(dsb[h][:, BLOCK:], qh[h], _TN) for h in heads]
            dvc = [_dot(pb[h][:, BLOCK:], dob[h], _TN) for h in heads]
            for h in heads:
                dq_ref[:, sls[h]] = dq[h].astype(BF16)
                dk_ref[:, sls[h]] = (ck[:, sls[h]] + dkp[h]).astype(BF16)
                dv_ref[:, sls[h]] = (cv[:, sls[h]] + dvp[h]).astype(BF16)
                ck[:, sls[h]] = dkc[h]
                cv[:, sls[h]] = dvc[h]

        @pl.when(n == nb)
        def _():
            dk_ref[...] = ck[...].astype(BF16)
            dv_ref[...] = cv[...].astype(BF16)

    blk = (BLOCK, A_GROUP_WIDTH)
    nq = lambda n: jnp.minimum(n, nb - 1)
    prv = lambda n: jnp.maximum(jnp.minimum(n, nb - 1) - 1, 0)
    cur_o = pl.BlockSpec(blk, lambda r, n: (nq(n), r))
    lag_o = pl.BlockSpec(blk, lambda r, n: (jnp.maximum(n - 1, 0), r))
    dq, dk, dv, db = pl.pallas_call(
        body, name=name, grid=(d, nb + 1),
        in_specs=[pl.BlockSpec(blk, lambda r, n: (nq(n), r * nlo)),
                  pl.BlockSpec(blk, lambda r, n: (nq(n), r * nlo + 1)),
                  pl.BlockSpec(blk, lambda r, n: (prv(n), r * nlo + 1)),
                  pl.BlockSpec(blk, lambda r, n: (nq(n), r * nlo + 2)),
                  pl.BlockSpec(blk, lambda r, n: (prv(n), r * nlo + 2)),
                  pl.BlockSpec((4, BLOCK, 2 * BLOCK), lambda r, n: (g, 0, 0)),
                  cur_o, cur_o, cur_o, cur_o],
        out_specs=[cur_o, lag_o, lag_o, pl.BlockSpec((4, BLOCK, 2 * BLOCK), lambda r, n: (0, 0, 0))],
        out_shape=[jax.ShapeDtypeStruct((L, d * A_GROUP_WIDTH), BF16)] * 3
        + [jax.ShapeDtypeStruct((4, BLOCK, 2 * BLOCK), F32)],
        scratch_shapes=[pltpu.VMEM(blk, F32), pltpu.VMEM(blk, F32)],
        compiler_params=_params("arbitrary", "arbitrary"),
    )(pv, pv, pv, pv, pv, bias_all, out, lse, view(dout), view(dlse))
    return dq.reshape(T, -1), dk.reshape(T, -1), dv.reshape(T, -1), db


def _a_merge(outs, lses):
    m = jnp.maximum(jnp.maximum(lses[0], lses[1]), lses[2])
    e = [jnp.exp(l - m) for l in lses]
    inv = 1.0 / (e[0] + e[1] + e[2])
    return (e[0] * outs[0] + e[1] * outs[1] + e[2] * outs[2]) * inv


def _a_merge_fwd(outs, lses, name):
    T = outs[0].shape[0]
    tm = _tile(T, (512, 256, 128))

    def body(o0, o1, o2, l0, l1, l2, y_ref):
        y_ref[...] = _a_merge([o0[...], o1[...], o2[...]], [l0[...], l1[...], l2[...]]).astype(BF16)

    row = pl.BlockSpec((tm, A_GROUP_WIDTH), lambda i: (i, 0))
    return pl.pallas_call(
        body, name=name, grid=(T // tm,), in_specs=[row] * 6, out_specs=row,
        out_shape=jax.ShapeDtypeStruct((T, A_GROUP_WIDTH), BF16),
        compiler_params=_params("parallel"),
    )(*outs, *lses)


def _a_merge_bwd(outs, lses, dy, name):
    T = outs[0].shape[0]
    tm = _tile(T, (512, 256, 128))

    def body(o0, o1, o2, l0, l1, l2, dy_ref, *outs_ref):
        _, vjp = jax.vjp(_a_merge, [o0[...], o1[...], o2[...]], [l0[...], l1[...], l2[...]])
        do, dl = vjp(dy_ref[...])
        for ref, val in zip(outs_ref, list(do) + list(dl)):
            ref[...] = val

    row = pl.BlockSpec((tm, A_GROUP_WIDTH), lambda i: (i, 0))
    res = pl.pallas_call(
        body, name=name, grid=(T // tm,), in_specs=[row] * 7, out_specs=[row] * 6,
        out_shape=[jax.ShapeDtypeStruct((T, A_GROUP_WIDTH), F32)] * 6,
        compiler_params=_params("parallel"),
    )(*outs, *lses, dy)
    return res[:3], res[3:]


SB_TQ = 512
SB_KS = 256
SB_UNROLL = 4
SB_Q_COL, SB_K_COL, SB_V_COL = 18, 22, 26


def _sb_tri():
    return jnp.asarray(np.arange(SB_KS)[:, None] > np.arange(SB_KS)[None, :], BF16)


def _logit_parts(m):
    z = m * SB_SCALE
    e = jnp.exp2(jnp.abs(m) * (-SB_SCALE * 1.4426950408889634))
    return z, jnp.maximum(z, 0.0) + jnp.log(1.0 + e)


def _sb_sizes(T):
    nsub = SB_TQ // SB_KS
    assert T % SB_TQ == 0 and SB_TQ % SB_KS == 0 and SB_UNROLL in (nsub, 2 * nsub)
    return SB_TQ, SB_KS, nsub, SB_UNROLL


def _wide(c, width):
    return jnp.concatenate([c] * (width // BLOCK), axis=1)


def _sb_fwd(proj_lo, name):
    T = proj_lo.shape[0]
    TQ, KS, nsub, unroll = _sb_sizes(T)
    nq = T // TQ

    def body(q_ref, k_ref, v_ref, tri_ref, o_ref, c_ref, acc, cs):
        i = pl.program_id(1)
        q = q_ref[...]
        tri = tri_ref[...]
        row = i * TQ + lax.broadcasted_iota(jnp.int32, (TQ, KS), 0)
        col = lax.broadcasted_iota(jnp.int32, (TQ, KS), 1)

        def sub(n, masked, c, o):
            off = pl.multiple_of(n * KS, KS)
            z, nl = _logit_parts(_dot(q, k_ref[pl.ds(off, KS), :], _NT))
            if masked:
                valid = row > n * KS + col
                nl = jnp.where(valid, nl, 0.0)
            w = jnp.exp(z - nl - _dot(nl.astype(BF16), tri, _NN) - _wide(c, KS))
            if masked:
                w = jnp.where(valid, w, 0.0)
            o = o + _dot(w.astype(BF16), v_ref[pl.ds(off, KS), :], _NN)
            return c + jnp.sum(nl, axis=1, keepdims=True), o

        c = jnp.zeros((TQ, BLOCK), F32)
        o = jnp.zeros((TQ, BLOCK), F32)
        for j in reversed(range(nsub)):
            c, o = sub(i * nsub + j, True, c, o)
        cs[...] = c
        acc[...] = o

        def group(top, count):
            c = cs[...]
            o = jnp.zeros((TQ, BLOCK), F32)
            for u in range(count):
                c, o = sub(top - u, False, c, o)
            cs[...] = c
            acc[...] += o

        below = i * nsub
        odd = below % unroll

        @pl.when(odd != 0)
        def _():
            group(below - 1, nsub)

        def step(m, carry):
            group(below - odd - 1 - m * unroll, unroll)
            return carry

        lax.fori_loop(0, below // unroll, step, 0)
        o_ref[...] = acc[...]
        c_ref[...] = cs[...]

    qo = pl.BlockSpec((TQ, BLOCK), lambda h, i: (i, h))
    return pl.pallas_call(
        body, name=name, grid=(SB_HEADS, nq),
        in_specs=[pl.BlockSpec((TQ, BLOCK), lambda h, i: (i, SB_Q_COL + h)),
                  pl.BlockSpec((T, BLOCK), lambda h, i: (0, SB_K_COL + h)),
                  pl.BlockSpec((T, BLOCK), lambda h, i: (0, SB_V_COL + h)),
                  pl.BlockSpec((SB_KS, SB_KS), lambda h, i: (0, 0))],
        out_specs=[qo, qo],
        out_shape=[jax.ShapeDtypeStruct((T, SB_HEADS * BLOCK), F32)] * 2,
        scratch_shapes=[pltpu.VMEM((TQ, BLOCK), F32), pltpu.VMEM((TQ, BLOCK), F32)],
        compiler_params=_params("parallel", "arbitrary"),
    )(proj_lo, proj_lo, proj_lo, _sb_tri())


def _sb_bwd(proj_lo, ctot, dy, name):
    T = proj_lo.shape[0]
    TQ, KS, nsub, unroll = _sb_sizes(T)
    nq = T // TQ

    def body(q_ref, k_ref, v_ref, c_ref, do_ref, tri_ref, dq_ref, dk_hbm, dv_hbm,
             dka, dva, dqa, cps, pgs, sem):
        h = pl.program_id(0)
        i = pl.program_id(1)

        @pl.when(i == 0)
        def _():
            dka[...] = jnp.zeros_like(dka)
            dva[...] = jnp.zeros_like(dva)

        q = q_ref[...]
        tri = tri_ref[...]
        dob = do_ref[...].astype(BF16)
        ctot = c_ref[...]
        row = i * TQ + lax.broadcasted_iota(jnp.int32, (TQ, KS), 0)
        col = lax.broadcasted_iota(jnp.int32, (TQ, KS), 1)

        def sub(n, masked, cp, pg, dq):
            off = pl.multiple_of(n * KS, KS)
            kb = k_ref[pl.ds(off, KS), :]
            z, nl = _logit_parts(_dot(q, kb, _NT))
            if masked:
                valid = row > n * KS + col
                nl = jnp.where(valid, nl, 0.0)
            cp = cp + jnp.sum(nl, axis=1, keepdims=True)
            lb = z - nl
            w = jnp.exp(lb - _dot(nl.astype(BF16), tri, _NN) - _wide(ctot - cp, KS))
            if masked:
                w = jnp.where(valid, w, 0.0)
            g = w * _dot(dob, v_ref[pl.ds(off, KS), :], _NT)
            dz = g - jnp.exp(lb) * (g + _dot(g.astype(BF16), tri, _NT) + _wide(pg, KS))
            if masked:
                dz = jnp.where(valid, dz, 0.0)
            pg = pg + jnp.sum(g, axis=1, keepdims=True)
            dzb = dz.astype(BF16)
            dka[pl.ds(off, KS), :] += _dot(dzb, q, _TN)
            dva[pl.ds(off, KS), :] += _dot(w.astype(BF16), dob, _TN)
            return cp, pg, dq + _dot(dzb, kb, _NN)

        zero = jnp.zeros((TQ, BLOCK), F32)
        cps[...] = zero
        pgs[...] = zero
        dqa[...] = zero

        def group(first, count):
            cp, pg, dq = cps[...], pgs[...], zero
            for u in range(count):
                cp, pg, dq = sub(first + u, False, cp, pg, dq)
            cps[...] = cp
            pgs[...] = pg
            dqa[...] += dq

        below = i * nsub
        odd = below % unroll

        def step(m, carry):
            group(m * unroll, unroll)
            return carry

        lax.fori_loop(0, below // unroll, step, 0)

        @pl.when(odd != 0)
        def _():
            group(below - nsub, nsub)

        cp, pg, dq = cps[...], pgs[...], dqa[...]
        for j in range(nsub):
            cp, pg, dq = sub(i * nsub + j, True, cp, pg, dq)
        dq_ref[...] = (dq * SB_SCALE).astype(BF16)

        @pl.when(i == nq - 1)
        def _():
            def scale_rows(r, carry):
                rows = pl.ds(pl.multiple_of(r * TQ, TQ), TQ)
                dka[rows, :] = dka[rows, :] * SB_SCALE
                return carry

            lax.fori_loop(0, nq, scale_rows, 0)
            ck = pltpu.make_async_copy(dka, dk_hbm.at[h], sem.at[0])
            cv = pltpu.make_async_copy(dva, dv_hbm.at[h], sem.at[1])
            ck.start()
            cv.start()
            ck.wait()
            cv.wait()

    qo = pl.BlockSpec((TQ, BLOCK), lambda h, i: (i, h))
    return pl.pallas_call(
        body, name=name, grid=(SB_HEADS, nq),
        in_specs=[pl.BlockSpec((TQ, BLOCK), lambda h, i: (i, SB_Q_COL + h)),
                  pl.BlockSpec((T, BLOCK), lambda h, i: (0, SB_K_COL + h)),
                  pl.BlockSpec((T, BLOCK), lambda h, i: (0, SB_V_COL + h)),
                  qo, qo,
                  pl.BlockSpec((SB_KS, SB_KS), lambda h, i: (0, 0))],
        out_specs=[qo, pl.BlockSpec(memory_space=pl.ANY), pl.BlockSpec(memory_space=pl.ANY)],
        out_shape=[jax.ShapeDtypeStruct((T, SB_HEADS * BLOCK), BF16),
                   jax.ShapeDtypeStruct((SB_HEADS, T, BLOCK), F32),
                   jax.ShapeDtypeStruct((SB_HEADS, T, BLOCK), F32)],
        scratch_shapes=[pltpu.VMEM((T, BLOCK), F32), pltpu.VMEM((T, BLOCK), F32),
                        pltpu.VMEM((TQ, BLOCK), F32), pltpu.VMEM((TQ, BLOCK), F32),
                        pltpu.VMEM((TQ, BLOCK), F32), pltpu.SemaphoreType.DMA((2,))],
        compiler_params=_params("arbitrary", "arbitrary"),
    )(proj_lo, proj_lo, proj_lo, ctot, dy, _sb_tri())


def _ret_tables(T):
    half = RET_DK // 2
    inv = ROPE_BASE ** (-jnp.arange(half, dtype=F32) / half)
    ang = jnp.arange(T, dtype=jnp.int32).astype(F32)[:, None] * inv[None, :]
    cos, sin = jnp.cos(ang), jnp.sin(ang)
    cos_t = jnp.tile(cos, (1, 2 * RET_HEADS))
    sin_t = jnp.tile(jnp.concatenate([-sin, sin], axis=1), (1, RET_HEADS))
    lg = np.log1p(-np.exp2(-5.0 - np.arange(RET_HEADS)))
    n = np.arange(BLOCK)
    diff = n[:, None] - n[None, :]
    dmat = np.where(diff >= 0, np.exp(np.minimum(diff, BLOCK)[None] * lg[:, None, None]), 0.0)
    zeta = np.repeat(np.exp((BLOCK - 1 - n)[:, None] * lg[None, :]), RET_DK, axis=1)
    xi = np.repeat(np.exp((n + 1)[:, None] * lg[None, :]), RET_DK, axis=1)
    chunk_decay = [float(v) for v in np.exp(BLOCK * lg)]
    return (cos_t, sin_t, jnp.asarray(dmat, F32), jnp.asarray(xi, F32), jnp.asarray(zeta, F32)), chunk_decay


def _swap_halves(x):
    lane = lax.broadcasted_iota(jnp.int32, x.shape, 1)
    lower = (lane % RET_DK) < (RET_DK // 2)
    w = x.shape[1]
    return jnp.where(lower, pltpu.roll(x, w - RET_DK // 2, 1), pltpu.roll(x, RET_DK // 2, 1))


def _gn_gate(o, rg, w):
    mu = jnp.mean(o, axis=-1, keepdims=True)
    xc = o - mu
    var = jnp.mean(xc * xc, axis=-1, keepdims=True)
    return (rg * jax.nn.sigmoid(rg)) * (xc * lax.rsqrt(var + EPS) * w)


def _ret_fwd(proj_hi, ret_norm_w, tables, chunk_decay, name):
    T = proj_hi.shape[0]
    nc = T // BLOCK
    cos_t, sin_t, dmat, xi, zeta = tables

    def body(rq_ref, rk_ref, rv_ref, rg_ref, w_ref, cos_ref, sin_ref, d_ref, xi_ref, ze_ref,
             y_ref, o_ref, st_ref, rs):
        @pl.when(pl.program_id(0) == 0)
        def _():
            rs[...] = jnp.zeros_like(rs)

        cos, sin = cos_ref[...], sin_ref[...]
        rq, rk = rq_ref[...], rk_ref[...]
        q = rq * cos + _swap_halves(rq) * sin
        k = (rk * cos + _swap_halves(rk) * sin) * RET_SCALE
        qb, kb = q.astype(BF16), k.astype(BF16)
        qx, kz = (q * xi_ref[...]).astype(BF16), (k * ze_ref[...]).astype(BF16)
        heads = range(RET_HEADS)
        sl = [slice(h * RET_DK, (h + 1) * RET_DK) for h in heads]
        sv = [slice(h * RET_DV, (h + 1) * RET_DV) for h in heads]
        vb = [rv_ref[:, sv[h]].astype(BF16) for h in heads]
        r = [rs[h] for h in heads]
        intra = [(_dot(qb[:, sl[h]], kb[:, sl[h]], _NT) * d_ref[h]).astype(BF16) for h in heads]
        o = [_dot(intra[h], vb[h], _NN) + _dot(qx[:, sl[h]], r[h].astype(BF16), _NN) for h in heads]
        new_r = [r[h] * chunk_decay[h] + _dot(kz[:, sl[h]], vb[h], _TN) for h in heads]
        for h in heads:
            st_ref[0, h] = r[h]
            rs[h] = new_r[h]
            o_ref[:, sv[h]] = o[h]
            y_ref[:, sv[h]] = _gn_gate(o[h], rg_ref[:, sv[h]], w_ref[:, sv[h]]).astype(BF16)

    qk = (BLOCK, RET_HEADS * RET_DK)
    vv = (BLOCK, RET_HEADS * RET_DV)
    const = lambda shape: pl.BlockSpec(shape, lambda n: (0,) * len(shape))
    return pl.pallas_call(
        body, name=name, grid=(nc,),
        in_specs=[pl.BlockSpec(qk, lambda n: (n, 0)), pl.BlockSpec(qk, lambda n: (n, 1)),
                  pl.BlockSpec(vv, lambda n: (n, 1)), pl.BlockSpec(vv, lambda n: (n, 2)),
                  const((1, RET_HEADS * RET_DV)),
                  pl.BlockSpec(qk, lambda n: (n, 0)), pl.BlockSpec(qk, lambda n: (n, 0)),
                  const((RET_HEADS, BLOCK, BLOCK)), const(qk), const(qk)],
        out_specs=[pl.BlockSpec(vv, lambda n: (n, 0)), pl.BlockSpec(vv, lambda n: (n, 0)),
                   pl.BlockSpec((1, RET_HEADS, RET_DK, RET_DV), lambda n: (n, 0, 0, 0))],
        out_shape=[jax.ShapeDtypeStruct((T, RET_HEADS * RET_DV), BF16),
                   jax.ShapeDtypeStruct((T, RET_HEADS * RET_DV), F32),
                   jax.ShapeDtypeStruct((nc, RET_HEADS, RET_DK, RET_DV), F32)],
        scratch_shapes=[pltpu.VMEM((RET_HEADS, RET_DK, RET_DV), F32)],
        compiler_params=_params("arbitrary"),
    )(proj_hi, proj_hi, proj_hi, proj_hi, ret_norm_w.reshape(1, -1), cos_t, sin_t, dmat, xi, zeta)


def _ret_bwd(proj_hi, ret_norm_w, o_r, states, dy, tables, chunk_decay, name):
    T = proj_hi.shape[0]
    nc = T // BLOCK
    cos_t, sin_t, dmat, xi, zeta = tables

    def body(rq_ref, rk_ref, rv_ref, rg_ref, w_ref, cos_ref, sin_ref, d_ref, xi_ref, ze_ref,
             o_ref, st_ref, dy_ref, dq_ref, dk_ref, dv_ref, dg_ref, dw_ref, drs, dqs, dks):
        @pl.when(pl.program_id(0) == 0)
        def _():
            drs[...] = jnp.zeros_like(drs)
            dw_ref[...] = jnp.zeros_like(dw_ref)

        cos, sin = cos_ref[...], sin_ref[...]
        rq, rk = rq_ref[...], rk_ref[...]
        q = rq * cos + _swap_halves(rq) * sin
        k = (rk * cos + _swap_halves(rk) * sin) * RET_SCALE
        qb, kb = q.astype(BF16), k.astype(BF16)
        qx, kz = (q * xi_ref[...]).astype(BF16), (k * ze_ref[...]).astype(BF16)
        heads = range(RET_HEADS)
        sl = [slice(h * RET_DK, (h + 1) * RET_DK) for h in heads]
        sv = [slice(h * RET_DV, (h + 1) * RET_DV) for h in heads]
        grads = [jax.vjp(_gn_gate, o_ref[:, sv[h]], rg_ref[:, sv[h]], w_ref[:, sv[h]])[1](dy_ref[:, sv[h]])
                 for h in heads]
        dob = [grads[h][0].astype(BF16) for h in heads]
        vb = [rv_ref[:, sv[h]].astype(BF16) for h in heads]
        rb = [st_ref[0, h].astype(BF16) for h in heads]
        dr = [drs[h] for h in heads]
        drb = [dr[h].astype(BF16) for h in heads]
        a = [(_dot(dob[h], vb[h], _NT) * d_ref[h]).astype(BF16) for h in heads]
        p = [(_dot(qb[:, sl[h]], kb[:, sl[h]], _NT) * d_ref[h]).astype(BF16) for h in heads]
        dqh = [_dot(a[h], kb[:, sl[h]], _NN) + _dot(dob[h], rb[h], _NT) * xi_ref[:, sl[h]] for h in heads]
        dkh = [_dot(a[h], qb[:, sl[h]], _TN) + _dot(vb[h], drb[h], _NT) * ze_ref[:, sl[h]] for h in heads]
        dvh = [_dot(p[h], dob[h], _TN) + _dot(kz[:, sl[h]], drb[h], _NN) for h in heads]
        new_dr = [dr[h] * chunk_decay[h] + _dot(qx[:, sl[h]], dob[h], _TN) for h in heads]
        for h in heads:
            dg_ref[:, sv[h]] = grads[h][1].astype(BF16)
            dw_ref[:, sv[h]] += grads[h][2]
            dqs[:, sl[h]] = dqh[h]
            dks[:, sl[h]] = dkh[h]
            dv_ref[:, sv[h]] = dvh[h].astype(BF16)
            drs[h] = new_dr[h]
        dq = dqs[...]
        dk = dks[...] * RET_SCALE
        dq_ref[...] = (dq * cos + _swap_halves(dq * sin)).astype(BF16)
        dk_ref[...] = (dk * cos + _swap_halves(dk * sin)).astype(BF16)

    qk = (BLOCK, RET_HEADS * RET_DK)
    vv = (BLOCK, RET_HEADS * RET_DV)
    rev = lambda n: nc - 1 - n
    const = lambda shape: pl.BlockSpec(shape, lambda n: (0,) * len(shape))
    return pl.pallas_call(
        body, name=name, grid=(nc,),
        in_specs=[pl.BlockSpec(qk, lambda n: (rev(n), 0)), pl.BlockSpec(qk, lambda n: (rev(n), 1)),
                  pl.BlockSpec(vv, lambda n: (rev(n), 1)), pl.BlockSpec(vv, lambda n: (rev(n), 2)),
                  const((1, RET_HEADS * RET_DV)),
                  pl.BlockSpec(qk, lambda n: (rev(n), 0)), pl.BlockSpec(qk, lambda n: (rev(n), 0)),
                  const((RET_HEADS, BLOCK, BLOCK)), const(qk), const(qk),
                  pl.BlockSpec(vv, lambda n: (rev(n), 0)),
                  pl.BlockSpec((1, RET_HEADS, RET_DK, RET_DV), lambda n: (rev(n), 0, 0, 0)),
                  pl.BlockSpec(vv, lambda n: (rev(n), 0))],
        out_specs=[pl.BlockSpec(qk, lambda n: (rev(n), 0)), pl.BlockSpec(qk, lambda n: (rev(n), 0)),
                   pl.BlockSpec(vv, lambda n: (rev(n), 0)), pl.BlockSpec(vv, lambda n: (rev(n), 0)),
                   const((1, RET_HEADS * RET_DV))],
        out_shape=[jax.ShapeDtypeStruct((T, RET_HEADS * RET_DK), BF16)] * 2
        + [jax.ShapeDtypeStruct((T, RET_HEADS * RET_DV), BF16)] * 2
        + [jax.ShapeDtypeStruct((1, RET_HEADS * RET_DV), F32)],
        scratch_shapes=[pltpu.VMEM((RET_HEADS, RET_DK, RET_DV), F32), pltpu.VMEM(qk, F32), pltpu.VMEM(qk, F32)],
        compiler_params=_params("arbitrary"),
    )(proj_hi, proj_hi, proj_hi, proj_hi, ret_norm_w.reshape(1, -1), cos_t, sin_t, dmat, xi, zeta,
      o_r, states, dy)


GATE_BLK = 512
GATE_FIRST_BLK = 3


def _gated(g0, g1, g2, b0, b1, b2, pa, pb, pc):
    return jax.nn.sigmoid(g0 + b0) * pa + jax.nn.sigmoid(g1 + b1) * pb + jax.nn.sigmoid(g2 + b2) * pc


def _gate_specs(tm):
    return [pl.BlockSpec((tm, GATE_BLK), functools.partial(lambda i, c: (i, c), c=GATE_FIRST_BLK + j))
            for j in range(6)]


def _gate_args(g, bg_ref):
    gi = [jnp.concatenate([g[2 * j][...], g[2 * j + 1][...]], axis=1) for j in range(3)]
    return gi + [bg_ref[j:j + 1, :] for j in range(3)]


def _merge_fwd(proj_hi, b_gate, pa, pb, pc, name):
    T = proj_hi.shape[0]
    tm = _tile(T, (256, 128))

    def body(g0, g1, g2, g3, g4, g5, bg_ref, pa_ref, pb_ref, pc_ref, o_ref):
        args = _gate_args((g0, g1, g2, g3, g4, g5), bg_ref)
        o_ref[...] = _gated(*args, pa_ref[...], pb_ref[...], pc_ref[...]).astype(BF16)

    row = pl.BlockSpec((tm, D_MODEL), lambda i: (i, 0))
    return pl.pallas_call(
        body, name=name, grid=(T // tm,),
        in_specs=_gate_specs(tm) + [pl.BlockSpec((3, D_MODEL), lambda i: (0, 0)), row, row, row],
        out_specs=row, out_shape=jax.ShapeDtypeStruct((T, D_MODEL), BF16),
        compiler_params=_params("parallel"),
    )(*([proj_hi] * 6), b_gate, pa, pb, pc)


def _merge_bwd(proj_hi, b_gate, pa, pb, pc, dm, name):
    T = proj_hi.shape[0]
    tm = _tile(T, (256, 128))

    def body(g0, g1, g2, g3, g4, g5, bg_ref, pa_ref, pb_ref, pc_ref, dm_ref,
             dgi_ref, dbg_ref, dpa_ref, dpb_ref, dpc_ref):
        args = _gate_args((g0, g1, g2, g3, g4, g5), bg_ref)
        _, vjp = jax.vjp(_gated, *args, pa_ref[...], pb_ref[...], pc_ref[...])
        d = vjp(dm_ref[...])
        for j in range(3):
            dgi_ref[:, j * D_MODEL:(j + 1) * D_MODEL] = d[j].astype(BF16)
        dpa_ref[...] = d[6].astype(BF16)
        dpb_ref[...] = d[7].astype(BF16)
        dpc_ref[...] = d[8].astype(BF16)

        @pl.when(pl.program_id(0) == 0)
        def _():
            dbg_ref[...] = jnp.zeros_like(dbg_ref)

        for j in range(3):
            dbg_ref[j:j + 1, :] += d[3 + j]

    row = pl.BlockSpec((tm, D_MODEL), lambda i: (i, 0))
    vec = pl.BlockSpec((3, D_MODEL), lambda i: (0, 0))
    return pl.pallas_call(
        body, name=name, grid=(T // tm,),
        in_specs=_gate_specs(tm) + [vec, row, row, row, row],
        out_specs=[pl.BlockSpec((tm, 3 * D_MODEL), lambda i: (i, 0)), vec, row, row, row],
        out_shape=[jax.ShapeDtypeStruct((T, 3 * D_MODEL), BF16), jax.ShapeDtypeStruct((3, D_MODEL), F32)]
        + [jax.ShapeDtypeStruct((T, D_MODEL), BF16)] * 3,
        compiler_params=_params("arbitrary"),
    )(*([proj_hi] * 6), b_gate, pa, pb, pc, dm)


FFN_TM = 256


def _gelu(u):
    return 0.5 * u * (1.0 + jnp.tanh(0.7978845608028654 * (u + 0.044715 * (u * u * u))))


def _conv_taps(u0, prev8, first):
    prev8 = jnp.where(first, 0.0, prev8)
    row = lax.broadcasted_iota(jnp.int32, u0.shape, 0)
    s1 = jnp.where(row == 0, prev8[7:8], pltpu.roll(u0, 1, 0))
    s2 = jnp.where(row == 0, prev8[6:7], jnp.where(row == 1, prev8[7:8], pltpu.roll(u0, 2, 0)))
    return s1, s2


def _ffn_specs(T, tm):
    row = pl.BlockSpec((tm, D_FF), lambda i: (i, 0))
    prev = pl.BlockSpec((8, D_FF), lambda i: (jnp.maximum(i * (tm // 8) - 1, 0), 0))
    nxt = pl.BlockSpec((8, D_FF), lambda i: (jnp.minimum((i + 1) * (tm // 8), T // 8 - 1), 0))
    return row, prev, nxt


def _ffn_mid_fwd(u0, gt, cw, cb, name):
    T = u0.shape[0]
    tm = _tile(T, (FFN_TM, 128))
    row, prev, _ = _ffn_specs(T, tm)

    def body(u_ref, p_ref, g_ref, cw_ref, cb_ref, f_ref):
        u0 = u_ref[...]
        s1, s2 = _conv_taps(u0, p_ref[...], pl.program_id(0) == 0)
        cw = cw_ref[...]
        u = cw[0:1] * s2 + cw[1:2] * s1 + cw[2:3] * u0 + cb_ref[...]
        f_ref[...] = (_gelu(u) * g_ref[...]).astype(BF16)

    return pl.pallas_call(
        body, name=name, grid=(T // tm,),
        in_specs=[row, prev, row, pl.BlockSpec((3, D_FF), lambda i: (0, 0)), pl.BlockSpec((1, D_FF), lambda i: (0, 0))],
        out_specs=row, out_shape=jax.ShapeDtypeStruct((T, D_FF), BF16),
        compiler_params=_params("parallel"),
    )(u0, u0, gt, cw, cb.reshape(1, D_FF))


def _ffn_mid_bwd_a(u0, gt, cw, cb, df, name):
    T = u0.shape[0]
    tm = _tile(T, (FFN_TM, 128))
    row, prev, _ = _ffn_specs(T, tm)

    def body(u_ref, p_ref, g_ref, cw_ref, cb_ref, df_ref, du_ref, dg_ref, dcw_ref, dcb_ref):
        u0 = u_ref[...]
        s1, s2 = _conv_taps(u0, p_ref[...], pl.program_id(0) == 0)
        cw = cw_ref[...]
        u = cw[0:1] * s2 + cw[1:2] * s1 + cw[2:3] * u0 + cb_ref[...]
        a, vjp = jax.vjp(_gelu, u)
        df = df_ref[...]
        dg_ref[...] = (df * a).astype(BF16)
        du = vjp(df * g_ref[...])[0]
        du_ref[...] = du

        @pl.when(pl.program_id(0) == 0)
        def _():
            dcw_ref[...] = jnp.zeros_like(dcw_ref)
            dcb_ref[...] = jnp.zeros_like(dcb_ref)

        dcw_ref[0:1, :] += jnp.sum(du * s2, axis=0, keepdims=True)
        dcw_ref[1:2, :] += jnp.sum(du * s1, axis=0, keepdims=True)
        dcw_ref[2:3, :] += jnp.sum(du * u0, axis=0, keepdims=True)
        dcb_ref[...] += jnp.sum(du, axis=0, keepdims=True)

    c3 = pl.BlockSpec((3, D_FF), lambda i: (0, 0))
    c1 = pl.BlockSpec((1, D_FF), lambda i: (0, 0))
    return pl.pallas_call(
        body, name=name, grid=(T // tm,),
        in_specs=[row, prev, row, c3, c1, row], out_specs=[row, row, c3, c1],
        out_shape=[jax.ShapeDtypeStruct((T, D_FF), F32), jax.ShapeDtypeStruct((T, D_FF), BF16),
                   jax.ShapeDtypeStruct((3, D_FF), F32), jax.ShapeDtypeStruct((1, D_FF), F32)],
        compiler_params=_params("arbitrary"),
    )(u0, u0, gt, cw, cb.reshape(1, D_FF), df)


def _ffn_mid_bwd_b(du, cw, name):
    T = du.shape[0]
    tm = _tile(T, (FFN_TM, 128))
    row, _, nxt = _ffn_specs(T, tm)

    def body(du_ref, n_ref, cw_ref, o_ref):
        du = du_ref[...]
        nx = jnp.where(pl.program_id(0) == T // tm - 1, 0.0, n_ref[...])
        r = lax.broadcasted_iota(jnp.int32, du.shape, 0)
        u1 = jnp.where(r == tm - 1, nx[0:1], pltpu.roll(du, tm - 1, 0))
        u2 = jnp.where(r == tm - 1, nx[1:2], jnp.where(r == tm - 2, nx[0:1], pltpu.roll(du, tm - 2, 0)))
        cw = cw_ref[...]
        o_ref[...] = (cw[2:3] * du + cw[1:2] * u1 + cw[0:1] * u2).astype(BF16)

    return pl.pallas_call(
        body, name=name, grid=(T // tm,),
        in_specs=[row, nxt, pl.BlockSpec((3, D_FF), lambda i: (0, 0))],
        out_specs=row, out_shape=jax.ShapeDtypeStruct((T, D_FF), BF16),
        compiler_params=_params("parallel"),
    )(du, du, cw)


def _mesh_peers():
    x, y, c = lax.axis_index("x"), lax.axis_index("y"), lax.axis_index("c")
    peers = []
    for k in range(1, N_DEV):
        px = 1 - x if k & 4 else x
        py = 1 - y if k & 2 else y
        pc = 1 - c if k & 1 else c
        peers.append(((px, py, pc), 4 * px + 2 * py + pc))
    return 4 * x + 2 * y + c, peers


def _exchange(src, name, scatter):
    shape = src.shape[-2:]

    def body(s_ref, o_ref, send_sems, recv_sems, local_sem):
        me, peers = _mesh_peers()
        mine = pltpu.make_async_copy(s_ref.at[me] if scatter else s_ref, o_ref.at[me], local_sem)
        mine.start()
        sends, recvs = [], []
        for k, (dev, idx) in enumerate(peers):
            sends.append(pltpu.make_async_remote_copy(
                src_ref=s_ref.at[idx] if scatter else s_ref, dst_ref=o_ref.at[me],
                send_sem=send_sems.at[k], recv_sem=recv_sems.at[k],
                device_id=dev, device_id_type=pl.DeviceIdType.MESH))
            recvs.append(pltpu.make_async_remote_copy(
                src_ref=s_ref.at[idx] if scatter else s_ref, dst_ref=o_ref.at[idx],
                send_sem=send_sems.at[k], recv_sem=recv_sems.at[k],
                device_id=dev, device_id_type=pl.DeviceIdType.MESH))
        for cp in sends:
            cp.start()
        for cp in recvs:
            cp.wait_recv()
        for cp in sends:
            cp.wait_send()
        mine.wait()

    return pl.pallas_call(
        body, name=name,
        in_specs=[pl.BlockSpec(memory_space=pl.ANY)], out_specs=pl.BlockSpec(memory_space=pl.ANY),
        out_shape=jax.ShapeDtypeStruct((N_DEV,) + shape, src.dtype),
        scratch_shapes=[pltpu.SemaphoreType.DMA((N_DEV - 1,)), pltpu.SemaphoreType.DMA((N_DEV - 1,)),
                        pltpu.SemaphoreType.DMA],
    )(src)


_HBM = pl.BlockSpec(memory_space=pltpu.HBM)
_SEM = pl.BlockSpec(memory_space=pltpu.SEMAPHORE)
_DATAFLOW = pltpu.SideEffectType.DATAFLOW_SIDE_EFFECTING


def _split_copies(s_ref, land_ref, send_sems, recv_sems, scatter, landing_row):
    me, peers = _mesh_peers()
    return [pltpu.make_async_remote_copy(
        src_ref=s_ref.at[idx] if scatter else s_ref, dst_ref=land_ref.at[landing_row(me, idx)],
        send_sem=send_sems.at[k], recv_sem=recv_sems.at[k],
        device_id=dev, device_id_type=pl.DeviceIdType.MESH) for k, (dev, idx) in enumerate(peers)]


def _exchange_start(src, name, scatter):
    land = lax.empty((N_DEV,) + src.shape[-2:], src.dtype)

    def body(s_ref, land_ref, send_sems, recv_sems, s_thru, land_thru, token):
        for cp in _split_copies(s_ref, land_ref, send_sems, recv_sems, scatter, lambda me, idx: me):
            cp.start()
        token[...] = jnp.zeros_like(token)

    return pl.pallas_call(
        body, name=name,
        out_shape=(pltpu.SemaphoreType.DMA((N_DEV - 1,)), pltpu.SemaphoreType.DMA((N_DEV - 1,)),
                   pltpu.HBM(src.shape, src.dtype), pltpu.HBM(land.shape, land.dtype),
                   jax.ShapeDtypeStruct((8, 128), F32)),
        in_specs=(_HBM, _HBM), out_specs=(_SEM, _SEM, _HBM, _HBM, pl.BlockSpec(memory_space=pltpu.VMEM)),
        input_output_aliases={0: 2, 1: 3},
        compiler_params=pltpu.CompilerParams(has_side_effects=_DATAFLOW),
    )(pltpu.with_memory_space_constraint(src, pltpu.HBM), pltpu.with_memory_space_constraint(land, pltpu.HBM))


def _exchange_wait(started, after, name, scatter):
    send_sems, recv_sems, s_thru, land_thru, _ = started

    def body(s_ref, land_ref, send_sems, recv_sems, after_ref, s_out, land_out):
        for cp in _split_copies(s_ref, land_ref, send_sems, recv_sems, scatter, lambda me, idx: idx):
            cp.wait_send()
            cp.wait_recv()

    src, got = pl.pallas_call(
        body, name=name,
        out_shape=(pltpu.HBM(s_thru.shape, s_thru.dtype), pltpu.HBM(land_thru.shape, land_thru.dtype)),
        in_specs=(_HBM, _HBM, _SEM, _SEM, pl.BlockSpec(memory_space=pl.ANY)), out_specs=(_HBM, _HBM),
        input_output_aliases={0: 0, 1: 1},
        compiler_params=pltpu.CompilerParams(has_side_effects=_DATAFLOW),
    )(s_thru, land_thru, send_sems, recv_sems, after)
    me = 4 * lax.axis_index("x") + 2 * lax.axis_index("y") + lax.axis_index("c")
    own = lax.dynamic_index_in_dim(src, me, 0, keepdims=True) if scatter else src[None]
    return lax.dynamic_update_slice(got, own, (me, 0, 0))


def _sum_devices(parts, name):
    _, R, C = parts.shape
    tr = _tile(R, (512, 480, 256, 240, 128, 64, 32, 16, 8))

    def body(p_ref, o_ref):
        acc = p_ref[0].astype(F32)
        for j in range(1, N_DEV):
            acc = acc + p_ref[j].astype(F32)
        o_ref[...] = acc

    return pl.pallas_call(
        body, name=name, grid=(R // tr,),
        in_specs=[pl.BlockSpec((N_DEV, tr, C), lambda i: (0, i, 0))],
        out_specs=pl.BlockSpec((tr, C), lambda i: (i, 0)),
        out_shape=jax.ShapeDtypeStruct((R, C), F32),
        compiler_params=_params("parallel"),
    )(parts)


def _adamw(w, g, m, v, name):
    shape = w.shape
    C = shape[-1]
    R = int(np.prod(shape[:-1])) if len(shape) > 1 else 1
    tr = _tile(R, (256, 128, 64, 32, 16, 8))

    def body(w_ref, g_ref, m_ref, v_ref, d_ref, nm_ref, nv_ref):
        g = g_ref[...]
        m = ADAM_B1 * m_ref[...] + (1.0 - ADAM_B1) * g
        v = ADAM_B2 * v_ref[...] + (1.0 - ADAM_B2) * (g * g)
        m_hat = m / (1.0 - ADAM_B1 ** ADAM_STEP)
        v_hat = v / (1.0 - ADAM_B2 ** ADAM_STEP)
        d_ref[...] = -ADAM_LR * (m_hat / (jnp.sqrt(v_hat) + ADAM_EPS) + ADAM_WD * w_ref[...])
        nm_ref[...] = m
        nv_ref[...] = v

    blk = pl.BlockSpec((tr, C), lambda i: (i, 0))
    outs = pl.pallas_call(
        body, name=name, grid=(R // tr,), in_specs=[blk] * 4, out_specs=[blk] * 3,
        out_shape=[jax.ShapeDtypeStruct((R, C), F32)] * 3,
        compiler_params=_params("parallel"),
    )(*[t.reshape(R, C) for t in (w, g, m, v)])
    return [o.reshape(shape) for o in outs]


PACK_ROWS = (1056, 352, 352, 128, 352, 32, 64, 64)
PACK_LAYER = sum(PACK_ROWS)
SMALL_SIZES = (("rel_bias", 384), ("norm_mix_w", 4096), ("ret_norm_w", 2048), ("norm_ffn_w", 4096),
               ("conv_b", 11264), ("final_norm_w", 1024), ("b_gate", 12288), ("conv_w", 33792))
SMALL_ROWS = 72


def _layer_forward(x, wl, tabs, l):
    sv = {"x_in": x}
    h = _rms_fwd(x, wl["norm_mix_w"], f"rms_mix_fwd_{l}")
    lo = _matmul(h, wl["win_lo"], "nt", f"in_proj_lo_{l}", out_dtype=BF16)
    hi = _matmul(h, wl["win_hi"], "nt", f"in_proj_hi_{l}")
    outs, lses, a_views = [], [], []
    for g in range(3):
        o, s, views = _a_attn_fwd(lo, tabs["bias"], g, f"a_fwd_{g}_{l}")
        outs.append(o)
        lses.append(s)
        a_views.append(views)
    y_a = _a_merge_fwd(outs, lses, f"a_merge_fwd_{l}")
    y_b, ctot = _sb_fwd(lo, f"sb_fwd_{l}")
    y_c, o_r, states = _ret_fwd(hi, wl["ret_norm_w"], tabs["ret"], tabs["decay"], f"ret_fwd_{l}")
    pa = _matmul(y_a, wl["wpa"], "nt", f"proj_a_{l}")
    pb = _matmul(y_b, wl["wpb"], "nt", f"proj_b_{l}")
    pc = _matmul(y_c, wl["wpc"], "nt", f"proj_c_{l}")
    merged = _merge_fwd(hi, wl["b_gate"], pa, pb, pc, f"merge_fwd_{l}")
    x_mid = _matmul(merged, wl["wout"], "nn", f"out_proj_{l}", add=x)
    h2 = _rms_fwd(x_mid, wl["norm_ffn_w"], f"rms_ffn_fwd_{l}")
    u0 = _matmul(h2, wl["wup"], "nt", f"ffn_up_{l}")
    gt = _matmul(h2, wl["wgate"], "nt", f"ffn_gate_{l}")
    f = _ffn_mid_fwd(u0, gt, wl["conv_w"], wl["conv_b"], f"ffn_mid_fwd_{l}")
    x_out = _matmul(f, wl["wdown"], "nn", f"ffn_down_{l}", add=x_mid)
    sv.update(h=h, lo=lo, hi=hi, outs=outs, lses=lses, a_views=a_views, y_a=y_a, y_b=y_b, ctot=ctot, y_c=y_c, o_r=o_r,
              states=states, pa=pa, pb=pb, pc=pc, merged=merged, x_mid=x_mid, h2=h2, u0=u0, gt=gt, f=f)
    return x_out, sv


def _layer_backward(dx, sv, wl, tabs, l):
    df = _matmul(dx, wl["wdown"], "nt", f"ffn_down_dx_{l}")
    d_wdown = _matmul(sv["f"], dx, "tn", f"ffn_down_dw_{l}")
    du, dgt, d_cw, d_cb = _ffn_mid_bwd_a(sv["u0"], sv["gt"], wl["conv_w"], wl["conv_b"], df, f"ffn_mid_bwd_a_{l}")
    du0 = _ffn_mid_bwd_b(du, wl["conv_w"], f"ffn_mid_bwd_b_{l}")
    dh2 = _matmul(du0, wl["wup"], "nn", f"ffn_up_dx_{l}")
    dh2 = _matmul(dgt, wl["wgate"], "nn", f"ffn_gate_dx_{l}", add=dh2)
    d_wup = _matmul(du0, sv["h2"], "tn", f"ffn_up_dw_{l}")
    d_wgate = _matmul(dgt, sv["h2"], "tn", f"ffn_gate_dw_{l}")
    dx_mid, d_nffn = _rms_bwd(sv["x_mid"], wl["norm_ffn_w"], dh2, dx, f"rms_ffn_bwd_{l}")
    dm = _matmul(dx_mid, wl["wout"], "nt", f"out_proj_dx_{l}")
    d_wout = _matmul(sv["merged"], dx_mid, "tn", f"out_proj_dw_{l}")
    dgi, d_bg, dpa, dpb, dpc = _merge_bwd(sv["hi"], wl["b_gate"], sv["pa"], sv["pb"], sv["pc"], dm, f"merge_bwd_{l}")
    dy_a = _matmul(dpa, wl["wpa"], "nn", f"proj_a_dx_{l}")
    dy_b = _matmul(dpb, wl["wpb"], "nn", f"proj_b_dx_{l}")
    dy_c = _matmul(dpc, wl["wpc"], "nn", f"proj_c_dx_{l}")
    d_wpa = _matmul(dpa, sv["y_a"], "tn", f"proj_a_dw_{l}")
    d_wpb = _matmul(dpb, sv["y_b"], "tn", f"proj_b_dw_{l}")
    d_wpc = _matmul(dpc, sv["y_c"], "tn", f"proj_c_dw_{l}")
    d_rq, d_rk, d_rv, d_rg, d_rnw = _ret_bwd(sv["hi"], wl["ret_norm_w"], sv["o_r"], sv["states"], dy_c,
                                             tabs["ret"], tabs["decay"], f"ret_bwd_{l}")
    d_sq, d_sk, d_sv = _sb_bwd(sv["lo"], sv["ctot"], dy_b, f"sb_bwd_{l}")
    douts, dlses = _a_merge_bwd(sv["outs"], sv["lses"], dy_a, f"a_merge_bwd_{l}")
    dqs, dks, dvs, dbs = [], [], [], []
    for g in range(3):
        dq, dk, dv, db = _a_attn_bwd(sv["lo"], tabs["bias"], *sv["a_views"][g], douts[g], dlses[g],
                                     g, f"a_bwd_{g}_{l}")
        dqs.append(dq)
        dks.append(dk)
        dvs.append(dv)
        dbs.append(db)
    heads = lambda t: [t[i].astype(BF16) for i in range(SB_HEADS)]
    a_cols = [t[g] for g in range(3) for t in (dqs, dks, dvs)]
    dlo = jnp.concatenate(a_cols + [d_sq] + heads(d_sk) + heads(d_sv), axis=1)
    dhi = jnp.concatenate([d_rq, d_rk, d_rv, d_rg, dgi], axis=1)
    dh = _matmul(dlo, wl["win_lo"], "nn", f"in_proj_lo_dx_{l}")
    dh = _matmul(dhi, wl["win_hi"], "nn", f"in_proj_hi_dx_{l}", add=dh)
    d_win = jnp.concatenate([_group_major(_matmul(dlo, sv["h"], "tn", f"in_proj_lo_dw_{l}")),
                             _matmul(dhi, sv["h"], "tn", f"in_proj_hi_dw_{l}")], axis=0)
    dx_in, d_nmix = _rms_bwd(sv["x_in"], wl["norm_mix_w"], dh, dx_mid, f"rms_mix_bwd_{l}")
    chunks = lambda t: t.reshape(N_DEV, -1, D_MODEL)
    big = jnp.concatenate([chunks(d_win), chunks(d_wup), chunks(d_wgate), chunks(d_wout), chunks(d_wdown),
                           chunks(d_wpa), chunks(d_wpb), chunks(d_wpc)], axis=1)
    small = dict(norm_mix_w=d_nmix[0], ret_norm_w=d_rnw[0], norm_ffn_w=d_nffn[0], conv_b=d_cb[0],
                 b_gate=d_bg, conv_w=d_cw, dbias=jnp.concatenate(dbs, axis=0))
    return dx_in, big, small


def kernel(x, rel_bias, norm_mix_w, w_in, b_gate, ret_norm_w, w_proj_a, w_proj_b, w_proj_c, w_out, norm_ffn_w, w_up, w_gate, conv_w, conv_b, w_down, final_norm_w, loss_target, m_rel_bias, m_norm_mix_w, m_w_in, m_b_gate, m_ret_norm_w, m_w_proj_a, m_w_proj_b, m_w_proj_c, m_w_out, m_norm_ffn_w, m_w_up, m_w_gate, m_conv_w, m_conv_b, m_w_down, m_final_norm_w, v_rel_bias, v_norm_mix_w, v_w_in, v_b_gate, v_ret_norm_w, v_w_proj_a, v_w_proj_b, v_w_proj_c, v_w_out, v_norm_ffn_w, v_w_up, v_w_gate, v_conv_w, v_conv_b, v_w_down, v_final_norm_w):
    T = x.shape[1]
    me = 4 * lax.axis_index("x") + 2 * lax.axis_index("y") + lax.axis_index("c")

    n_bg, n_cw = b_gate.size, conv_w.size
    tiny = jnp.concatenate([b_gate.reshape(-1), conv_w.reshape(-1), jnp.zeros((8 * D_MODEL - n_bg - n_cw,), F32)])
    tall = _exchange(tiny.reshape(8, D_MODEL), "gather_small_weights", scatter=False).reshape(N_DEV, -1)
    gathers = []
    zero = (tall[0, 0] != tall[0, 0]).astype(BF16)
    for l in range(DEPTH):
        rows = [w_in[l].T, w_up[l].T, w_gate[l].T, w_out[l], w_down[l],
                w_proj_a[l].T.reshape(-1, D_MODEL), w_proj_b[l].T.reshape(-1, D_MODEL),
                w_proj_c[l].T.reshape(-1, D_MODEL)]
        src = jnp.concatenate(rows, axis=0).astype(BF16) + zero
        gathers.append(_exchange_start(src, f"gather_weights_start_{l}", scatter=False))
        token = gathers[-1][4]
        zero = token[0, 0].astype(BF16)
    spread = lambda t, w: t.reshape(N_DEV, DEPTH, 3, w).transpose(1, 2, 0, 3).reshape(DEPTH, 3, N_DEV * w)
    b_gate_full = spread(tall[:, :n_bg], b_gate.shape[-1])
    conv_w_full = spread(tall[:, n_bg:n_bg + n_cw], conv_w.shape[-1])

    def layer_weights(l, wall):
        offs = np.cumsum((0,) + PACK_ROWS)
        seg = lambda j: wall[:, offs[j]:offs[j + 1], :]
        win = seg(0).reshape(-1, D_MODEL)
        unpack = lambda j, k: seg(j).reshape(N_DEV, D_MODEL // N_DEV, k).reshape(D_MODEL, k)
        return dict(win_lo=_group_major(win[:LO_WIDTH]), win_hi=win[LO_WIDTH:], wup=seg(1).reshape(-1, D_MODEL),
                    wgate=seg(2).reshape(-1, D_MODEL), wout=seg(3).reshape(-1, D_MODEL),
                    wdown=seg(4).reshape(-1, D_MODEL), wpa=unpack(5, 256), wpb=unpack(6, 512), wpc=unpack(7, 512),
                    norm_mix_w=norm_mix_w[l], norm_ffn_w=norm_ffn_w[l], ret_norm_w=ret_norm_w[l],
                    b_gate=b_gate_full[l], conv_w=conv_w_full[l], conv_b=conv_b[l])

    buckets = _a_buckets()
    ret_tabs, decay = _ret_tables(T)
    tabs = dict(bias=_a_bias_tables(rel_bias, buckets), ret=ret_tabs, decay=decay)

    xs = x[0]
    saved, wls = [], []
    for l in range(DEPTH):
        wall = _exchange_wait(gathers[l], token if l == 0 else xs, f"gather_weights_wait_{l}", scatter=False)
        wls.append(layer_weights(l, wall))
        xs, sv = _layer_forward(xs, wls[l], tabs, l)
        saved.append(sv)
    loss_tile, dx, d_final = _loss_head(xs, final_norm_w, loss_target[0], "loss_head")
    loss = lax.psum(loss_tile[0, 0], ("x", "y", "c"))

    scatters, smalls = [None] * DEPTH, [None] * DEPTH
    for l in reversed(range(DEPTH)):
        dx, big, smalls[l] = _layer_backward(dx, saved[l], wls[l], tabs, l)
        scatters[l] = _exchange_start(big, f"scatter_grads_start_{l}", scatter=True)
        if l > 0:
            wls[l - 1] = dict(wls[l - 1], conv_b=wls[l - 1]["conv_b"] + scatters[l][4][0, 0])
    grad_x = dx[None]

    mine = jnp.stack([_sum_devices(_exchange_wait(scatters[l], dx, f"scatter_grads_wait_{l}", scatter=True),
                                   f"sum_grads_{l}") for l in range(DEPTH)])
    dbias = smalls[0]["dbias"] + smalls[1]["dbias"] + smalls[2]["dbias"] + smalls[3]["dbias"]
    small_vals = dict(rel_bias=_a_bias_grad(dbias, buckets), final_norm_w=d_final[0])
    for name in ("norm_mix_w", "ret_norm_w", "norm_ffn_w", "conv_b", "b_gate", "conv_w"):
        small_vals[name] = jnp.stack([smalls[l][name] for l in range(DEPTH)])
    flat = jnp.concatenate([small_vals[n].reshape(-1) for n, _ in SMALL_SIZES])
    flat = jnp.concatenate([flat, jnp.zeros((SMALL_ROWS * D_MODEL - flat.shape[0],), F32)])
    sparts = _exchange(flat.reshape(SMALL_ROWS, D_MODEL), "gather_small_grads", scatter=False)
    ssum = _sum_devices(sparts, "sum_small_grads").reshape(-1)

    grads = {}
    off = 0
    for name, size in SMALL_SIZES:
        grads[name] = ssum[off:off + size]
        off += size
    grads["rel_bias"] = grads["rel_bias"].reshape(REL_BUCKETS, 12)
    for name in ("norm_mix_w", "norm_ffn_w"):
        grads[name] = grads[name].reshape(DEPTH, D_MODEL)
    grads["ret_norm_w"] = grads["ret_norm_w"].reshape(DEPTH, -1)
    grads["conv_b"] = grads["conv_b"].reshape(DEPTH, D_FF)
    bw, cwid = b_gate.shape[-1], conv_w.shape[-1]
    grads["b_gate"] = lax.dynamic_slice_in_dim(grads["b_gate"].reshape(DEPTH, 3, -1), me * bw, bw, axis=2)
    grads["conv_w"] = lax.dynamic_slice_in_dim(grads["conv_w"].reshape(DEPTH, 3, -1), me * cwid, cwid, axis=2)

    offs = np.cumsum((0,) + PACK_ROWS)
    per_layer = mine
    seg = lambda j: per_layer[:, offs[j]:offs[j + 1], :]
    back = lambda j, k: seg(j).reshape(DEPTH, D_MODEL // N_DEV, k).transpose(0, 2, 1)
    grads["w_in"] = seg(0).transpose(0, 2, 1)
    grads["w_up"] = seg(1).transpose(0, 2, 1)
    grads["w_gate"] = seg(2).transpose(0, 2, 1)
    grads["w_out"] = seg(3)
    grads["w_down"] = seg(4)
    grads["w_proj_a"] = back(5, 256)
    grads["w_proj_b"] = back(6, 512)
    grads["w_proj_c"] = back(7, 512)

    order = ["rel_bias", "norm_mix_w", "w_in", "b_gate", "ret_norm_w", "w_proj_a", "w_proj_b", "w_proj_c",
             "w_out", "norm_ffn_w", "w_up", "w_gate", "conv_w", "conv_b", "w_down", "final_norm_w"]
    ws = dict(rel_bias=rel_bias, norm_mix_w=norm_mix_w, w_in=w_in, b_gate=b_gate, ret_norm_w=ret_norm_w,
              w_proj_a=w_proj_a, w_proj_b=w_proj_b, w_proj_c=w_proj_c, w_out=w_out, norm_ffn_w=norm_ffn_w,
              w_up=w_up, w_gate=w_gate, conv_w=conv_w, conv_b=conv_b, w_down=w_down, final_norm_w=final_norm_w)
    ms = dict(rel_bias=m_rel_bias, norm_mix_w=m_norm_mix_w, w_in=m_w_in, b_gate=m_b_gate, ret_norm_w=m_ret_norm_w,
              w_proj_a=m_w_proj_a, w_proj_b=m_w_proj_b, w_proj_c=m_w_proj_c, w_out=m_w_out,
              norm_ffn_w=m_norm_ffn_w, w_up=m_w_up, w_gate=m_w_gate, conv_w=m_conv_w, conv_b=m_conv_b,
              w_down=m_w_down, final_norm_w=m_final_norm_w)
    vs = dict(rel_bias=v_rel_bias, norm_mix_w=v_norm_mix_w, w_in=v_w_in, b_gate=v_b_gate, ret_norm_w=v_ret_norm_w,
              w_proj_a=v_w_proj_a, w_proj_b=v_w_proj_b, w_proj_c=v_w_proj_c, w_out=v_w_out,
              norm_ffn_w=v_norm_ffn_w, w_up=v_w_up, w_gate=v_w_gate, conv_w=v_conv_w, conv_b=v_conv_b,
              w_down=v_w_down, final_norm_w=v_final_norm_w)
    deltas, new_m, new_v = [], [], []
    for name in order:
        g = grads[name].reshape(ws[name].shape)
        grads[name] = g
        d, nm, nv = _adamw(ws[name], g, ms[name], vs[name], f"adamw_{name}")
        deltas.append(d)
        new_m.append(nm)
        new_v.append(nv)
    return (loss, grad_x, *[grads[n] for n in order], *deltas, *new_m, *new_v)
```
